```python
import math, functools
import jax, jax.numpy as jnp
from jax import lax
import numpy as np

D_MODEL = 2048
BATCH = 1
SEQ = 8192
DEPTH = 2
DEC_BATCH = 128
DEC_SEQ = 8
PAST_LEN = 2048
PAGE_SIZE = 128

N_ATT_LAYERS = (DEPTH + 1) // 2
N_SSM_LAYERS = DEPTH // 2

FOX_HEADS = 8
FOX_HEAD_DIM = 128
FOX_WIDTH = FOX_HEADS * FOX_HEAD_DIM
Q_BLOCK = 128
FORGET_BIAS_INIT = 3.0
CONF_CH = D_MODEL // 2
CONF_WIDTH = 31
IN_EVEN = 3 * FOX_WIDTH + FOX_HEADS + 2 * CONF_CH
OUT_EVEN = FOX_WIDTH + CONF_CH
SSM_D_INNER = 2 * D_MODEL
SSM_HEAD_DIM = 64
SSM_HEADS = SSM_D_INNER // SSM_HEAD_DIM
SSM_GROUPS = 8
SSM_HPG = SSM_HEADS // SSM_GROUPS
SSM_STATE = 128
SSM_CONV = 4
SSM_CONV_CH = SSM_D_INNER + 2 * SSM_GROUPS * SSM_STATE
IN_ODD = SSM_D_INNER + SSM_CONV_CH + SSM_HEADS
SSM_CHUNK = 128
D_FF = 5632
FFN_CONV = 3
EPS = 1e-6

kernel_name = 'fox_conformer_mamba2_convffn_step'


def rmsnorm(x, g):
    xf = x.astype(jnp.float32)
    y = xf * lax.rsqrt(jnp.mean(xf * xf, axis=-1, keepdims=True) + EPS)
    return (y * g.astype(jnp.float32)).astype(x.dtype)


def layernorm(x, g, b):
    xf = x.astype(jnp.float32)
    mu = jnp.mean(xf, axis=-1, keepdims=True)
    xc = xf - mu
    var = jnp.mean(xc * xc, axis=-1, keepdims=True)
    return (xc * lax.rsqrt(var + EPS) * g.astype(jnp.float32) + b.astype(jnp.float32)).astype(x.dtype)


def group_rmsnorm(y, g, n_groups):
    b, L, d = y.shape
    yf = y.astype(jnp.float32).reshape(b, L, n_groups, d // n_groups)
    yf = yf * lax.rsqrt(jnp.mean(yf * yf, axis=-1, keepdims=True) + EPS)
    return (yf.reshape(b, L, d) * g.astype(jnp.float32)).astype(y.dtype)


def causal_dwconv(x, prev, w, b):
    width = w.shape[0]
    xp = jnp.concatenate([prev.astype(x.dtype), x], axis=1)
    y = lax.conv_general_dilated(xp, w[:, None, :].astype(x.dtype), window_strides=(1,), padding='VALID',
                                 dimension_numbers=('NWC', 'WIO', 'NWC'), feature_group_count=x.shape[-1])
    return y + b.astype(x.dtype), xp[:, xp.shape[1] - (width - 1):]


def fox_scores(q, cq, qpos, k, ck, kpos):
    s = jnp.einsum('bqhd,bkhd->bhqk', q, k, preferred_element_type=jnp.float32) * (FOX_HEAD_DIM ** -0.5)
    s = s + jnp.swapaxes(cq, 1, 2)[..., :, None] - jnp.swapaxes(ck, 1, 2)[..., None, :]
    return jnp.where(kpos[None, None, None, :] <= qpos[None, None, :, None], s, -jnp.inf)


def fox_prompt(q, k, v, logf):
    b, L, h, d = q.shape
    c = jnp.cumsum(logf.astype(jnp.float32), axis=1)
    pos = jnp.arange(L)
    nb = L // Q_BLOCK
    qb = q.reshape(b, nb, Q_BLOCK, h, d).transpose(1, 0, 2, 3, 4)
    cb = c.reshape(b, nb, Q_BLOCK, h).transpose(1, 0, 2, 3)
    pb = pos.reshape(nb, Q_BLOCK)

    def one_block(args):
        qi, ci, pi = args
        p = jax.nn.softmax(fox_scores(qi, ci, pi, k, c, pos), axis=-1)
        return jnp.einsum('bhqk,bkhd->bqhd', p.astype(v.dtype), v)

    o = lax.map(one_block, (qb, cb, pb))
    return o.transpose(1, 0, 2, 3, 4).reshape(b, L, h, d)


def fox_sample(q, k, v, logf, k_past, v_past, logf_past):
    P = k_past.shape[1]
    T = q.shape[1]
    c_past = jnp.cumsum(logf_past.astype(jnp.float32), axis=1)
    c_new = c_past[:, -1:] + jnp.cumsum(logf.astype(jnp.float32), axis=1)
    qpos = P + jnp.arange(T)
    s_past = fox_scores(q, c_new, qpos, k_past, c_past, jnp.arange(P))
    s_new = fox_scores(q, c_new, qpos, k, c_new, qpos)
    p = jax.nn.softmax(jnp.concatenate([s_past, s_new], axis=-1), axis=-1).astype(v.dtype)
    return (jnp.einsum('bhqk,bkhd->bqhd', p[..., :P], v_past)
            + jnp.einsum('bhqk,bkhd->bqhd', p[..., P:], v))


def even_mix(h, conf_prev, attend, w_in, b_f, conf_w, conf_b, conf_g, conf_beta, w_out):
    b, L, _ = h.shape
    z = h @ w_in
    q, k, v, fl, glu = jnp.split(z, [FOX_WIDTH, 2 * FOX_WIDTH, 3 * FOX_WIDTH, 3 * FOX_WIDTH + FOX_HEADS], axis=-1)
    q = q.reshape(b, L, FOX_HEADS, FOX_HEAD_DIM)
    k = k.reshape(b, L, FOX_HEADS, FOX_HEAD_DIM)
    v = v.reshape(b, L, FOX_HEADS, FOX_HEAD_DIM)
    logf = jax.nn.log_sigmoid((fl + b_f).astype(jnp.float32)).astype(h.dtype)
    o_att = attend(q, k, v, logf).reshape(b, L, FOX_WIDTH)
    a, g = jnp.split(glu, 2, axis=-1)
    u = a * jax.nn.sigmoid(g)
    u, conf_state = causal_dwconv(u, conf_prev, conf_w, conf_b)
    o_conf = jax.nn.silu(layernorm(u, conf_g, conf_beta))
    y = jnp.concatenate([o_att, o_conf.astype(o_att.dtype)], axis=-1) @ w_out
    return y, k, v, logf, conf_state


def ssd_scan(x, dt, A, B, C, h0):
    b, L, G, R, P = x.shape
    N = B.shape[-1]
    cs = math.gcd(L, SSM_CHUNK)
    nc = L // cs
    xf = x.astype(jnp.float32).reshape(b, nc, cs, G, R, P)
    Bf = B.astype(jnp.float32).reshape(b, nc, cs, G, N)
    Cf = C.astype(jnp.float32).reshape(b, nc, cs, G, N)
    dtc = dt.reshape(b, nc, cs, G, R)
    acum = jnp.cumsum(dtc * A, axis=2)
    causal = jnp.tril(jnp.ones((cs, cs), dtype=bool))[None, None, :, :, None, None]
    seg = acum[:, :, :, None] - acum[:, :, None, :]
    decay = jnp.exp(jnp.where(causal, seg, -jnp.inf))
    cb = jnp.einsum('bclgn,bcsgn->bclsg', Cf, Bf)
    dx = dtc[..., None] * xf
    y_diag = jnp.einsum('bclsgr,bcsgrp->bclgrp', cb[..., None] * decay, dx)
    decay_end = jnp.exp(acum[:, :, -1:] - acum)
    chunk_states = jnp.einsum('bcsgn,bcsgrp->bcgrpn', Bf, decay_end[..., None] * dx)
    chunk_decay = jnp.exp(acum[:, :, -1])

    def step(hc, inp):
        s_c, d_c = inp
        return hc * d_c[..., None, None] + s_c, hc

    h_last, h_prev = lax.scan(step, h0.astype(jnp.float32),
                              (jnp.moveaxis(chunk_states, 1, 0), jnp.moveaxis(chunk_decay, 1, 0)))
    h_prev = jnp.moveaxis(h_prev, 0, 1)
    y_off = jnp.einsum('bclgn,bcgrpn->bclgrp', Cf, h_prev) * jnp.exp(acum)[..., None]
    y = (y_diag + y_off).reshape(b, L, G, R, P)
    return y.astype(x.dtype), h_last


def odd_mix(h, conv_prev, ssm_prev, w_in, conv_w, conv_b, dt_bias, a_log, d_skip, norm_g, w_out):
    b, L, _ = h.shape
    zxbcdt = h @ w_in
    z, xbc, dt = jnp.split(zxbcdt, [SSM_D_INNER, SSM_D_INNER + SSM_CONV_CH], axis=-1)
    xbc, conv_state = causal_dwconv(xbc, conv_prev, conv_w, conv_b)
    xbc = jax.nn.silu(xbc)
    xs, B, C = jnp.split(xbc, [SSM_D_INNER, SSM_D_INNER + SSM_GROUPS * SSM_STATE], axis=-1)
    xs = xs.reshape(b, L, SSM_GROUPS, SSM_HPG, SSM_HEAD_DIM)
    B = B.reshape(b, L, SSM_GROUPS, SSM_STATE)
    C = C.reshape(b, L, SSM_GROUPS, SSM_STATE)
    dt = jax.nn.softplus((dt + dt_bias).astype(jnp.float32)).reshape(b, L, SSM_GROUPS, SSM_HPG)
    A = -jnp.exp(a_log.astype(jnp.float32)).reshape(SSM_GROUPS, SSM_HPG)
    h0 = ssm_prev.reshape(b, SSM_GROUPS, SSM_HPG, SSM_HEAD_DIM, SSM_STATE)
    y, h_last = ssd_scan(xs, dt, A, B, C, h0)
    y = y + d_skip.reshape(SSM_GROUPS, SSM_HPG)[..., None].astype(xs.dtype) * xs
    y = y.reshape(b, L, SSM_D_INNER) * jax.nn.silu(z)
    y = group_rmsnorm(y, norm_g, SSM_GROUPS)
    ssm_state = h_last.reshape(b, SSM_HEADS, SSM_HEAD_DIM, SSM_STATE).astype(ssm_prev.dtype)
    return y @ w_out, conv_state, ssm_state


def conv_ffn(h, prev, w_up, dw_w, dw_b, w_down):
    u = h @ w_up
    u, state = causal_dwconv(u, prev, dw_w, dw_b)
    a, v = jnp.split(u, 2, axis=-1)
    return (jax.nn.silu(a) * v) @ w_down, state


def run_trunk(x, attend, conf_prev, mconv_prev, ssm_prev, ffn_prev,
              ln_mix, ln_ffn, ln_final,
              att_w_in, att_b_f, conf_dw_w, conf_dw_b, conf_ln_g, conf_ln_b, att_w_out,
              ssm_w_in, ssm_conv_w, ssm_conv_b, ssm_dt_bias, ssm_a_log, ssm_d, ssm_norm_g, ssm_w_out,
              ffn_w_up, ffn_dw_w, ffn_dw_b, ffn_w_down):
    ks, vs, lfs, confs, mconvs, ssms, ffns = [], [], [], [], [], [], []
    for i in range(DEPTH):
        j = i // 2
        h = rmsnorm(x, ln_mix[i])
        if i % 2 == 0:
            y, k, v, lf, cs = even_mix(h, conf_prev[j], functools.partial(attend, j),
                                       att_w_in[j], att_b_f[j], conf_dw_w[j], conf_dw_b[j],
                                       conf_ln_g[j], conf_ln_b[j], att_w_out[j])
            ks.append(k); vs.append(v); lfs.append(lf); confs.append(cs)
        else:
            y, mc, ss = odd_mix(h, mconv_prev[j], ssm_prev[j], ssm_w_in[j], ssm_conv_w[j], ssm_conv_b[j],
                                ssm_dt_bias[j], ssm_a_log[j], ssm_d[j], ssm_norm_g[j], ssm_w_out[j])
            mconvs.append(mc); ssms.append(ss)
        x = x + y
        y, fs = conv_ffn(rmsnorm(x, ln_ffn[i]), ffn_prev[i], ffn_w_up[i], ffn_dw_w[i], ffn_dw_b[i], ffn_w_down[i])
        x = x + y
        ffns.append(fs)
    return (rmsnorm(x, ln_final), jnp.stack(ks), jnp.stack(vs), jnp.stack(lfs), jnp.stack(confs),
            jnp.stack(mconvs), jnp.stack(ssms), jnp.stack(ffns))


def setup_inputs(seed: int = 0) -> dict:
    key = jax.random.key(seed)
    keys = list(jax.random.split(key, 40))

    def nrm(shape, scale=1.0):
        return jax.random.normal(keys.pop(), shape, jnp.float32) * scale

    d = D_MODEL
    n_pages = PAST_LEN // PAGE_SIZE
    n_used = DEC_BATCH * n_pages
    n_pool = n_used + (n_used + 3) // 4
    page_table = jax.random.permutation(keys.pop(), n_pool)[:n_used].reshape(DEC_BATCH, n_pages).astype(jnp.int32)
    dt0 = jnp.exp(jax.random.uniform(keys.pop(), (N_SSM_LAYERS, SSM_HEADS), jnp.float32,
                                     math.log(1e-3), math.log(1e-1)))
    a_log = jnp.log(jax.random.uniform(keys.pop(), (N_SSM_LAYERS, SSM_HEADS), jnp.float32, 1.0, 16.0))
    return {
        'x_prompt': nrm((BATCH, SEQ, d)),
        'x_sample': nrm((DEC_BATCH, DEC_SEQ, d)),
        'cache_k': nrm((N_ATT_LAYERS, n_pool, PAGE_SIZE, FOX_HEADS, FOX_HEAD_DIM)),
        'cache_v': nrm((N_ATT_LAYERS, n_pool, PAGE_SIZE, FOX_HEADS, FOX_HEAD_DIM)),
        'cache_logf': jax.nn.log_sigmoid(FORGET_BIAS_INIT + nrm((N_ATT_LAYERS, n_pool, PAGE_SIZE, FOX_HEADS))),
        'state_conf_conv': nrm((N_ATT_LAYERS, DEC_BATCH, CONF_WIDTH - 1, CONF_CH)),
        'state_ssm_conv': nrm((N_SSM_LAYERS, DEC_BATCH, SSM_CONV - 1, SSM_CONV_CH)),
        'state_ssm': nrm((N_SSM_LAYERS, DEC_BATCH, SSM_HEADS, SSM_HEAD_DIM, SSM_STATE), 0.1),
        'state_ffn_conv': nrm((DEPTH, DEC_BATCH, FFN_CONV - 1, 2 * D_FF)),
        'page_table': page_table,
        'ln_mix': 1.0 + nrm((DEPTH, d), 0.02),
        'ln_ffn': 1.0 + nrm((DEPTH, d), 0.02),
        'ln_final': 1.0 + nrm((d,), 0.02),
        'att_w_in': nrm((N_ATT_LAYERS, d, IN_EVEN), d ** -0.5),
        'att_b_f': FORGET_BIAS_INIT + nrm((N_ATT_LAYERS, FOX_HEADS), 0.5),
        'conf_dw_w': nrm((N_ATT_LAYERS, CONF_WIDTH, CONF_CH), CONF_WIDTH ** -0.5),
        'conf_dw_b': nrm((N_ATT_LAYERS, CONF_CH), 0.02),
        'conf_ln_g': 1.0 + nrm((N_ATT_LAYERS, CONF_CH), 0.02),
        'conf_ln_b': nrm((N_ATT_LAYERS, CONF_CH), 0.02),
        'att_w_out': nrm((N_ATT_LAYERS, OUT_EVEN, d), OUT_EVEN ** -0.5),
        'ssm_w_in': nrm((N_SSM_LAYERS, d, IN_ODD), d ** -0.5),
        'ssm_conv_w': nrm((N_SSM_LAYERS, SSM_CONV, SSM_CONV_CH), SSM_CONV ** -0.5),
        'ssm_conv_b': nrm((N_SSM_LAYERS, SSM_CONV_CH), 0.02),
        'ssm_dt_bias': dt0 + jnp.log(-jnp.expm1(-dt0)),
        'ssm_a_log': a_log,
        'ssm_d': 1.0 + nrm((N_SSM_LAYERS, SSM_HEADS), 0.1),
        'ssm_norm_g': 1.0 + nrm((N_SSM_LAYERS, SSM_D_INNER), 0.02),
        'ssm_w_out': nrm((N_SSM_LAYERS, SSM_D_INNER, d), SSM_D_INNER ** -0.5),
        'ffn_w_up': nrm((DEPTH, d, 2 * D_FF), d ** -0.5),
        'ffn_dw_w': nrm((DEPTH, FFN_CONV, 2 * D_FF), FFN_CONV ** -0.5),
        'ffn_dw_b': nrm((DEPTH, 2 * D_FF), 0.02),
        'ffn_w_down': nrm((DEPTH, D_FF, d), D_FF ** -0.5),
    }


def reference(x_prompt, x_sample, cache_k, cache_v, cache_logf, state_conf_conv, state_ssm_conv, state_ssm,
              state_ffn_conv, page_table,
              ln_mix, ln_ffn, ln_final,
              att_w_in, att_b_f, conf_dw_w, conf_dw_b, conf_ln_g, conf_ln_b, att_w_out,
              ssm_w_in, ssm_conv_w, ssm_conv_b, ssm_dt_bias, ssm_a_log, ssm_d, ssm_norm_g, ssm_w_out,
              ffn_w_up, ffn_dw_w, ffn_dw_b, ffn_w_down):
    weights = (ln_mix, ln_ffn, ln_final,
               att_w_in, att_b_f, conf_dw_w, conf_dw_b, conf_ln_g, conf_ln_b, att_w_out,
               ssm_w_in, ssm_conv_w, ssm_conv_b, ssm_dt_bias, ssm_a_log, ssm_d, ssm_norm_g, ssm_w_out,
               ffn_w_up, ffn_dw_w, ffn_dw_b, ffn_w_down)

    bp = x_prompt.shape[0]
    dtp = x_prompt.dtype
    conf0 = jnp.zeros((N_ATT_LAYERS, bp, CONF_WIDTH - 1, CONF_CH), dtp)
    mconv0 = jnp.zeros((N_SSM_LAYERS, bp, SSM_CONV - 1, SSM_CONV_CH), dtp)
    ssm0 = jnp.zeros((N_SSM_LAYERS, bp, SSM_HEADS, SSM_HEAD_DIM, SSM_STATE), state_ssm.dtype)
    ffn0 = jnp.zeros((DEPTH, bp, FFN_CONV - 1, 2 * D_FF), dtp)

    def attend_prompt(j, q, k, v, lf):
        return fox_prompt(q, k, v, lf)

    def attend_sample(j, q, k, v, lf):
        nb = page_table.shape[0]
        k_past = cache_k[j, page_table].reshape(nb, -1, FOX_HEADS, FOX_HEAD_DIM)
        v_past = cache_v[j, page_table].reshape(nb, -1, FOX_HEADS, FOX_HEAD_DIM)
        lf_past = cache_logf[j, page_table].reshape(nb, -1, FOX_HEADS)
        return fox_sample(q, k, v, lf, k_past, v_past, lf_past)

    (y_prompt, k_prompt, v_prompt, logf_prompt, conf_prompt, ssm_conv_prompt, ssm_prompt,
     ffn_prompt) = run_trunk(x_prompt, attend_prompt, conf0, mconv0, ssm0, ffn0, *weights)
    (y_sample, k_sample, v_sample, logf_sample, conf_sample, ssm_conv_sample, ssm_sample,
     ffn_sample) = run_trunk(x_sample, attend_sample, state_conf_conv, state_ssm_conv, state_ssm,
                             state_ffn_conv, *weights)
    return (y_prompt, y_sample,
            k_prompt, v_prompt, logf_prompt, conf_prompt, ssm_conv_prompt, ssm_prompt, ffn_prompt,
            k_sample, v_sample, logf_sample, conf_sample, ssm_conv_sample, ssm_sample, ffn_sample)
```

```python
import functools

import jax
import jax.numpy as jnp
from jax import lax
from jax.experimental import pallas as pl
from jax.experimental.pallas import tpu as pltpu

EPS = 1e-6
BF = jnp.bfloat16
F32 = jnp.float32
LANES = 128
SUBLANES = 8
VMEM_LIMIT = 56 * 1024 * 1024
NEG_INF = float("-inf")


def _cparams(*sem):
    return pltpu.CompilerParams(dimension_semantics=sem, vmem_limit_bytes=VMEM_LIMIT)


def _tile(n, pref):
    return pref if n % pref == 0 else n


def _iota(shape, dim):
    return lax.broadcasted_iota(jnp.int32, shape, dim)


def _split3(x):
    hi = x.astype(BF)
    r1 = x - hi.astype(F32)
    mid = r1.astype(BF)
    lo = (r1 - mid.astype(F32)).astype(BF)
    return hi, mid, lo


def _dot(a, b):
    return jnp.dot(a, b, preferred_element_type=F32)


def _dot_nt(a, b):
    return lax.dot_general(a, b, (((1,), (1,)), ((), ())), preferred_element_type=F32)


def _sel_left(sel, x, parts=3):
    ps = _split3(x)[:parts]
    out = _dot(sel, ps[0])
    for p in ps[1:]:
        out = out + _dot(sel, p)
    return out


def _sel_right(x, sel, parts=3):
    ps = _split3(x)[:parts]
    out = _dot(ps[0], sel)
    for p in ps[1:]:
        out = out + _dot(p, sel)
    return out


def _log_sigmoid(x):
    return jnp.minimum(x, 0.0) - jnp.log(1.0 + jnp.exp(-jnp.abs(x)))


def _softplus(x):
    return jnp.maximum(x, 0.0) + jnp.log(1.0 + jnp.exp(-jnp.abs(x)))


def _silu(x):
    return x * jax.nn.sigmoid(x)


def _rms(x, g):
    return x * lax.rsqrt(jnp.mean(x * x, axis=-1, keepdims=True) + EPS) * g


def _tri(n):
    return (_iota((n, n), 1) <= _iota((n, n), 0)).astype(BF)


def _even_in_body(x_ref, g_ref, w_ref, wf_ref, bf_ref,
                  q_ref, k_ref, v_ref, u_ref, lf_ref, xn_ref, a_ref, *, nf, nc):
    j = pl.program_id(1)

    @pl.when(j == 0)
    def _():
        xn = _rms(x_ref[...], g_ref[...]).astype(BF)
        xn_ref[...] = xn
        lf_ref[...] = _log_sigmoid(_dot(xn, wf_ref[...]) + bf_ref[...])

    z = _dot(xn_ref[...], w_ref[...])

    @pl.when(j < nf)
    def _():
        q_ref[...] = z

    @pl.when((j >= nf) & (j < 2 * nf))
    def _():
        k_ref[...] = z

    @pl.when((j >= 2 * nf) & (j < 3 * nf))
    def _():
        v_ref[...] = z

    @pl.when((j >= 3 * nf) & ((j - 3 * nf) % 2 == 0))
    def _():
        a_ref[...] = z

    @pl.when((j >= 3 * nf) & ((j - 3 * nf) % 2 == 1))
    def _():
        u_ref[...] = a_ref[...] * jax.nn.sigmoid(z)


def _even_in(x, g, w5, wf, bfp, fw, cc):
    m, d = x.shape
    tm = _tile(m, 512)
    tn = min(_tile(fw, 512), _tile(cc, 512))
    nf, nc = fw // tn, cc // tn
    nj = 3 * nf + 2 * nc

    def w_map(i, j):
        jj = j - 3 * nf
        return (0, jnp.where(j < 3 * nf, j, 3 * nf + (jj % 2) * nc + jj // 2))

    def seg_map(base, n, div=1):
        return lambda i, j: (i, jnp.clip((j - base) // div, 0, n - 1))

    return pl.pallas_call(
        functools.partial(_even_in_body, nf=nf, nc=nc),
        grid=(m // tm, nj),
        in_specs=[
            pl.BlockSpec((tm, d), lambda i, j: (i, 0)),
            pl.BlockSpec((1, d), lambda i, j: (0, 0)),
            pl.BlockSpec((d, tn), w_map),
            pl.BlockSpec((d, LANES), lambda i, j: (0, 0)),
            pl.BlockSpec((1, LANES), lambda i, j: (0, 0)),
        ],
        out_specs=[
            pl.BlockSpec((tm, tn), seg_map(0, nf)),
            pl.BlockSpec((tm, tn), seg_map(nf, nf)),
            pl.BlockSpec((tm, tn), seg_map(2 * nf, nf)),
            pl.BlockSpec((tm, tn), seg_map(3 * nf, nc, 2)),
            pl.BlockSpec((tm, LANES), lambda i, j: (i, 0)),
        ],
        out_shape=[
            jax.ShapeDtypeStruct((m, fw), F32),
            jax.ShapeDtypeStruct((m, fw), F32),
            jax.ShapeDtypeStruct((m, fw), F32),
            jax.ShapeDtypeStruct((m, cc), F32),
            jax.ShapeDtypeStruct((m, LANES), F32),
        ],
        scratch_shapes=[pltpu.VMEM((tm, d), BF), pltpu.VMEM((tm, tn), F32)],
        compiler_params=_cparams("parallel", "arbitrary"),
        name="even_in",
    )(x, g, w5, wf, bfp)


def _fox_cumsum_body(lf_ref, cq_ref, ckt_ref, carry_ref, *, nh):
    @pl.when(pl.program_id(0) == 0)
    def _():
        carry_ref[...] = jnp.zeros_like(carry_ref)

    lf = lf_ref[...]
    tb = lf.shape[0]
    cs = _sel_left(_tri(tb), lf) + carry_ref[...]
    carry_ref[...] = cs[tb - 1:tb, :]
    for h in range(nh):
        cq_ref[h] = jnp.broadcast_to(cs[:, h:h + 1], (tb, LANES))
    ckt_ref[...] = cs.T[:nh, :]


def _fox_cumsum(lf, nh):
    m = lf.shape[0]
    tb = _tile(m, 256)
    return pl.pallas_call(
        functools.partial(_fox_cumsum_body, nh=nh),
        grid=(m // tb,),
        in_specs=[pl.BlockSpec((tb, LANES), lambda i: (i, 0))],
        out_specs=[pl.BlockSpec((nh, tb, LANES), lambda i: (0, i, 0)),
                   pl.BlockSpec((nh, tb), lambda i: (0, i))],
        out_shape=[jax.ShapeDtypeStruct((nh, m, LANES), F32),
                   jax.ShapeDtypeStruct((nh, m), F32)],
        scratch_shapes=[pltpu.VMEM((1, LANES), F32)],
        compiler_params=_cparams("arbitrary"),
        name="fox_cumsum",
    )(lf)


def _fox_prompt_body(q_ref, k_ref, v_ref, cq_ref, ck_ref, o_ref, m_ref, l_ref, acc_ref, *, scale):
    qi = pl.program_id(1)
    kj = pl.program_id(2)
    tq = q_ref.shape[0]
    tk = k_ref.shape[0]

    @pl.when(kj == 0)
    def _():
        m_ref[...] = jnp.full_like(m_ref, NEG_INF)
        l_ref[...] = jnp.zeros_like(l_ref)
        acc_ref[...] = jnp.zeros_like(acc_ref)

    @pl.when(kj <= qi)
    def _():
        s = _dot_nt(q_ref[...].astype(BF), k_ref[...].astype(BF)) * scale
        cq = jnp.concatenate([cq_ref[...]] * (tk // LANES), axis=1)
        s = s + (cq - ck_ref[...])
        row = qi * tq + _iota((tq, tk), 0)
        col = kj * tk + _iota((tq, tk), 1)
        s = jnp.where(col <= row, s, NEG_INF)
        m_prev = m_ref[...]
        m_new = jnp.maximum(m_prev, jnp.max(s, axis=1, keepdims=True))
        p = jnp.exp(s - m_new)
        alpha = jnp.exp(m_prev - m_new)
        l_ref[...] = alpha * l_ref[...] + jnp.sum(p, axis=1, keepdims=True)
        acc_ref[...] = alpha * acc_ref[...] + _dot(p.astype(BF), v_ref[...].astype(BF))
        m_ref[...] = m_new

    @pl.when(kj == qi)
    def _():
        o_ref[...] = acc_ref[...] / l_ref[...]


def _fox_prompt(q, k, v, cq, ck, nh, dh):
    m = q.shape[0]
    t = _tile(m, 512)
    n = m // t
    kv_map = lambda h, i, j: (jnp.minimum(i, j), h)
    return pl.pallas_call(
        functools.partial(_fox_prompt_body, scale=dh ** -0.5),
        grid=(nh, n, n),
        in_specs=[
            pl.BlockSpec((t, dh), lambda h, i, j: (i, h)),
            pl.BlockSpec((t, dh), kv_map),
            pl.BlockSpec((t, dh), kv_map),
            pl.BlockSpec((None, t, LANES), lambda h, i, j: (h, i, 0)),
            pl.BlockSpec((None, 1, t), lambda h, i, j: (h, 0, jnp.minimum(i, j))),
        ],
        out_specs=pl.BlockSpec((t, dh), lambda h, i, j: (i, h)),
        out_shape=jax.ShapeDtypeStruct((m, nh * dh), F32),
        scratch_shapes=[pltpu.VMEM((t, 1), F32), pltpu.VMEM((t, 1), F32), pltpu.VMEM((t, dh), F32)],
        compiler_params=_cparams("parallel", "parallel", "arbitrary"),
        name="fox_prompt",
    )(q, k, v, cq, ck)


def _fox_sample_body(pt_ref, q_ref, kn_ref, vn_ref, lfn_ref, *rest, npg, nh, dh, scale):
    k_refs = rest[:npg]
    v_refs = rest[npg:2 * npg]
    lf_refs = rest[2 * npg:3 * npg]
    o_ref, s_ref, acc_ref = rest[3 * npg:]
    t, fw = q_ref.shape
    page = k_refs[0].shape[0]
    ht = nh * t
    ppr = LANES // nh

    q = q_ref[...]
    qt = jnp.concatenate([q] * nh + [jnp.zeros((LANES - ht, fw), F32)], axis=0)
    qbd = jnp.where(_iota((LANES, fw), 0) // t == _iota((LANES, fw), 1) // dh, qt, 0.0).astype(BF)

    r2 = _iota((LANES, LANES), 0)
    c2 = _iota((LANES, LANES), 1)
    e_past = ((r2 % nh == c2 // t) & (c2 < ht)).astype(BF)
    e_new = ((r2 == c2 // t) & (c2 < ht)).astype(BF)
    tri = _tri(page)
    rp = _iota((page, LANES), 0)
    cp = _iota((page, LANES), 1)

    carry = jnp.zeros((1, LANES), F32)
    for p in range(npg):
        lfp = lf_refs[p][...]
        b = jnp.concatenate([jnp.broadcast_to(lfp[r:r + 1, :], (ppr, LANES))
                             for r in range(page // ppr)], axis=0)
        b = jnp.where(cp // nh == rp % ppr, b, 0.0)
        cpage = _sel_left(tri, _sel_right(b, e_past)) + carry
        carry = cpage[page - 1:page, :]
        st = _dot_nt(k_refs[p][...].astype(BF), qbd) * scale
        s_ref[p * page:(p + 1) * page, :] = st - cpage

    y = _sel_right(lfn_ref[...], e_new)
    rt = _iota((t, LANES), 0)
    sh = 1
    while sh < t:
        y = y + jnp.where(rt >= sh, pltpu.roll(y, sh, 0), 0.0)
        sh *= 2
    cn = carry + y
    cq = jnp.sum(jnp.where(rt == _iota((t, LANES), 1) % t, cn, 0.0), axis=0, keepdims=True)
    knp = jnp.concatenate([kn_ref[...], jnp.zeros((page - t, fw), F32)], axis=0)
    stn = _dot_nt(knp.astype(BF), qbd) * scale
    cnp = jnp.concatenate([cn, jnp.zeros((page - t, LANES), F32)], axis=0)
    s_ref[npg * page:, :] = jnp.where((rp < t) & (rp <= cp % t), stn - cnp, NEG_INF)

    mx = jnp.max(s_ref[0:page, :], axis=0, keepdims=True)
    for p in range(1, npg + 1):
        mx = jnp.maximum(mx, jnp.max(s_ref[p * page:(p + 1) * page, :], axis=0, keepdims=True))
    shift = cq - (mx + cq)

    acc_ref[...] = jnp.zeros_like(acc_ref)
    lsum = jnp.zeros((LANES, 1), F32)
    for p in range(npg + 1):
        pt = jnp.exp(s_ref[p * page:(p + 1) * page, :] + shift).T
        lsum = lsum + jnp.sum(pt, axis=1, keepdims=True)
        if p < npg:
            vp = v_refs[p][...]
        else:
            vp = jnp.concatenate([vn_ref[...], jnp.zeros((page - t, fw), F32)], axis=0)
        acc_ref[...] += _dot(pt[:ht, :].astype(BF), vp.astype(BF))

    for h in range(nh):
        o_ref[:, h * dh:(h + 1) * dh] = (acc_ref[h * t:(h + 1) * t, h * dh:(h + 1) * dh]
                                          / lsum[h * t:(h + 1) * t, :])


def _fox_sample(q, kn, vn, lfn, ck, cv, clf, page_table, nh, dh):
    nb, npg = page_table.shape
    m, fw = q.shape
    t = m // nb
    page = ck.shape[1]
    rows = page * nh // LANES

    def pg_map(p):
        return lambda b, pt: (pt[b * npg + p], 0, 0)

    seq = pl.BlockSpec((t, fw), lambda b, pt: (b, 0))
    in_specs = [seq, seq, seq, pl.BlockSpec((t, LANES), lambda b, pt: (b, 0))]
    in_specs += [pl.BlockSpec((None, page, fw), pg_map(p)) for p in range(npg)]
    in_specs += [pl.BlockSpec((None, page, fw), pg_map(p)) for p in range(npg)]
    in_specs += [pl.BlockSpec((None, rows, LANES), pg_map(p)) for p in range(npg)]
    return pl.pallas_call(
        functools.partial(_fox_sample_body, npg=npg, nh=nh, dh=dh, scale=dh ** -0.5),
        grid_spec=pltpu.PrefetchScalarGridSpec(
            num_scalar_prefetch=1,
            grid=(nb,),
            in_specs=in_specs,
            out_specs=pl.BlockSpec((t, fw), lambda b, pt: (b, 0)),
            scratch_shapes=[pltpu.VMEM(((npg + 1) * page, LANES), F32),
                            pltpu.VMEM((nh * t, fw), F32)],
        ),
        out_shape=jax.ShapeDtypeStruct((m, fw), F32),
        compiler_params=_cparams("parallel"),
        name="fox_sample",
    )(page_table.reshape(-1), q, kn, vn, lfn, *([ck] * npg), *([cv] * npg), *([clf] * npg))


def _conv_seq_body(xm_ref, xh_ref, w_ref, b_ref, o_ref, xf_ref, *, width, act):
    tl = xm_ref.shape[0]
    hb = xh_ref.shape[0]
    xf_ref[0:hb, :] = jnp.where(pl.program_id(0) == 0, 0.0, xh_ref[...])
    xf_ref[hb:, :] = xm_ref[...]
    base = hb - (width - 1)
    acc = b_ref[...] + w_ref[0:1, :] * xf_ref[base:base + tl, :]
    for j in range(1, width):
        acc = acc + w_ref[j:j + 1, :] * xf_ref[base + j:base + j + tl, :]
    o_ref[...] = _silu(acc) if act else acc


def _conv_seq(x, w, b, act):
    l, c = x.shape
    width = w.shape[0]
    hb = -(-(width - 1) // SUBLANES) * SUBLANES
    tl = _tile(l, 256)
    tc = _tile(c, 256)
    r = tl // hb
    return pl.pallas_call(
        functools.partial(_conv_seq_body, width=width, act=act),
        grid=(l // tl, c // tc),
        in_specs=[
            pl.BlockSpec((tl, tc), lambda i, j: (i, j)),
            pl.BlockSpec((hb, tc), lambda i, j: (jnp.maximum(i * r - 1, 0), j)),
            pl.BlockSpec((width, tc), lambda i, j: (0, j)),
            pl.BlockSpec((1, tc), lambda i, j: (0, j)),
        ],
        out_specs=pl.BlockSpec((tl, tc), lambda i, j: (i, j)),
        out_shape=jax.ShapeDtypeStruct((l, c), F32),
        scratch_shapes=[pltpu.VMEM((tl + hb, tc), F32)],
        compiler_params=_cparams("parallel", "parallel"),
        name="conv_seq",
    )(x, x, w, b.reshape(1, c))


def _conv_step_body(xp_ref, w_ref, b_ref, o_ref, *, width, act):
    t = o_ref.shape[1]
    acc = b_ref[...] + w_ref[0:1, :] * xp_ref[:, 0:t, :]
    for j in range(1, width):
        acc = acc + w_ref[j:j + 1, :] * xp_ref[:, j:j + t, :]
    o_ref[...] = _silu(acc) if act else acc


def _conv_step(xp, w, b, act):
    nb, rows, c = xp.shape
    width = w.shape[0]
    t = rows - (width - 1)
    bb = _tile(nb, 16)
    tc = _tile(c, 512)
    return pl.pallas_call(
        functools.partial(_conv_step_body, width=width, act=act),
        grid=(nb // bb, c // tc),
        in_specs=[
            pl.BlockSpec((bb, rows, tc), lambda i, j: (i, 0, j)),
            pl.BlockSpec((width, tc), lambda i, j: (0, j)),
            pl.BlockSpec((1, tc), lambda i, j: (0, j)),
        ],
        out_specs=pl.BlockSpec((bb, t, tc), lambda i, j: (i, 0, j)),
        out_shape=jax.ShapeDtypeStruct((nb, t, c), F32),
        compiler_params=_cparams("parallel", "parallel"),
        name="conv_step",
    )(xp, w, b.reshape(1, c))


def _out_even_body(x_ref, att_ref, cv_ref, lg_ref, lb_ref, wa_ref, wc_ref, o_ref, ab_ref, cb_ref):
    @pl.when(pl.program_id(1) == 0)
    def _():
        u = cv_ref[...]
        xc = u - jnp.mean(u, axis=-1, keepdims=True)
        var = jnp.mean(xc * xc, axis=-1, keepdims=True)
        cb_ref[...] = _silu(xc * lax.rsqrt(var + EPS) * lg_ref[...] + lb_ref[...]).astype(BF)
        ab_ref[...] = att_ref[...].astype(BF)

    o_ref[...] = x_ref[...] + _dot(ab_ref[...], wa_ref[...]) + _dot(cb_ref[...], wc_ref[...])


def _out_even(x, att, cv, lg, lb, wa, wc):
    m, d = x.shape
    fw, cc = att.shape[1], cv.shape[1]
    tm = _tile(m, 512)
    tn = _tile(d, 512)
    return pl.pallas_call(
        _out_even_body,
        grid=(m // tm, d // tn),
        in_specs=[
            pl.BlockSpec((tm, tn), lambda i, j: (i, j)),
            pl.BlockSpec((tm, fw), lambda i, j: (i, 0)),
            pl.BlockSpec((tm, cc), lambda i, j: (i, 0)),
            pl.BlockSpec((1, cc), lambda i, j: (0, 0)),
            pl.BlockSpec((1, cc), lambda i, j: (0, 0)),
            pl.BlockSpec((fw, tn), lambda i, j: (0, j)),
            pl.BlockSpec((cc, tn), lambda i, j: (0, j)),
        ],
        out_specs=pl.BlockSpec((tm, tn), lambda i, j: (i, j)),
        out_shape=jax.ShapeDtypeStruct((m, d), F32),
        scratch_shapes=[pltpu.VMEM((tm, fw), BF), pltpu.VMEM((tm, cc), BF)],
        compiler_params=_cparams("parallel", "arbitrary"),
        name="out_even",
    )(x, att, cv, lg.reshape(1, cc), lb.reshape(1, cc), wa, wc)


def _mm_res_body(x_ref, a_ref, w_ref, o_ref):
    o_ref[...] = x_ref[...] + _dot(a_ref[...], w_ref[...])


def _mm_res(x, a, w):
    m, d = x.shape
    kk = a.shape[1]
    tm = _tile(m, 512)
    tn = _tile(d, 512)
    return pl.pallas_call(
        _mm_res_body,
        grid=(m // tm, d // tn),
        in_specs=[
            pl.BlockSpec((tm, tn), lambda i, j: (i, j)),
            pl.BlockSpec((tm, kk), lambda i, j: (i, 0)),
            pl.BlockSpec((kk, tn), lambda i, j: (0, j)),
        ],
        out_specs=pl.BlockSpec((tm, tn), lambda i, j: (i, j)),
        out_shape=jax.ShapeDtypeStruct((m, d), F32),
        compiler_params=_cparams("parallel", "arbitrary"),
        name="mm_res",
    )(x, a, w)


def _shifted(u, s, fix, mask):
    r = pltpu.roll(u, s, 0)
    n = fix.shape[0]
    if n == u.shape[0]:
        return jnp.where(mask, fix, r)
    return jnp.concatenate([jnp.where(mask, fix, r[:n]), r[n:]], axis=0)


def _conv3(u, w_ref, b_ref, fix1, fix2, mask1, mask2):
    return (b_ref[...] + w_ref[2:3, :] * u + w_ref[1:2, :] * _shifted(u, 1, fix1, mask1)
            + w_ref[0:1, :] * _shifted(u, 2, fix2, mask2))


def _ffn_body(*refs, stepwise, final, seq):
    x_ref, g_ref, wa_ref, wv_ref, dwa_ref, dwv_ref, ba_ref, bv_ref, wd_ref = refs[:9]
    k = 9
    if stepwise:
        sa_ref, sv_ref = refs[k:k + 2]
        k += 2
    if final:
        fg_ref = refs[k]
        k += 1
    o_ref, us_ref, xn_ref = refs[k:k + 3]
    k += 3
    i = pl.program_id(0)
    j = pl.program_id(1)
    nj = pl.num_programs(1)
    tm, tf = x_ref.shape[0], wa_ref.shape[1]

    @pl.when(j == 0)
    def _():
        xn_ref[...] = _rms(x_ref[...], g_ref[...]).astype(BF)
        o_ref[...] = x_ref[...]

    xn = xn_ref[...]
    ua = _dot(xn, wa_ref[...])
    uv = _dot(xn, wv_ref[...])

    if stepwise:
        sh = tm - seq
        fa1, fa2 = pltpu.roll(sa_ref[...], (sh + 1) % tm, 0), pltpu.roll(sa_ref[...], (sh + 2) % tm, 0)
        fv1, fv2 = pltpu.roll(sv_ref[...], (sh + 1) % tm, 0), pltpu.roll(sv_ref[...], (sh + 2) % tm, 0)
        row = _iota((tm, tf), 0)
        m1, m2 = row % seq < 1, row % seq < 2
        ub_ref = refs[k]
        for half, u in enumerate((ua, uv)):
            for c in range(tf // LANES):
                ub_ref[...] = u[:, c * LANES:(c + 1) * LANES]
                for r in range(2):
                    us_ref[r, :, half * tf + c * LANES:half * tf + (c + 1) * LANES] = (
                        ub_ref[pl.ds(seq - 2 + r, tm // seq, stride=seq), :])
    else:
        ca_ref, cv_ref = refs[k:k + 2]

        @pl.when(i == 0)
        def _():
            ca_ref[j] = jnp.zeros((SUBLANES, tf), F32)
            cv_ref[j] = jnp.zeros((SUBLANES, tf), F32)

        ca, cv = ca_ref[j], cv_ref[j]
        fa1, fa2 = pltpu.roll(ca, 1, 0), pltpu.roll(ca, 2, 0)
        fv1, fv2 = pltpu.roll(cv, 1, 0), pltpu.roll(cv, 2, 0)
        row = _iota((SUBLANES, tf), 0)
        m1, m2 = row < 1, row < 2
        ca_ref[j] = ua[tm - SUBLANES:, :]
        cv_ref[j] = uv[tm - SUBLANES:, :]
        us_ref[:, 0:tf] = ua[tm - SUBLANES:, :]
        us_ref[:, tf:2 * tf] = uv[tm - SUBLANES:, :]

    a = _conv3(ua, dwa_ref, ba_ref, fa1, fa2, m1, m2)
    v = _conv3(uv, dwv_ref, bv_ref, fv1, fv2, m1, m2)
    o_ref[...] += _dot((_silu(a) * v).astype(BF), wd_ref[...])

    if final:
        @pl.when(j == nj - 1)
        def _():
            o_ref[...] = _rms(o_ref[...], fg_ref[...])


def _ffn(x, g, w_up, dw_w, dw_b, w_down, state, final_g, seq):
    m, d = x.shape
    f = w_down.shape[0]
    tm = _tile(m, 512)
    tf = _tile(f, 512)
    nf = f // tf
    stepwise = state is not None
    final = final_g is not None
    row_spec = lambda i, j: (i, 0)
    a_col = lambda i, j: (0, j)
    v_col = lambda i, j: (0, j + nf)
    in_specs = [
        pl.BlockSpec((tm, d), row_spec),
        pl.BlockSpec((1, d), lambda i, j: (0, 0)),
        pl.BlockSpec((d, tf), a_col),
        pl.BlockSpec((d, tf), v_col),
        pl.BlockSpec((3, tf), a_col),
        pl.BlockSpec((3, tf), v_col),
        pl.BlockSpec((1, tf), a_col),
        pl.BlockSpec((1, tf), v_col),
        pl.BlockSpec((tf, d), lambda i, j: (j, 0)),
    ]
    args = [x, g.reshape(1, d), w_up, w_up, dw_w, dw_w, dw_b.reshape(1, 2 * f), dw_b.reshape(1, 2 * f), w_down]
    scratch = [pltpu.VMEM((tm, d), BF)]
    if stepwise:
        in_specs += [pl.BlockSpec((tm, tf), lambda i, j: (i, j)),
                     pl.BlockSpec((tm, tf), lambda i, j: (i, j + nf))]
        args += [state, state]
        us_shape = jax.ShapeDtypeStruct((2, nf, m // seq, 2 * tf), F32)
        us_spec = pl.BlockSpec((2, None, tm // seq, 2 * tf), lambda i, j: (0, j, i, 0))
        scratch += [pltpu.VMEM((tm, LANES), F32)]
    else:
        us_shape = jax.ShapeDtypeStruct((nf, SUBLANES, 2 * tf), F32)
        us_spec = pl.BlockSpec((None, SUBLANES, 2 * tf), lambda i, j: (j, 0, 0))
        scratch += [pltpu.VMEM((nf, SUBLANES, tf), F32), pltpu.VMEM((nf, SUBLANES, tf), F32)]
    if final:
        in_specs.append(pl.BlockSpec((1, d), lambda i, j: (0, 0)))
        args.append(final_g.reshape(1, d))
    out, us = pl.pallas_call(
        functools.partial(_ffn_body, stepwise=stepwise, final=final, seq=seq),
        grid=(m // tm, nf),
        in_specs=in_specs,
        out_specs=[pl.BlockSpec((tm, d), row_spec), us_spec],
        out_shape=[jax.ShapeDtypeStruct((m, d), F32), us_shape],
        scratch_shapes=scratch,
        compiler_params=_cparams("arbitrary", "arbitrary"),
        name="conv_ffn",
    )(*args)
    if stepwise:
        halves = [jnp.transpose(h, (2, 0, 1, 3)).reshape(m // seq, 2, f) for h in (us[..., :tf], us[..., tf:])]
    else:
        halves = [jnp.swapaxes(h, 0, 1).reshape(SUBLANES, f)[None, SUBLANES - 2:]
                  for h in (us[..., :tf], us[..., tf:])]
    return out, jnp.concatenate(halves, axis=-1)


def _ssm_in_body(x_ref, g_ref, w_ref, wdt_ref, dtb_ref, z_ref, xbc_ref, dt_ref, xn_ref, *, nz):
    j = pl.program_id(1)

    @pl.when(j == 0)
    def _():
        xn = _rms(x_ref[...], g_ref[...]).astype(BF)
        xn_ref[...] = xn
        dt_ref[...] = _softplus(_dot(xn, wdt_ref[...]) + dtb_ref[...])

    z = _dot(xn_ref[...], w_ref[...])

    @pl.when(j < nz)
    def _():
        z_ref[...] = z

    @pl.when(j >= nz)
    def _():
        xbc_ref[...] = z


def _ssm_in(x, g, w, wdt, dtb, di, cch):
    m, d = x.shape
    tm = _tile(m, 512)
    tn = min(_tile(di, 512), _tile(cch, 512))
    nz, nx = di // tn, cch // tn
    return pl.pallas_call(
        functools.partial(_ssm_in_body, nz=nz),
        grid=(m // tm, nz + nx),
        in_specs=[
            pl.BlockSpec((tm, d), lambda i, j: (i, 0)),
            pl.BlockSpec((1, d), lambda i, j: (0, 0)),
            pl.BlockSpec((d, tn), lambda i, j: (0, j)),
            pl.BlockSpec((d, LANES), lambda i, j: (0, 0)),
            pl.BlockSpec((1, LANES), lambda i, j: (0, 0)),
        ],
        out_specs=[
            pl.BlockSpec((tm, tn), lambda i, j: (i, jnp.clip(j, 0, nz - 1))),
            pl.BlockSpec((tm, tn), lambda i, j: (i, jnp.clip(j - nz, 0, nx - 1))),
            pl.BlockSpec((tm, LANES), lambda i, j: (i, 0)),
        ],
        out_shape=[
            jax.ShapeDtypeStruct((m, di), F32),
            jax.ShapeDtypeStruct((m, cch), F32),
            jax.ShapeDtypeStruct((m, LANES), F32),
        ],
        scratch_shapes=[pltpu.VMEM((tm, d), BF)],
        compiler_params=_cparams("parallel", "arbitrary"),
        name="ssm_in",
    )(x, g, w, wdt, dtb)


def _gate_norm(y, z, ng):
    yz = y * _silu(z)
    return yz * lax.rsqrt(jnp.mean(yz * yz, axis=-1, keepdims=True) + EPS) * ng


def _ssd_seq_body(xbc_ref, dt_ref, z_ref, alog_ref, dsk_ref, ng_ref, y_ref, hout_ref, ht_ref,
                  *, nheads, groups, hd, ns):
    c = pl.program_id(0)
    cs = dt_ref.shape[0]
    hpg = nheads // groups
    gp = hpg * hd
    di = nheads * hd

    @pl.when(c == 0)
    def _():
        ht_ref[...] = jnp.zeros_like(ht_ref)

    dt = dt_ref[...]
    a_neg = jnp.where(_iota((1, LANES), 1) < nheads, -jnp.exp(alog_ref[...]), 0.0)
    acum = _sel_left(_tri(cs), dt * a_neg)
    acum_t = acum.T
    dt_t = dt.T
    tot = acum[cs - 1:cs, :]
    ex = (_iota((LANES, di), 1) // hd == _iota((LANES, di), 0)).astype(BF)
    coefx = _sel_right(jnp.exp(tot - acum) * dt, ex, parts=2)
    cdx = _sel_right(jnp.broadcast_to(jnp.exp(tot), (SUBLANES, LANES)), ex, parts=2)[0:1, :]
    causal = _iota((cs, cs), 1) <= _iota((cs, cs), 0)

    for g in range(groups):
        xs = xbc_ref[:, g * gp:(g + 1) * gp]
        bm = xbc_ref[:, di + g * ns:di + (g + 1) * ns]
        cm = xbc_ref[:, di + (groups + g) * ns:di + (groups + g + 1) * ns]
        cb = _dot_nt(cm.astype(BF), bm.astype(BF))
        ht = ht_ref[g]
        htb = ht.astype(BF)
        xb = xs.astype(BF)
        ys = []
        for r in range(hpg):
            hh = g * hpg + r
            acol = acum[:, hh:hh + 1]
            seg = jnp.broadcast_to(acol, (cs, cs)) - acum_t[hh:hh + 1, :]
            mp = cb * jnp.exp(jnp.where(causal, seg, NEG_INF)) * dt_t[hh:hh + 1, :]
            csc = cm * jnp.exp(jnp.broadcast_to(acol, (cs, ns)))
            lhs = jnp.concatenate([mp, csc], axis=1).astype(BF)
            rhs = jnp.concatenate([xb[:, r * hd:(r + 1) * hd], htb[:, r * hd:(r + 1) * hd]], axis=0)
            ys.append(_dot(lhs, rhs))
        yg = jnp.concatenate(ys, axis=1) + dsk_ref[:, g * gp:(g + 1) * gp] * xs
        y_ref[:, g * gp:(g + 1) * gp] = _gate_norm(
            yg, z_ref[:, g * gp:(g + 1) * gp], ng_ref[:, g * gp:(g + 1) * gp]).astype(BF)
        wg = (coefx[:, g * gp:(g + 1) * gp] * xs).astype(BF)
        ht_ref[g] = ht * cdx[:, g * gp:(g + 1) * gp] + _dot(bm.T.astype(BF), wg)

    @pl.when(c == pl.num_programs(0) - 1)
    def _():
        hout_ref[...] = ht_ref[...]


def _ssd_seq(xbc, dt, z, alog, dsk, ng, nheads, groups, hd, ns, chunk):
    l, cch = xbc.shape
    di = nheads * hd
    gp = di // groups
    return pl.pallas_call(
        functools.partial(_ssd_seq_body, nheads=nheads, groups=groups, hd=hd, ns=ns),
        grid=(l // chunk,),
        in_specs=[
            pl.BlockSpec((chunk, cch), lambda c: (c, 0)),
            pl.BlockSpec((chunk, LANES), lambda c: (c, 0)),
            pl.BlockSpec((chunk, di), lambda c: (c, 0)),
            pl.BlockSpec((1, LANES), lambda c: (0, 0)),
            pl.BlockSpec((1, di), lambda c: (0, 0)),
            pl.BlockSpec((1, di), lambda c: (0, 0)),
        ],
        out_specs=[pl.BlockSpec((chunk, di), lambda c: (c, 0)),
                   pl.BlockSpec((groups, ns, gp), lambda c: (0, 0, 0))],
        out_shape=[jax.ShapeDtypeStruct((l, di), BF),
                   jax.ShapeDtypeStruct((groups, ns, gp), F32)],
        scratch_shapes=[pltpu.VMEM((groups, ns, gp), F32)],
        compiler_params=_cparams("arbitrary"),
        name="ssd_seq",
    )(xbc, dt, z, alog, dsk, ng)


def _ssd_step_body(xbc_ref, dt_ref, z_ref, alog_ref, alogc_ref, dsk_ref, ng_ref, h0_ref, y_ref, hn_ref,
                   *, nheads, groups, hd, ns):
    t = dt_ref.shape[0]
    hpg = nheads // groups
    gp = hpg * hd
    di = nheads * hd
    gn = groups * ns

    dt = dt_ref[...]
    a_neg = jnp.where(_iota((1, LANES), 1) < nheads, -jnp.exp(alog_ref[...]), 0.0)
    rt = _iota((t, LANES), 0)
    acum = dt * a_neg
    sh = 1
    while sh < t:
        acum = acum + jnp.where(rt >= sh, pltpu.roll(acum, sh, 0), 0.0)
        sh *= 2
    tot = acum[t - 1:t, :]
    coef = jnp.exp(tot - acum) * dt
    eac = jnp.exp(acum)

    xs = xbc_ref[:, 0:di]
    bm = xbc_ref[:, di:di + gn]
    cm = xbc_ref[:, di + gn:di + 2 * gn]

    prods = jnp.concatenate([cm * bm[s:s + 1, :] for s in range(t)], axis=0)
    rsel = ((_iota((gn, LANES), 1) // hpg == _iota((gn, LANES), 0) // ns)
            & (_iota((gn, LANES), 1) < nheads)).astype(BF)
    cbe = _dot(prods.astype(BF), rsel)
    acl = jnp.concatenate([acum] * t, axis=0)
    acs = jnp.concatenate([jnp.broadcast_to(acum[s:s + 1, :], (t, LANES)) for s in range(t)], axis=0)
    dts = jnp.concatenate([jnp.broadcast_to(dt[s:s + 1, :], (t, LANES)) for s in range(t)], axis=0)
    rr = _iota((t * t, LANES), 0)
    ms = cbe * jnp.exp(jnp.where(rr % t >= rr // t, acl - acs, NEG_INF)) * dts
    ex = (_iota((LANES, di), 1) // hd == _iota((LANES, di), 0)).astype(BF)
    big = _sel_right(jnp.concatenate([ms, eac, coef], axis=0), ex, parts=2)
    y = big[0:t, :] * xs[0:1, :]
    for s in range(1, t):
        y = y + big[s * t:(s + 1) * t, :] * xs[s:s + 1, :]
    eacx = big[t * t:t * t + t, :]
    coefx = big[t * t + t:, :]

    dt_t = jnp.concatenate([dt, jnp.zeros((LANES - t, LANES), F32)], axis=0).T
    a_col = -jnp.exp(alogc_ref[...])
    cd_col = jnp.exp(jnp.sum(dt_t * a_col, axis=1, keepdims=True))

    for g in range(groups):
        gs = slice(g * gp, (g + 1) * gp)
        h0 = h0_ref[gs, :]
        yoff = _dot_nt(cm[:, g * ns:(g + 1) * ns].astype(BF), h0.astype(BF))
        yg = y[:, gs] + yoff * eacx[:, gs] + dsk_ref[:, gs] * xs[:, gs]
        y_ref[:, gs] = _gate_norm(yg, z_ref[:, gs], ng_ref[:, gs]).astype(BF)
        wg = (coefx[:, gs] * xs[:, gs]).astype(BF)
        wpad = jnp.concatenate([wg.astype(F32), jnp.zeros((LANES - t, gp), F32)], axis=0)
        wt = jnp.concatenate([wpad[:, k * LANES:(k + 1) * LANES].T for k in range(gp // LANES)], axis=0)
        bpad = jnp.concatenate([bm[:, g * ns:(g + 1) * ns], jnp.zeros((LANES - t, ns), F32)], axis=0)
        upd = _dot(wt.astype(BF), bpad.astype(BF))
        for r in range(hpg):
            hh = g * hpg + r
            rs = slice(r * hd, (r + 1) * hd)
            hn_ref[g * gp + r * hd:g * gp + (r + 1) * hd, :] = (
                h0[rs, :] * jnp.broadcast_to(cd_col[hh:hh + 1, :], (hd, ns)) + upd[rs, :])


def _ssd_step(xbc, dt, z, alog, alogc, dsk, ng, h0, nheads, groups, hd, ns):
    nb = h0.shape[0]
    m, cch = xbc.shape
    t = m // nb
    di = nheads * hd
    return pl.pallas_call(
        functools.partial(_ssd_step_body, nheads=nheads, groups=groups, hd=hd, ns=ns),
        grid=(nb,),
        in_specs=[
            pl.BlockSpec((t, cch), lambda b: (b, 0)),
            pl.BlockSpec((t, LANES), lambda b: (b, 0)),
            pl.BlockSpec((t, di), lambda b: (b, 0)),
            pl.BlockSpec((1, LANES), lambda b: (0, 0)),
            pl.BlockSpec((LANES, 1), lambda b: (0, 0)),
            pl.BlockSpec((1, di), lambda b: (0, 0)),
            pl.BlockSpec((1, di), lambda b: (0, 0)),
            pl.BlockSpec((None, di, ns), lambda b: (b, 0, 0)),
        ],
        out_specs=[pl.BlockSpec((t, di), lambda b: (b, 0)),
                   pl.BlockSpec((None, di, ns), lambda b: (b, 0, 0))],
        out_shape=[jax.ShapeDtypeStruct((m, di), BF),
                   jax.ShapeDtypeStruct((nb, di, ns), F32)],
        compiler_params=_cparams("parallel"),
        name="ssd_step",
    )(xbc, dt, z, alog, alogc, dsk, ng, h0)


def _pad_lanes(a):
    return jnp.pad(a, [(0, 0)] * (a.ndim - 1) + [(0, LANES - a.shape[-1])])


def _trunk(x, nb, page_table, cache_k, cache_v, cache_lf, conf_prev, mconv_prev, ssm_prev, ffn_prev, w):
    _, t, d = x.shape
    m = nb * t
    stepwise = conf_prev is not None
    nh, dh = cache_k.shape[-2:]
    fw = nh * dh
    x = x.reshape(m, d)

    w_in = w["att_w_in"][0]
    cc = w["conf_dw_w"].shape[2]
    cw = w["conf_dw_w"].shape[1]
    w5 = jnp.concatenate([w_in[:, :3 * fw], w_in[:, 3 * fw + nh:]], axis=1).astype(BF)
    wf = _pad_lanes(w_in[:, 3 * fw:3 * fw + nh]).astype(BF)
    bfp = _pad_lanes(w["att_b_f"][0].reshape(1, nh))
    q, k, v, u, lf = _even_in(x, w["ln_mix"][0].reshape(1, d), w5, wf, bfp, fw, cc)
    logf = lf[:, :nh]
    if stepwise:
        npool, page = cache_k.shape[1:3]
        att = _fox_sample(q, k, v, lf,
                          cache_k[0].reshape(npool, page, fw), cache_v[0].reshape(npool, page, fw),
                          cache_lf[0].reshape(npool, page * nh // LANES, LANES), page_table, nh, dh)
        xp = jnp.concatenate([conf_prev[0], u.reshape(nb, t, cc)], axis=1)
        conf_state = xp[:, t:]
        cv = _conv_step(xp, w["conf_dw_w"][0], w["conf_dw_b"][0], act=False).reshape(m, cc)
    else:
        cq, ckt = _fox_cumsum(lf, nh)
        att = _fox_prompt(q, k, v, cq, ckt.reshape(nh, 1, m), nh, dh)
        conf_state = u[m - (cw - 1):].reshape(1, cw - 1, cc)
        cv = _conv_seq(u, w["conf_dw_w"][0], w["conf_dw_b"][0], act=False)
    w_out = w["att_w_out"][0].astype(BF)
    x = _out_even(x, att, cv, w["conf_ln_g"][0], w["conf_ln_b"][0], w_out[:fw], w_out[fw:])

    ffn_states = []

    def ffn(x, i, final_g):
        if stepwise:
            two_f = ffn_prev.shape[-1]
            st = jnp.zeros((nb, t, two_f), F32).at[:, t - 2:].set(ffn_prev[i]).reshape(m, two_f)
        else:
            st = None
        x, us = _ffn(x, w["ln_ffn"][i], w["ffn_w_up"][i].astype(BF), w["ffn_dw_w"][i], w["ffn_dw_b"][i],
                     w["ffn_w_down"][i].astype(BF), st, final_g, t)
        ffn_states.append(us)
        return x

    x = ffn(x, 0, None)

    nheads, hd, ns = w["ssm_state_shape"]
    di = nheads * hd
    cch = w["ssm_conv_w"].shape[2]
    groups = (cch - di) // (2 * ns)
    w_in = w["ssm_w_in"][0]
    wdt = _pad_lanes(w_in[:, di + cch:]).astype(BF)
    dtb = _pad_lanes(w["ssm_dt_bias"][0].reshape(1, nheads))
    z, xbc_raw, dt = _ssm_in(x, w["ln_mix"][1].reshape(1, d), w_in[:, :di + cch].astype(BF), wdt, dtb, di, cch)
    sw = w["ssm_conv_w"].shape[1]
    alog = _pad_lanes(w["ssm_a_log"][0].reshape(1, nheads))
    dsk = jnp.repeat(w["ssm_d"][0], hd).reshape(1, di)
    ng = w["ssm_norm_g"][0].reshape(1, di)
    if stepwise:
        xp = jnp.concatenate([mconv_prev[0], xbc_raw.reshape(nb, t, cch)], axis=1)
        mconv_state = xp[:, t:]
        xbc = _conv_step(xp, w["ssm_conv_w"][0], w["ssm_conv_b"][0], act=True).reshape(m, cch)
        yn, hn = _ssd_step(xbc, dt, z, alog, alog.reshape(LANES, 1), dsk, ng,
                           ssm_prev[0].reshape(nb, di, ns), nheads, groups, hd, ns)
        ssm_state = hn.reshape(nb, nheads, hd, ns)
    else:
        mconv_state = xbc_raw[m - (sw - 1):].reshape(1, sw - 1, cch)
        xbc = _conv_seq(xbc_raw, w["ssm_conv_w"][0], w["ssm_conv_b"][0], act=True)
        chunk = _tile(m, 128)
        yn, hout = _ssd_seq(xbc, dt, z, alog, dsk, ng, nheads, groups, hd, ns, chunk)
        hpg = nheads // groups
        ssm_state = jnp.transpose(hout.reshape(groups, ns, hpg, hd), (0, 2, 3, 1)).reshape(1, nheads, hd, ns)
    x = _mm_res(x, yn, w["ssm_w_out"][0].astype(BF))
    x = ffn(x, 1, w["ln_final"])

    return (x.reshape(nb, t, d),
            k.reshape(1, nb, t, nh, dh), v.reshape(1, nb, t, nh, dh), logf.reshape(1, nb, t, nh),
            conf_state[None], mconv_state[None], ssm_state[None], jnp.stack(ffn_states))


def kernel(x_prompt, x_sample, cache_k, cache_v, cache_logf, state_conf_conv, state_ssm_conv, state_ssm,
           state_ffn_conv, page_table,
           ln_mix, ln_ffn, ln_final,
           att_w_in, att_b_f, conf_dw_w, conf_dw_b, conf_ln_g, conf_ln_b, att_w_out,
           ssm_w_in, ssm_conv_w, ssm_conv_b, ssm_dt_bias, ssm_a_log, ssm_d, ssm_norm_g, ssm_w_out,
           ffn_w_up, ffn_dw_w, ffn_dw_b, ffn_w_down):
    assert ln_mix.shape[0] == 2 and x_prompt.shape[0] == 1, "two layers, one prompt sequence"
    w = dict(ln_mix=ln_mix, ln_ffn=ln_ffn, ln_final=ln_final,
             att_w_in=att_w_in, att_b_f=att_b_f, conf_dw_w=conf_dw_w, conf_dw_b=conf_dw_b,
             conf_ln_g=conf_ln_g, conf_ln_b=conf_ln_b, att_w_out=att_w_out,
             ssm_w_in=ssm_w_in, ssm_conv_w=ssm_conv_w, ssm_conv_b=ssm_conv_b, ssm_dt_bias=ssm_dt_bias,
             ssm_a_log=ssm_a_log, ssm_d=ssm_d, ssm_norm_g=ssm_norm_g, ssm_w_out=ssm_w_out,
             ffn_w_up=ffn_w_up, ffn_dw_w=ffn_dw_w, ffn_dw_b=ffn_dw_b, ffn_w_down=ffn_w_down,
             ssm_state_shape=state_ssm.shape[2:])
    nb = x_sample.shape[0]
    outs_p = _trunk(x_prompt, 1, None, cache_k, cache_v, cache_logf, None, None, None, None, w)
    outs_s = _trunk(x_sample, nb, page_table, cache_k, cache_v, cache_logf,
                    state_conf_conv, state_ssm_conv, state_ssm, state_ffn_conv, w)
    return (outs_p[0], outs_s[0]) + tuple(outs_p[1:]) + tuple(outs_s[1:])
```

```python
import functools

import jax
import jax.numpy as jnp
from jax import lax
from jax.experimental import pallas as pl
from jax.experimental.pallas import tpu as pltpu

EPS = 1e-6
BF = jnp.bfloat16
F32 = jnp.float32
LANES = 128
SUBLANES = 8
VMEM_LIMIT = 56 * 1024 * 1024
NEG_INF = float("-inf")


def _cparams(*sem):
    return pltpu.CompilerParams(dimension_semantics=sem, vmem_limit_bytes=VMEM_LIMIT)


def _tile(n, pref):
    return pref if n % pref == 0 else n


def _iota(shape, dim):
    return lax.broadcasted_iota(jnp.int32, shape, dim)


def _split3(x):
    hi = x.astype(BF)
    r1 = x - hi.astype(F32)
    mid = r1.astype(BF)
    lo = (r1 - mid.astype(F32)).astype(BF)
    return hi, mid, lo


def _dot(a, b):
    return jnp.dot(a, b, preferred_element_type=F32)


def _dot_nt(a, b):
    return lax.dot_general(a, b, (((1,), (1,)), ((), ())), preferred_element_type=F32)


def _sel_left(sel, x, parts=3):
    ps = _split3(x)[:parts]
    out = _dot(sel, ps[0])
    for p in ps[1:]:
        out = out + _dot(sel, p)
    return out


def _sel_right(x, sel, parts=3):
    ps = _split3(x)[:parts]
    out = _dot(ps[0], sel)
    for p in ps[1:]:
        out = out + _dot(p, sel)
    return out


def _log_sigmoid(x):
    return jnp.minimum(x, 0.0) - jnp.log(1.0 + jnp.exp(-jnp.abs(x)))


def _softplus(x):
    return jnp.maximum(x, 0.0) + jnp.log(1.0 + jnp.exp(-jnp.abs(x)))


def _silu(x):
    return x * jax.nn.sigmoid(x)


def _rms(x, g):
    return x * lax.rsqrt(jnp.mean(x * x, axis=-1, keepdims=True) + EPS) * g


def _tri(n):
    return (_iota((n, n), 1) <= _iota((n, n), 0)).astype(BF)


def _even_in_body(x_ref, g_ref, w_ref, wf_ref, bf_ref,
                  q_ref, k_ref, v_ref, u_ref, lf_ref, xn_ref, a_ref, *, nf, nc):
    j = pl.program_id(1)

    @pl.when(j == 0)
    def _():
        xn = _rms(x_ref[...], g_ref[...]).astype(BF)
        xn_ref[...] = xn
        lf_ref[...] = _log_sigmoid(_dot(xn, wf_ref[...]) + bf_ref[...])

    z = _dot(xn_ref[...], w_ref[...])

    @pl.when(j < nf)
    def _():
        q_ref[...] = z

    @pl.when((j >= nf) & (j < 2 * nf))
    def _():
        k_ref[...] = z

    @pl.when((j >= 2 * nf) & (j < 3 * nf))
    def _():
        v_ref[...] = z

    @pl.when((j >= 3 * nf) & ((j - 3 * nf) % 2 == 0))
    def _():
        a_ref[...] = z

    @pl.when((j >= 3 * nf) & ((j - 3 * nf) % 2 == 1))
    def _():
        u_ref[...] = a_ref[...] * jax.nn.sigmoid(z)


def _even_in(x, g, w5, wf, bfp, fw, cc):
    m, d = x.shape
    tm = _tile(m, 512)
    tn = min(_tile(fw, 512), _tile(cc, 512))
    nf, nc = fw // tn, cc // tn
    nj = 3 * nf + 2 * nc

    def w_map(i, j):
        jj = j - 3 * nf
        return (0, jnp.where(j < 3 * nf, j, 3 * nf + (jj % 2) * nc + jj // 2))

    def seg_map(base, n, div=1):
        return lambda i, j: (i, jnp.clip((j - base) // div, 0, n - 1))

    return pl.pallas_call(
        functools.partial(_even_in_body, nf=nf, nc=nc),
        grid=(m // tm, nj),
        in_specs=[
            pl.BlockSpec((tm, d), lambda i, j: (i, 0)),
            pl.BlockSpec((1, d), lambda i, j: (0, 0)),
            pl.BlockSpec((d, tn), w_map),
            pl.BlockSpec((d, LANES), lambda i, j: (0, 0)),
            pl.BlockSpec((1, LANES), lambda i, j: (0, 0)),
        ],
        out_specs=[
            pl.BlockSpec((tm, tn), seg_map(0, nf)),
            pl.BlockSpec((tm, tn), seg_map(nf, nf)),
            pl.BlockSpec((tm, tn), seg_map(2 * nf, nf)),
            pl.BlockSpec((tm, tn), seg_map(3 * nf, nc, 2)),
            pl.BlockSpec((tm, LANES), lambda i, j: (i, 0)),
        ],
        out_shape=[
            jax.ShapeDtypeStruct((m, fw), F32),
            jax.ShapeDtypeStruct((m, fw), F32),
            jax.ShapeDtypeStruct((m, fw), F32),
            jax.ShapeDtypeStruct((m, cc), F32),
            jax.ShapeDtypeStruct((m, LANES), F32),
        ],
        scratch_shapes=[pltpu.VMEM((tm, d), BF), pltpu.VMEM((tm, tn), F32)],
        compiler_params=_cparams("parallel", "arbitrary"),
        name="even_in",
    )(x, g, w5, wf, bfp)


def _fox_cumsum_body(lf_ref, cq_ref, ckt_ref, carry_ref, *, nh):
    @pl.when(pl.program_id(0) == 0)
    def _():
        carry_ref[...] = jnp.zeros_like(carry_ref)

    lf = lf_ref[...]
    tb = lf.shape[0]
    cs = _sel_left(_tri(tb), lf) + carry_ref[...]
    carry_ref[...] = cs[tb - 1:tb, :]
    for h in range(nh):
        cq_ref[h] = jnp.broadcast_to(cs[:, h:h + 1], (tb, LANES))
    ckt_ref[...] = cs.T[:nh, :]


def _fox_cumsum(lf, nh):
    m = lf.shape[0]
    tb = _tile(m, 256)
    return pl.pallas_call(
        functools.partial(_fox_cumsum_body, nh=nh),
        grid=(m // tb,),
        in_specs=[pl.BlockSpec((tb, LANES), lambda i: (i, 0))],
        out_specs=[pl.BlockSpec((nh, tb, LANES), lambda i: (0, i, 0)),
                   pl.BlockSpec((nh, tb), lambda i: (0, i))],
        out_shape=[jax.ShapeDtypeStruct((nh, m, LANES), F32),
                   jax.ShapeDtypeStruct((nh, m), F32)],
        scratch_shapes=[pltpu.VMEM((1, LANES), F32)],
        compiler_params=_cparams("arbitrary"),
        name="fox_cumsum",
    )(lf)


def _fox_prompt_body(q_ref, k_ref, v_ref, cq_ref, ck_ref, o_ref, kb_ref, vb_ref, m_ref, acc_ref, *, scale):
    qi = pl.program_id(1)
    t, dh = q_ref.shape
    log2e = 1.4426950408889634

    @pl.when(qi == 0)
    def _():
        kb_ref[...] = k_ref[...].astype(BF)
        vb_ref[:, 0:dh] = v_ref[...].astype(BF)
        vb_ref[:, dh:] = (_iota((k_ref.shape[0], dh), 1) == 0).astype(BF)

    qs = (q_ref[...] * (scale * log2e)).astype(BF)
    cq = cq_ref[...] * log2e
    m_ref[...] = jnp.full_like(m_ref, NEG_INF)
    acc_ref[...] = jnp.zeros_like(acc_ref)
    reps = t // LANES

    def block(kj, masked):
        off = pl.multiple_of(kj * t, t)
        s = _dot_nt(qs, kb_ref[pl.ds(off, t), :]) - ck_ref[kj] * log2e
        if masked:
            s = jnp.where(_iota((t, t), 1) <= _iota((t, t), 0), s, NEG_INF)
        m_prev = m_ref[...]
        m_new = jnp.maximum(m_prev, jnp.max(s, axis=1, keepdims=True) + cq)
        shift = m_new - cq
        p = jnp.exp2(s - jnp.concatenate([shift] * reps, axis=1))
        alpha = jnp.exp2(m_prev - m_new)
        acc_ref[...] = (jnp.concatenate([alpha] * (2 * dh // LANES), axis=1) * acc_ref[...]
                        + _dot(p.astype(BF), vb_ref[pl.ds(off, t), :]))
        m_ref[...] = m_new

    def body(kj, carry):
        block(kj, False)
        return carry

    lax.fori_loop(0, qi, body, 0)
    block(qi, True)
    o_ref[...] = acc_ref[:, 0:dh] / acc_ref[:, dh:dh + 1]


def _fox_prompt(q, k, v, cq, ck, nh, dh):
    m = q.shape[0]
    t = _tile(m, 512)
    n = m // t
    return pl.pallas_call(
        functools.partial(_fox_prompt_body, scale=dh ** -0.5),
        grid=(nh, n),
        in_specs=[
            pl.BlockSpec((t, dh), lambda h, i: (i, h)),
            pl.BlockSpec((m, dh), lambda h, i: (0, h)),
            pl.BlockSpec((m, dh), lambda h, i: (0, h)),
            pl.BlockSpec((None, t, LANES), lambda h, i: (h, i, 0)),
            pl.BlockSpec((None, n, 1, t), lambda h, i: (h, 0, 0, 0)),
        ],
        out_specs=pl.BlockSpec((t, dh), lambda h, i: (i, h)),
        out_shape=jax.ShapeDtypeStruct((m, nh * dh), F32),
        scratch_shapes=[pltpu.VMEM((m, dh), BF), pltpu.VMEM((m, 2 * dh), BF),
                        pltpu.VMEM((t, LANES), F32), pltpu.VMEM((t, 2 * dh), F32)],
        compiler_params=_cparams("parallel", "arbitrary"),
        name="fox_prompt",
    )(q, k, v, cq, ck.reshape(nh, n, 1, t))


def _fox_sample_body(pt_ref, q_ref, kn_ref, vn_ref, lfn_ref, *rest, npg, nh, dh, scale):
    k_refs = rest[:npg]
    v_refs = rest[npg:2 * npg]
    lf_refs = rest[2 * npg:3 * npg]
    o_ref, s_ref, acc_ref = rest[3 * npg:]
    t, fw = q_ref.shape
    page = k_refs[0].shape[0] // nh
    ht = nh * t
    ppr = LANES // nh

    def load_page(ref):
        return jnp.concatenate([ref[pl.ds(h, page, stride=nh), :] for h in range(nh)], axis=1)

    q = q_ref[...]
    qt = jnp.concatenate([q] * nh + [jnp.zeros((LANES - ht, fw), F32)], axis=0)
    qbd = jnp.where(_iota((LANES, fw), 0) // t == _iota((LANES, fw), 1) // dh, qt, 0.0).astype(BF)

    r2 = _iota((LANES, LANES), 0)
    c2 = _iota((LANES, LANES), 1)
    e_past = ((r2 % nh == c2 // t) & (c2 < ht)).astype(BF)
    e_new = ((r2 == c2 // t) & (c2 < ht)).astype(BF)
    tri = _tri(page)
    rp = _iota((page, LANES), 0)
    cp = _iota((page, LANES), 1)

    carry = jnp.zeros((1, LANES), F32)
    for p in range(npg):
        lfp = lf_refs[p][...]
        b = jnp.concatenate([jnp.broadcast_to(lfp[r:r + 1, :], (ppr, LANES))
                             for r in range(page // ppr)], axis=0)
        b = jnp.where(cp // nh == rp % ppr, b, 0.0)
        cpage = _sel_left(tri, _sel_right(b, e_past)) + carry
        carry = cpage[page - 1:page, :]
        st = _dot_nt(load_page(k_refs[p]).astype(BF), qbd) * scale
        s_ref[p * page:(p + 1) * page, :] = st - cpage

    y = _sel_right(lfn_ref[...], e_new)
    rt = _iota((t, LANES), 0)
    sh = 1
    while sh < t:
        y = y + jnp.where(rt >= sh, pltpu.roll(y, sh, 0), 0.0)
        sh *= 2
    cn = carry + y
    cq = jnp.sum(jnp.where(rt == _iota((t, LANES), 1) % t, cn, 0.0), axis=0, keepdims=True)
    knp = jnp.concatenate([kn_ref[...], jnp.zeros((page - t, fw), F32)], axis=0)
    stn = _dot_nt(knp.astype(BF), qbd) * scale
    cnp = jnp.concatenate([cn, jnp.zeros((page - t, LANES), F32)], axis=0)
    s_ref[npg * page:, :] = jnp.where((rp < t) & (rp <= cp % t), stn - cnp, NEG_INF)

    mx = jnp.max(s_ref[0:page, :], axis=0, keepdims=True)
    for p in range(1, npg + 1):
        mx = jnp.maximum(mx, jnp.max(s_ref[p * page:(p + 1) * page, :], axis=0, keepdims=True))
    shift = cq - (mx + cq)

    acc_ref[...] = jnp.zeros_like(acc_ref)
    lsum = jnp.zeros((LANES, 1), F32)
    for p in range(npg + 1):
        pt = jnp.exp(s_ref[p * page:(p + 1) * page, :] + shift).T
        lsum = lsum + jnp.sum(pt, axis=1, keepdims=True)
        if p < npg:
            vp = load_page(v_refs[p])
        else:
            vp = jnp.concatenate([vn_ref[...], jnp.zeros((page - t, fw), F32)], axis=0)
        acc_ref[...] += _dot(pt[:ht, :].astype(BF), vp.astype(BF))

    for h in range(nh):
        o_ref[:, h * dh:(h + 1) * dh] = (acc_ref[h * t:(h + 1) * t, h * dh:(h + 1) * dh]
                                          / lsum[h * t:(h + 1) * t, :])


def _fox_sample(q, kn, vn, lfn, ck, cv, clf, page_table, nh, dh):
    nb, npg = page_table.shape
    m, fw = q.shape
    t = m // nb
    prows = ck.shape[1]
    page = prows // nh
    rows = page * nh // LANES

    def pg_map(p):
        return lambda b, pt: (pt[b * npg + p], 0, 0)

    seq = pl.BlockSpec((t, fw), lambda b, pt: (b, 0))
    in_specs = [seq, seq, seq, pl.BlockSpec((t, LANES), lambda b, pt: (b, 0))]
    in_specs += [pl.BlockSpec((None, prows, dh), pg_map(p)) for p in range(npg)]
    in_specs += [pl.BlockSpec((None, prows, dh), pg_map(p)) for p in range(npg)]
    in_specs += [pl.BlockSpec((None, rows, LANES), pg_map(p)) for p in range(npg)]
    return pl.pallas_call(
        functools.partial(_fox_sample_body, npg=npg, nh=nh, dh=dh, scale=dh ** -0.5),
        grid_spec=pltpu.PrefetchScalarGridSpec(
            num_scalar_prefetch=1,
            grid=(nb,),
            in_specs=in_specs,
            out_specs=pl.BlockSpec((t, fw), lambda b, pt: (b, 0)),
            scratch_shapes=[pltpu.VMEM(((npg + 1) * page, LANES), F32),
                            pltpu.VMEM((nh * t, fw), F32)],
        ),
        out_shape=jax.ShapeDtypeStruct((m, fw), F32),
        compiler_params=_cparams("parallel"),
        name="fox_sample",
    )(page_table.reshape(-1), q, kn, vn, lfn, *([ck] * npg), *([cv] * npg), *([clf] * npg))


def _conv_seq_body(xm_ref, xh_ref, w_ref, b_ref, o_ref, xf_ref, *, width, act):
    tl = xm_ref.shape[0]
    hb = xh_ref.shape[0]
    xf_ref[0:hb, :] = jnp.where(pl.program_id(0) == 0, 0.0, xh_ref[...])
    xf_ref[hb:, :] = xm_ref[...]
    base = hb - (width - 1)
    acc = b_ref[...] + w_ref[0:1, :] * xf_ref[base:base + tl, :]
    for j in range(1, width):
        acc = acc + w_ref[j:j + 1, :] * xf_ref[base + j:base + j + tl, :]
    o_ref[...] = _silu(acc) if act else acc


def _conv_seq(x, w, b, act):
    l, c = x.shape
    width = w.shape[0]
    hb = -(-(width - 1) // SUBLANES) * SUBLANES
    tl = _tile(l, 256)
    tc = _tile(c, 256)
    r = tl // hb
    return pl.pallas_call(
        functools.partial(_conv_seq_body, width=width, act=act),
        grid=(l // tl, c // tc),
        in_specs=[
            pl.BlockSpec((tl, tc), lambda i, j: (i, j)),
            pl.BlockSpec((hb, tc), lambda i, j: (jnp.maximum(i * r - 1, 0), j)),
            pl.BlockSpec((width, tc), lambda i, j: (0, j)),
            pl.BlockSpec((1, tc), lambda i, j: (0, j)),
        ],
        out_specs=pl.BlockSpec((tl, tc), lambda i, j: (i, j)),
        out_shape=jax.ShapeDtypeStruct((l, c), F32),
        scratch_shapes=[pltpu.VMEM((tl + hb, tc), F32)],
        compiler_params=_cparams("parallel", "parallel"),
        name="conv_seq",
    )(x, x, w, b.reshape(1, c))


def _conv_step_body(xp_ref, w_ref, b_ref, o_ref, *, width, act):
    t = o_ref.shape[1]
    acc = b_ref[...] + w_ref[0:1, :] * xp_ref[:, 0:t, :]
    for j in range(1, width):
        acc = acc + w_ref[j:j + 1, :] * xp_ref[:, j:j + t, :]
    o_ref[...] = _silu(acc) if act else acc


def _conv_step(xp, w, b, act):
    nb, rows, c = xp.shape
    width = w.shape[0]
    t = rows - (width - 1)
    bb = _tile(nb, 16)
    tc = _tile(c, 512)
    return pl.pallas_call(
        functools.partial(_conv_step_body, width=width, act=act),
        grid=(nb // bb, c // tc),
        in_specs=[
            pl.BlockSpec((bb, rows, tc), lambda i, j: (i, 0, j)),
            pl.BlockSpec((width, tc), lambda i, j: (0, j)),
            pl.BlockSpec((1, tc), lambda i, j: (0, j)),
        ],
        out_specs=pl.BlockSpec((bb, t, tc), lambda i, j: (i, 0, j)),
        out_shape=jax.ShapeDtypeStruct((nb, t, c), F32),
        compiler_params=_cparams("parallel", "parallel"),
        name="conv_step",
    )(xp, w, b.reshape(1, c))


def _out_even_body(x_ref, att_ref, cv_ref, lg_ref, lb_ref, wa_ref, wc_ref, o_ref, ab_ref, cb_ref):
    @pl.when(pl.program_id(1) == 0)
    def _():
        u = cv_ref[...]
        xc = u - jnp.mean(u, axis=-1, keepdims=True)
        var = jnp.mean(xc * xc, axis=-1, keepdims=True)
        cb_ref[...] = _silu(xc * lax.rsqrt(var + EPS) * lg_ref[...] + lb_ref[...]).astype(BF)
        ab_ref[...] = att_ref[...].astype(BF)

    o_ref[...] = x_ref[...] + _dot(ab_ref[...], wa_ref[...]) + _dot(cb_ref[...], wc_ref[...])


def _out_even(x, att, cv, lg, lb, wa, wc):
    m, d = x.shape
    fw, cc = att.shape[1], cv.shape[1]
    tm = _tile(m, 512)
    tn = _tile(d, 512)
    return pl.pallas_call(
        _out_even_body,
        grid=(m // tm, d // tn),
        in_specs=[
            pl.BlockSpec((tm, tn), lambda i, j: (i, j)),
            pl.BlockSpec((tm, fw), lambda i, j: (i, 0)),
            pl.BlockSpec((tm, cc), lambda i, j: (i, 0)),
            pl.BlockSpec((1, cc), lambda i, j: (0, 0)),
            pl.BlockSpec((1, cc), lambda i, j: (0, 0)),
            pl.BlockSpec((fw, tn), lambda i, j: (0, j)),
            pl.BlockSpec((cc, tn), lambda i, j: (0, j)),
        ],
        out_specs=pl.BlockSpec((tm, tn), lambda i, j: (i, j)),
        out_shape=jax.ShapeDtypeStruct((m, d), F32),
        scratch_shapes=[pltpu.VMEM((tm, fw), BF), pltpu.VMEM((tm, cc), BF)],
        compiler_params=_cparams("parallel", "arbitrary"),
        name="out_even",
    )(x, att, cv, lg.reshape(1, cc), lb.reshape(1, cc), wa, wc)


def _mm_res_body(x_ref, a_ref, w_ref, o_ref):
    o_ref[...] = x_ref[...] + _dot(a_ref[...], w_ref[...])


def _mm_res(x, a, w):
    m, d = x.shape
    kk = a.shape[1]
    tm = _tile(m, 512)
    tn = _tile(d, 512)
    return pl.pallas_call(
        _mm_res_body,
        grid=(m // tm, d // tn),
        in_specs=[
            pl.BlockSpec((tm, tn), lambda i, j: (i, j)),
            pl.BlockSpec((tm, kk), lambda i, j: (i, 0)),
            pl.BlockSpec((kk, tn), lambda i, j: (0, j)),
        ],
        out_specs=pl.BlockSpec((tm, tn), lambda i, j: (i, j)),
        out_shape=jax.ShapeDtypeStruct((m, d), F32),
        compiler_params=_cparams("parallel", "arbitrary"),
        name="mm_res",
    )(x, a, w)


def _causal_conv(u, w_ref, b_ref, prev, seq):
    tm = u.shape[0]
    width = w_ref.shape[0]
    row = _iota(prev.shape, 0)
    acc = b_ref[...] + w_ref[width - 1:width, :] * u
    for s in range(1, width):
        r = pltpu.roll(u, s, 0)
        if seq is None:
            head = jnp.where(row < s, pltpu.roll(prev, s, 0), r[:SUBLANES])
            r = jnp.concatenate([head, r[SUBLANES:]], axis=0)
        else:
            r = jnp.where(row % seq < s, pltpu.roll(prev, (tm - seq + s) % tm, 0), r)
        acc = acc + w_ref[width - 1 - s:width - s, :] * r
    return acc


def _save_seq_tails(u, us_ref, col, ub_ref, seq, keep):
    tm, c = u.shape
    for k in range(c // LANES):
        ub_ref[...] = u[:, k * LANES:(k + 1) * LANES]
        for r in range(keep):
            us_ref[r, :, col + k * LANES:col + (k + 1) * LANES] = (
                ub_ref[pl.ds(seq - keep + r, tm // seq, stride=seq), :])


def _ffn_body(*refs, stepwise, final, seq):
    x_ref, g_ref, wa_ref, wv_ref, dwa_ref, dwv_ref, ba_ref, bv_ref, wd_ref = refs[:9]
    k = 9
    if stepwise:
        sa_ref, sv_ref = refs[k:k + 2]
        k += 2
    if final:
        fg_ref = refs[k]
        k += 1
    o_ref, us_ref, xn_ref = refs[k:k + 3]
    k += 3
    i = pl.program_id(0)
    j = pl.program_id(1)
    nj = pl.num_programs(1)
    tm, tf = x_ref.shape[0], wa_ref.shape[1]

    @pl.when(j == 0)
    def _():
        xn_ref[...] = _rms(x_ref[...], g_ref[...]).astype(BF)
        o_ref[...] = x_ref[...]

    xn = xn_ref[...]
    ua = _dot(xn, wa_ref[...])
    uv = _dot(xn, wv_ref[...])

    if stepwise:
        pa, pv = sa_ref[...], sv_ref[...]
        _save_seq_tails(ua, us_ref, 0, refs[k], seq, 2)
        _save_seq_tails(uv, us_ref, tf, refs[k], seq, 2)
    else:
        ca_ref, cv_ref = refs[k:k + 2]

        @pl.when(i == 0)
        def _():
            ca_ref[j] = jnp.zeros((SUBLANES, tf), F32)
            cv_ref[j] = jnp.zeros((SUBLANES, tf), F32)

        pa, pv = ca_ref[j], cv_ref[j]
        ca_ref[j] = ua[tm - SUBLANES:, :]
        cv_ref[j] = uv[tm - SUBLANES:, :]
        us_ref[:, 0:tf] = ua[tm - SUBLANES:, :]
        us_ref[:, tf:2 * tf] = uv[tm - SUBLANES:, :]

    cseq = seq if stepwise else None
    a = _causal_conv(ua, dwa_ref, ba_ref, pa, cseq)
    v = _causal_conv(uv, dwv_ref, bv_ref, pv, cseq)
    o_ref[...] += _dot((_silu(a) * v).astype(BF), wd_ref[...])

    if final:
        @pl.when(j == nj - 1)
        def _():
            o_ref[...] = _rms(o_ref[...], fg_ref[...])


def _ffn(x, g, w_up, dw_w, dw_b, w_down, state, final_g, seq):
    m, d = x.shape
    f = w_down.shape[0]
    tm = _tile(m, 512)
    tf = _tile(f, 512)
    nf = f // tf
    stepwise = state is not None
    final = final_g is not None
    row_spec = lambda i, j: (i, 0)
    a_col = lambda i, j: (0, j)
    v_col = lambda i, j: (0, j + nf)
    in_specs = [
        pl.BlockSpec((tm, d), row_spec),
        pl.BlockSpec((1, d), lambda i, j: (0, 0)),
        pl.BlockSpec((d, tf), a_col),
        pl.BlockSpec((d, tf), v_col),
        pl.BlockSpec((3, tf), a_col),
        pl.BlockSpec((3, tf), v_col),
        pl.BlockSpec((1, tf), a_col),
        pl.BlockSpec((1, tf), v_col),
        pl.BlockSpec((tf, d), lambda i, j: (j, 0)),
    ]
    args = [x, g.reshape(1, d), w_up, w_up, dw_w, dw_w, dw_b.reshape(1, 2 * f), dw_b.reshape(1, 2 * f), w_down]
    scratch = [pltpu.VMEM((tm, d), BF)]
    if stepwise:
        in_specs += [pl.BlockSpec((tm, tf), lambda i, j: (i, j)),
                     pl.BlockSpec((tm, tf), lambda i, j: (i, j + nf))]
        args += [state, state]
        us_shape = jax.ShapeDtypeStruct((2, nf, m // seq, 2 * tf), F32)
        us_spec = pl.BlockSpec((2, None, tm // seq, 2 * tf), lambda i, j: (0, j, i, 0))
        scratch += [pltpu.VMEM((tm, LANES), F32)]
    else:
        us_shape = jax.ShapeDtypeStruct((m // tm, nf, SUBLANES, 2 * tf), F32)
        us_spec = pl.BlockSpec((None, None, SUBLANES, 2 * tf), lambda i, j: (i, j, 0, 0))
        scratch += [pltpu.VMEM((nf, SUBLANES, tf), F32), pltpu.VMEM((nf, SUBLANES, tf), F32)]
    if final:
        in_specs.append(pl.BlockSpec((1, d), lambda i, j: (0, 0)))
        args.append(final_g.reshape(1, d))
    out, us = pl.pallas_call(
        functools.partial(_ffn_body, stepwise=stepwise, final=final, seq=seq),
        grid=(m // tm, nf),
        in_specs=in_specs,
        out_specs=[pl.BlockSpec((tm, d), row_spec), us_spec],
        out_shape=[jax.ShapeDtypeStruct((m, d), F32), us_shape],
        scratch_shapes=scratch,
        compiler_params=_cparams("arbitrary", "arbitrary"),
        name="conv_ffn",
    )(*args)
    if stepwise:
        halves = [jnp.transpose(h, (2, 0, 1, 3)).reshape(m // seq, 2, f) for h in (us[..., :tf], us[..., tf:])]
    else:
        us = us[m // tm - 1]
        halves = [jnp.swapaxes(h, 0, 1).reshape(SUBLANES, f)[None, SUBLANES - 2:]
                  for h in (us[..., :tf], us[..., tf:])]
    return out, jnp.concatenate(halves, axis=-1)


def _ssm_in_body(*refs, nz, stepwise, seq):
    x_ref, g_ref, w_ref, wdt_ref, dtb_ref, cw_ref, cb_ref = refs[:7]
    k = 7
    if stepwise:
        st_ref = refs[k]
        k += 1
    z_ref, xbc_ref, dt_ref, us_ref, xn_ref, aux_ref = refs[k:k + 6]
    i = pl.program_id(0)
    j = pl.program_id(1)
    tm = x_ref.shape[0]
    keep = cw_ref.shape[0] - 1

    @pl.when(j == 0)
    def _():
        xn = _rms(x_ref[...], g_ref[...]).astype(BF)
        xn_ref[...] = xn
        dt_ref[...] = _softplus(_dot(xn, wdt_ref[...]) + dtb_ref[...])

    z = _dot(xn_ref[...], w_ref[...])

    @pl.when(j < nz)
    def _():
        z_ref[...] = z

    @pl.when(j >= nz)
    def _():
        if stepwise:
            prev = st_ref[...]
            _save_seq_tails(z, us_ref, 0, aux_ref, seq, keep)
        else:
            jx = j - nz

            @pl.when(i == 0)
            def _():
                aux_ref[jx] = jnp.zeros(aux_ref.shape[1:], F32)

            prev = aux_ref[jx]
            aux_ref[jx] = z[tm - SUBLANES:, :]
            us_ref[...] = z[tm - SUBLANES:, :]
        xbc_ref[...] = _silu(_causal_conv(z, cw_ref, cb_ref, prev, seq if stepwise else None))


def _ssm_in(x, g, w, wdt, dtb, cw, cb, state, di, cch, seq):
    m, d = x.shape
    tm = _tile(m, 512)
    tn = min(_tile(di, 512), _tile(cch, 512))
    nz, nx = di // tn, cch // tn
    width = cw.shape[0]
    keep = width - 1
    stepwise = state is not None
    xcol = lambda i, j: (0, jnp.clip(j - nz, 0, nx - 1))
    in_specs = [
        pl.BlockSpec((tm, d), lambda i, j: (i, 0)),
        pl.BlockSpec((1, d), lambda i, j: (0, 0)),
        pl.BlockSpec((d, tn), lambda i, j: (0, j)),
        pl.BlockSpec((d, LANES), lambda i, j: (0, 0)),
        pl.BlockSpec((1, LANES), lambda i, j: (0, 0)),
        pl.BlockSpec((width, tn), xcol),
        pl.BlockSpec((1, tn), xcol),
    ]
    args = [x, g, w, wdt, dtb, cw, cb.reshape(1, cch)]
    if stepwise:
        in_specs.append(pl.BlockSpec((tm, tn), lambda i, j: (i, jnp.clip(j - nz, 0, nx - 1))))
        args.append(state)
        us_shape = jax.ShapeDtypeStruct((keep, nx, m // seq, tn), F32)
        us_spec = pl.BlockSpec((keep, None, tm // seq, tn), lambda i, j: (0, jnp.clip(j - nz, 0, nx - 1), i, 0))
        aux = pltpu.VMEM((tm, LANES), F32)
    else:
        us_shape = jax.ShapeDtypeStruct((m // tm, nx, SUBLANES, tn), F32)
        us_spec = pl.BlockSpec((None, None, SUBLANES, tn), lambda i, j: (i, jnp.clip(j - nz, 0, nx - 1), 0, 0))
        aux = pltpu.VMEM((nx, SUBLANES, tn), F32)
    z, xbc, dt, us = pl.pallas_call(
        functools.partial(_ssm_in_body, nz=nz, stepwise=stepwise, seq=seq),
        grid=(m // tm, nz + nx),
        in_specs=in_specs,
        out_specs=[
            pl.BlockSpec((tm, tn), lambda i, j: (i, jnp.clip(j, 0, nz - 1))),
            pl.BlockSpec((tm, tn), lambda i, j: (i, jnp.clip(j - nz, 0, nx - 1))),
            pl.BlockSpec((tm, LANES), lambda i, j: (i, 0)),
            us_spec,
        ],
        out_shape=[
            jax.ShapeDtypeStruct((m, di), F32),
            jax.ShapeDtypeStruct((m, cch), F32),
            jax.ShapeDtypeStruct((m, LANES), F32),
            us_shape,
        ],
        scratch_shapes=[pltpu.VMEM((tm, d), BF), aux],
        compiler_params=_cparams("arbitrary", "arbitrary"),
        name="ssm_in",
    )(*args)
    if stepwise:
        conv_state = jnp.transpose(us, (2, 0, 1, 3)).reshape(m // seq, keep, cch)
    else:
        conv_state = jnp.swapaxes(us[m // tm - 1], 0, 1).reshape(SUBLANES, cch)[None, SUBLANES - keep:]
    return z, xbc, dt, conv_state


def _gate_norm(y, z, ng):
    yz = y * _silu(z)
    return yz * lax.rsqrt(jnp.mean(yz * yz, axis=-1, keepdims=True) + EPS) * ng


def _ssd_seq_body(xbc_ref, dt_ref, z_ref, alog_ref, dsk_ref, ng_ref, y_ref, hout_ref, ht_ref,
                  *, nheads, groups, hd, ns):
    c = pl.program_id(0)
    cs = dt_ref.shape[0]
    hpg = nheads // groups
    gp = hpg * hd
    di = nheads * hd

    @pl.when(c == 0)
    def _():
        ht_ref[...] = jnp.zeros_like(ht_ref)

    dt = dt_ref[...]
    a_neg = jnp.where(_iota((1, LANES), 1) < nheads, -jnp.exp(alog_ref[...]), 0.0)
    acum = _sel_left(_tri(cs), dt * a_neg)
    acum_t = acum.T
    dt_t = dt.T
    tot = acum[cs - 1:cs, :]
    ex = (_iota((LANES, di), 1) // hd == _iota((LANES, di), 0)).astype(BF)
    coefx = _sel_right(jnp.exp(tot - acum) * dt, ex, parts=2)
    cdx = _sel_right(jnp.broadcast_to(jnp.exp(tot), (SUBLANES, LANES)), ex, parts=2)[0:1, :]
    causal = _iota((cs, cs), 1) <= _iota((cs, cs), 0)

    for g in range(groups):
        xs = xbc_ref[:, g * gp:(g + 1) * gp]
        bm = xbc_ref[:, di + g * ns:di + (g + 1) * ns]
        cm = xbc_ref[:, di + (groups + g) * ns:di + (groups + g + 1) * ns]
        cb = _dot_nt(cm.astype(BF), bm.astype(BF))
        ht = ht_ref[g]
        htb = ht.astype(BF)
        xb = xs.astype(BF)
        ys = []
        for r in range(hpg):
            hh = g * hpg + r
            acol = acum[:, hh:hh + 1]
            seg = jnp.broadcast_to(acol, (cs, cs)) - acum_t[hh:hh + 1, :]
            mp = cb * jnp.exp(jnp.where(causal, seg, NEG_INF)) * dt_t[hh:hh + 1, :]
            csc = cm * jnp.exp(jnp.broadcast_to(acol, (cs, ns)))
            lhs = jnp.concatenate([mp, csc], axis=1).astype(BF)
            rhs = jnp.concatenate([xb[:, r * hd:(r + 1) * hd], htb[:, r * hd:(r + 1) * hd]], axis=0)
            ys.append(_dot(lhs, rhs))
        yg = jnp.concatenate(ys, axis=1) + dsk_ref[:, g * gp:(g + 1) * gp] * xs
        y_ref[:, g * gp:(g + 1) * gp] = _gate_norm(
            yg, z_ref[:, g * gp:(g + 1) * gp], ng_ref[:, g * gp:(g + 1) * gp]).astype(BF)
        wg = (coefx[:, g * gp:(g + 1) * gp] * xs).astype(BF)
        ht_ref[g] = ht * cdx[:, g * gp:(g + 1) * gp] + _dot(bm.T.astype(BF), wg)

    @pl.when(c == pl.num_programs(0) - 1)
    def _():
        hout_ref[...] = ht_ref[...]


def _ssd_seq(xbc, dt, z, alog, dsk, ng, nheads, groups, hd, ns, chunk):
    l, cch = xbc.shape
    di = nheads * hd
    gp = di // groups
    return pl.pallas_call(
        functools.partial(_ssd_seq_body, nheads=nheads, groups=groups, hd=hd, ns=ns),
        grid=(l // chunk,),
        in_specs=[
            pl.BlockSpec((chunk, cch), lambda c: (c, 0)),
            pl.BlockSpec((chunk, LANES), lambda c: (c, 0)),
            pl.BlockSpec((chunk, di), lambda c: (c, 0)),
            pl.BlockSpec((1, LANES), lambda c: (0, 0)),
            pl.BlockSpec((1, di), lambda c: (0, 0)),
            pl.BlockSpec((1, di), lambda c: (0, 0)),
        ],
        out_specs=[pl.BlockSpec((chunk, di), lambda c: (c, 0)),
                   pl.BlockSpec((groups, ns, gp), lambda c: (0, 0, 0))],
        out_shape=[jax.ShapeDtypeStruct((l, di), BF),
                   jax.ShapeDtypeStruct((groups, ns, gp), F32)],
        scratch_shapes=[pltpu.VMEM((groups, ns, gp), F32)],
        compiler_params=_cparams("arbitrary"),
        name="ssd_seq",
    )(xbc, dt, z, alog, dsk, ng)


def _ssd_step_body(xbc_ref, dt_ref, z_ref, alog_ref, alogc_ref, dsk_ref, ng_ref, h0_ref, y_ref, hn_ref,
                   *, nheads, groups, hd, ns):
    t = dt_ref.shape[0]
    hpg = nheads // groups
    gp = hpg * hd
    di = nheads * hd
    gn = groups * ns

    dt = dt_ref[...]
    a_neg = jnp.where(_iota((1, LANES), 1) < nheads, -jnp.exp(alog_ref[...]), 0.0)
    rt = _iota((t, LANES), 0)
    acum = dt * a_neg
    sh = 1
    while sh < t:
        acum = acum + jnp.where(rt >= sh, pltpu.roll(acum, sh, 0), 0.0)
        sh *= 2
    tot = acum[t - 1:t, :]
    coef = jnp.exp(tot - acum) * dt
    eac = jnp.exp(acum)

    xs = xbc_ref[:, 0:di]
    bm = xbc_ref[:, di:di + gn]
    cm = xbc_ref[:, di + gn:di + 2 * gn]

    prods = jnp.concatenate([cm * bm[s:s + 1, :] for s in range(t)], axis=0)
    rsel = ((_iota((gn, LANES), 1) // hpg == _iota((gn, LANES), 0) // ns)
            & (_iota((gn, LANES), 1) < nheads)).astype(BF)
    cbe = _dot(prods.astype(BF), rsel)
    acl = jnp.concatenate([acum] * t, axis=0)
    acs = jnp.concatenate([jnp.broadcast_to(acum[s:s + 1, :], (t, LANES)) for s in range(t)], axis=0)
    dts = jnp.concatenate([jnp.broadcast_to(dt[s:s + 1, :], (t, LANES)) for s in range(t)], axis=0)
    rr = _iota((t * t, LANES), 0)
    ms = cbe * jnp.exp(jnp.where(rr % t >= rr // t, acl - acs, NEG_INF)) * dts
    ex = (_iota((LANES, di), 1) // hd == _iota((LANES, di), 0)).astype(BF)
    big = _sel_right(jnp.concatenate([ms, eac, coef], axis=0), ex, parts=2)
    y = big[0:t, :] * xs[0:1, :]
    for s in range(1, t):
        y = y + big[s * t:(s + 1) * t, :] * xs[s:s + 1, :]
    eacx = big[t * t:t * t + t, :]
    coefx = big[t * t + t:, :]

    dt_t = jnp.concatenate([dt, jnp.zeros((LANES - t, LANES), F32)], axis=0).T
    a_col = -jnp.exp(alogc_ref[...])
    cd_col = jnp.exp(jnp.sum(dt_t * a_col, axis=1, keepdims=True))

    for g in range(groups):
        gs = slice(g * gp, (g + 1) * gp)
        h0 = h0_ref[gs, :]
        yoff = _dot_nt(cm[:, g * ns:(g + 1) * ns].astype(BF), h0.astype(BF))
        yg = y[:, gs] + yoff * eacx[:, gs] + dsk_ref[:, gs] * xs[:, gs]
        y_ref[:, gs] = _gate_norm(yg, z_ref[:, gs], ng_ref[:, gs]).astype(BF)
        wg = (coefx[:, gs] * xs[:, gs]).astype(BF)
        wpad = jnp.concatenate([wg.astype(F32), jnp.zeros((LANES - t, gp), F32)], axis=0)
        wt = jnp.concatenate([wpad[:, k * LANES:(k + 1) * LANES].T for k in range(gp // LANES)], axis=0)
        bpad = jnp.concatenate([bm[:, g * ns:(g + 1) * ns], jnp.zeros((LANES - t, ns), F32)], axis=0)
        upd = _dot(wt.astype(BF), bpad.astype(BF))
        for r in range(hpg):
            hh = g * hpg + r
            rs = slice(r * hd, (r + 1) * hd)
            hn_ref[g * gp + r * hd:g * gp + (r + 1) * hd, :] = (
                h0[rs, :] * jnp.broadcast_to(cd_col[hh:hh + 1, :], (hd, ns)) + upd[rs, :])


def _ssd_step(xbc, dt, z, alog, alogc, dsk, ng, h0, nheads, groups, hd, ns):
    nb = h0.shape[0]
    m, cch = xbc.shape
    t = m // nb
    di = nheads * hd
    return pl.pallas_call(
        functools.partial(_ssd_step_body, nheads=nheads, groups=groups, hd=hd, ns=ns),
        grid=(nb,),
        in_specs=[
            pl.BlockSpec((t, cch), lambda b: (b, 0)),
            pl.BlockSpec((t, LANES), lambda b: (b, 0)),
            pl.BlockSpec((t, di), lambda b: (b, 0)),
            pl.BlockSpec((1, LANES), lambda b: (0, 0)),
            pl.BlockSpec((LANES, 1), lambda b: (0, 0)),
            pl.BlockSpec((1, di), lambda b: (0, 0)),
            pl.BlockSpec((1, di), lambda b: (0, 0)),
            pl.BlockSpec((None, di, ns), lambda b: (b, 0, 0)),
        ],
        out_specs=[pl.BlockSpec((t, di), lambda b: (b, 0)),
                   pl.BlockSpec((None, di, ns), lambda b: (b, 0, 0))],
        out_shape=[jax.ShapeDtypeStruct((m, di), BF),
                   jax.ShapeDtypeStruct((nb, di, ns), F32)],
        compiler_params=_cparams("parallel"),
        name="ssd_step",
    )(xbc, dt, z, alog, alogc, dsk, ng, h0)


def _pad_lanes(a):
    return jnp.pad(a, [(0, 0)] * (a.ndim - 1) + [(0, LANES - a.shape[-1])])


def _trunk(x, nb, page_table, cache_k, cache_v, cache_lf, conf_prev, mconv_prev, ssm_prev, ffn_prev, w):
    _, t, d = x.shape
    m = nb * t
    stepwise = conf_prev is not None
    nh, dh = cache_k.shape[-2:]
    fw = nh * dh
    x = x.reshape(m, d)

    w_in = w["att_w_in"][0]
    cc = w["conf_dw_w"].shape[2]
    cw = w["conf_dw_w"].shape[1]
    w5 = jnp.concatenate([w_in[:, :3 * fw], w_in[:, 3 * fw + nh:]], axis=1).astype(BF)
    wf = _pad_lanes(w_in[:, 3 * fw:3 * fw + nh]).astype(BF)
    bfp = _pad_lanes(w["att_b_f"][0].reshape(1, nh))
    q, k, v, u, lf = _even_in(x, w["ln_mix"][0].reshape(1, d), w5, wf, bfp, fw, cc)
    logf = lf[:, :nh]
    if stepwise:
        npool, page = cache_k.shape[1:3]
        att = _fox_sample(q, k, v, lf,
                          cache_k[0].reshape(npool, page * nh, dh), cache_v[0].reshape(npool, page * nh, dh),
                          cache_lf[0].reshape(npool, page * nh // LANES, LANES), page_table, nh, dh)
        xp = jnp.concatenate([conf_prev[0], u.reshape(nb, t, cc)], axis=1)
        conf_state = xp[:, t:]
        cv = _conv_step(xp, w["conf_dw_w"][0], w["conf_dw_b"][0], act=False).reshape(m, cc)
    else:
        cq, ckt = _fox_cumsum(lf, nh)
        att = _fox_prompt(q, k, v, cq, ckt.reshape(nh, 1, m), nh, dh)
        conf_state = u[m - (cw - 1):].reshape(1, cw - 1, cc)
        cv = _conv_seq(u, w["conf_dw_w"][0], w["conf_dw_b"][0], act=False)
    w_out = w["att_w_out"][0].astype(BF)
    x = _out_even(x, att, cv, w["conf_ln_g"][0], w["conf_ln_b"][0], w_out[:fw], w_out[fw:])

    ffn_states = []

    def ffn(x, i, final_g):
        if stepwise:
            two_f = ffn_prev.shape[-1]
            st = jnp.zeros((nb, t, two_f), F32).at[:, t - 2:].set(ffn_prev[i]).reshape(m, two_f)
        else:
            st = None
        x, us = _ffn(x, w["ln_ffn"][i], w["ffn_w_up"][i].astype(BF), w["ffn_dw_w"][i], w["ffn_dw_b"][i],
                     w["ffn_w_down"][i].astype(BF), st, final_g, t)
        ffn_states.append(us)
        return x

    x = ffn(x, 0, None)

    nheads, hd, ns = w["ssm_state_shape"]
    di = nheads * hd
    cch = w["ssm_conv_w"].shape[2]
    groups = (cch - di) // (2 * ns)
    w_in = w["ssm_w_in"][0]
    wdt = _pad_lanes(w_in[:, di + cch:]).astype(BF)
    dtb = _pad_lanes(w["ssm_dt_bias"][0].reshape(1, nheads))
    sw = w["ssm_conv_w"].shape[1]
    if stepwise:
        st = jnp.zeros((nb, t, cch), F32).at[:, t - (sw - 1):].set(mconv_prev[0]).reshape(m, cch)
    else:
        st = None
    z, xbc, dt, mconv_state = _ssm_in(x, w["ln_mix"][1].reshape(1, d), w_in[:, :di + cch].astype(BF), wdt, dtb,
                                      w["ssm_conv_w"][0], w["ssm_conv_b"][0], st, di, cch, t)
    alog = _pad_lanes(w["ssm_a_log"][0].reshape(1, nheads))
    dsk = jnp.repeat(w["ssm_d"][0], hd).reshape(1, di)
    ng = w["ssm_norm_g"][0].reshape(1, di)
    if stepwise:
        yn, hn = _ssd_step(xbc, dt, z, alog, alog.reshape(LANES, 1), dsk, ng,
                           ssm_prev[0].reshape(nb, di, ns), nheads, groups, hd, ns)
        ssm_state = hn.reshape(nb, nheads, hd, ns)
    else:
        chunk = _tile(m, 128)
        yn, hout = _ssd_seq(xbc, dt, z, alog, dsk, ng, nheads, groups, hd, ns, chunk)
        hpg = nheads // groups
        ssm_state = jnp.transpose(hout.reshape(groups, ns, hpg, hd), (0, 2, 3, 1)).reshape(1, nheads, hd, ns)
    x = _mm_res(x, yn, w["ssm_w_out"][0].astype(BF))
    x = ffn(x, 1, w["ln_final"])

    return (x.reshape(nb, t, d),
            k.reshape(1, nb, t, nh, dh), v.reshape(1, nb, t, nh, dh), logf.reshape(1, nb, t, nh),
            conf_state[None], mconv_state[None], ssm_state[None], jnp.stack(ffn_states))


def kernel(x_prompt, x_sample, cache_k, cache_v, cache_logf, state_conf_conv, state_ssm_conv, state_ssm,
           state_ffn_conv, page_table,
           ln_mix, ln_ffn, ln_final,
           att_w_in, att_b_f, conf_dw_w, conf_dw_b, conf_ln_g, conf_ln_b, att_w_out,
           ssm_w_in, ssm_conv_w, ssm_conv_b, ssm_dt_bias, ssm_a_log, ssm_d, ssm_norm_g, ssm_w_out,
           ffn_w_up, ffn_dw_w, ffn_dw_b, ffn_w_down):
    assert ln_mix.shape[0] == 2 and x_prompt.shape[0] == 1, "two layers, one prompt sequence"
    w = dict(ln_mix=ln_mix, ln_ffn=ln_ffn, ln_final=ln_final,
             att_w_in=att_w_in, att_b_f=att_b_f, conf_dw_w=conf_dw_w, conf_dw_b=conf_dw_b,
             conf_ln_g=conf_ln_g, conf_ln_b=conf_ln_b, att_w_out=att_w_out,
             ssm_w_in=ssm_w_in, ssm_conv_w=ssm_conv_w, ssm_conv_b=ssm_conv_b, ssm_dt_bias=ssm_dt_bias,
             ssm_a_log=ssm_a_log, ssm_d=ssm_d, ssm_norm_g=ssm_norm_g, ssm_w_out=ssm_w_out,
             ffn_w_up=ffn_w_up, ffn_dw_w=ffn_dw_w, ffn_dw_b=ffn_dw_b, ffn_w_down=ffn_w_down,
             ssm_state_shape=state_ssm.shape[2:])
    nb = x_sample.shape[0]
    outs_p = _trunk(x_prompt, 1, None, cache_k, cache_v, cache_logf, None, None, None, None, w)
    outs_s = _trunk(x_sample, nb, page_table, cache_k, cache_v, cache_logf,
                    state_conf_conv, state_ssm_conv, state_ssm, state_ffn_conv, w)
    return (outs_p[0], outs_s[0]) + tuple(outs_p[1:]) + tuple(outs_s[1:])
```

```python
import functools

import jax
import jax.numpy as jnp
from jax import lax
from jax.experimental import pallas as pl
from jax.experimental.pallas import tpu as pltpu

EPS = 1e-6
BF = jnp.bfloat16
F32 = jnp.float32
LANES = 128
SUBLANES = 8
VMEM_LIMIT = 56 * 1024 * 1024
NEG_INF = float("-inf")
PROJ_ROWS = 1024
FFN_ROWS = 512


def _cparams(*sem):
    return pltpu.CompilerParams(dimension_semantics=sem, vmem_limit_bytes=VMEM_LIMIT)


def _tile(n, pref):
    t = pref
    while t >= LANES and t % LANES == 0:
        if n % t == 0:
            return t
        t //= 2
    return n


def _iota(shape, dim):
    return lax.broadcasted_iota(jnp.int32, shape, dim)


def _split3(x):
    hi = x.astype(BF)
    r1 = x - hi.astype(F32)
    mid = r1.astype(BF)
    lo = (r1 - mid.astype(F32)).astype(BF)
    return hi, mid, lo


def _dot(a, b):
    return jnp.dot(a, b, preferred_element_type=F32)


def _dot_nt(a, b):
    return lax.dot_general(a, b, (((1,), (1,)), ((), ())), preferred_element_type=F32)


def _sel_left(sel, x, parts=3):
    ps = _split3(x)[:parts]
    out = _dot(sel, ps[0])
    for p in ps[1:]:
        out = out + _dot(sel, p)
    return out


def _sel_right(x, sel, parts=3):
    ps = _split3(x)[:parts]
    out = _dot(ps[0], sel)
    for p in ps[1:]:
        out = out + _dot(p, sel)
    return out


def _log_sigmoid(x):
    return jnp.minimum(x, 0.0) - jnp.log(1.0 + jnp.exp(-jnp.abs(x)))


def _softplus(x):
    return jnp.maximum(x, 0.0) + jnp.log(1.0 + jnp.exp(-jnp.abs(x)))


def _silu(x):
    return x * jax.nn.sigmoid(x)


def _rms(x, g):
    return x * lax.rsqrt(jnp.mean(x * x, axis=-1, keepdims=True) + EPS) * g


def _tri(n):
    return (_iota((n, n), 1) <= _iota((n, n), 0)).astype(BF)


def _even_in_body(x_ref, g_ref, w_ref, wf_ref, bf_ref,
                  q_ref, k_ref, v_ref, u_ref, lf_ref, xn_ref, a_ref, *, nf, nc):
    j = pl.program_id(1)

    @pl.when(j == 0)
    def _():
        xn = _rms(x_ref[...], g_ref[...]).astype(BF)
        xn_ref[...] = xn
        lf_ref[...] = _log_sigmoid(_dot(xn, wf_ref[...]) + bf_ref[...])

    z = _dot(xn_ref[...], w_ref[...])

    @pl.when(j < nf)
    def _():
        q_ref[...] = z

    @pl.when((j >= nf) & (j < 2 * nf))
    def _():
        k_ref[...] = z

    @pl.when((j >= 2 * nf) & (j < 3 * nf))
    def _():
        v_ref[...] = z

    @pl.when((j >= 3 * nf) & ((j - 3 * nf) % 2 == 0))
    def _():
        a_ref[...] = z

    @pl.when((j >= 3 * nf) & ((j - 3 * nf) % 2 == 1))
    def _():
        u_ref[...] = a_ref[...] * jax.nn.sigmoid(z)


def _even_in(x, g, w5, wf, bfp, fw, cc):
    m, d = x.shape
    tm = _tile(m, PROJ_ROWS)
    tn = min(_tile(fw, 512), _tile(cc, 512))
    nf, nc = fw // tn, cc // tn
    nj = 3 * nf + 2 * nc

    def w_map(i, j):
        jj = j - 3 * nf
        return (0, jnp.where(j < 3 * nf, j, 3 * nf + (jj % 2) * nc + jj // 2))

    def seg_map(base, n, div=1):
        return lambda i, j: (i, jnp.clip((j - base) // div, 0, n - 1))

    return pl.pallas_call(
        functools.partial(_even_in_body, nf=nf, nc=nc),
        grid=(m // tm, nj),
        in_specs=[
            pl.BlockSpec((tm, d), lambda i, j: (i, 0)),
            pl.BlockSpec((1, d), lambda i, j: (0, 0)),
            pl.BlockSpec((d, tn), w_map),
            pl.BlockSpec((d, LANES), lambda i, j: (0, 0)),
            pl.BlockSpec((1, LANES), lambda i, j: (0, 0)),
        ],
        out_specs=[
            pl.BlockSpec((tm, tn), seg_map(0, nf)),
            pl.BlockSpec((tm, tn), seg_map(nf, nf)),
            pl.BlockSpec((tm, tn), seg_map(2 * nf, nf)),
            pl.BlockSpec((tm, tn), seg_map(3 * nf, nc, 2)),
            pl.BlockSpec((tm, LANES), lambda i, j: (i, 0)),
        ],
        out_shape=[
            jax.ShapeDtypeStruct((m, fw), F32),
            jax.ShapeDtypeStruct((m, fw), F32),
            jax.ShapeDtypeStruct((m, fw), F32),
            jax.ShapeDtypeStruct((m, cc), F32),
            jax.ShapeDtypeStruct((m, LANES), F32),
        ],
        scratch_shapes=[pltpu.VMEM((tm, d), BF), pltpu.VMEM((tm, tn), F32)],
        compiler_params=_cparams("parallel", "arbitrary"),
        name="even_in",
    )(x, g, w5, wf, bfp)


def _fox_cumsum_body(lf_ref, cq_ref, ckt_ref, carry_ref, *, nh):
    @pl.when(pl.program_id(0) == 0)
    def _():
        carry_ref[...] = jnp.zeros_like(carry_ref)

    lf = lf_ref[...]
    tb = lf.shape[0]
    cs = _sel_left(_tri(tb), lf) + carry_ref[...]
    carry_ref[...] = cs[tb - 1:tb, :]
    for h in range(nh):
        cq_ref[h] = jnp.broadcast_to(cs[:, h:h + 1], (tb, LANES))
    ckt_ref[...] = cs.T[:nh, :]


def _fox_cumsum(lf, nh):
    m = lf.shape[0]
    tb = _tile(m, 256)
    return pl.pallas_call(
        functools.partial(_fox_cumsum_body, nh=nh),
        grid=(m // tb,),
        in_specs=[pl.BlockSpec((tb, LANES), lambda i: (i, 0))],
        out_specs=[pl.BlockSpec((nh, tb, LANES), lambda i: (0, i, 0)),
                   pl.BlockSpec((nh, tb), lambda i: (0, i))],
        out_shape=[jax.ShapeDtypeStruct((nh, m, LANES), F32),
                   jax.ShapeDtypeStruct((nh, m), F32)],
        scratch_shapes=[pltpu.VMEM((1, LANES), F32)],
        compiler_params=_cparams("arbitrary"),
        name="fox_cumsum",
    )(lf)


def _fox_prompt_body(q_ref, k_ref, v_ref, cq_ref, ck_ref, o_ref, kb_ref, vb_ref, m_ref, acc_ref, s_ref, *, scale):
    qi = pl.program_id(1)
    t, dh = q_ref.shape
    log2e = 1.4426950408889634

    @pl.when(qi == 0)
    def _():
        kb_ref[...] = k_ref[...].astype(BF)
        vb_ref[:, 0:dh] = v_ref[...].astype(BF)
        vb_ref[:, dh:] = (_iota((k_ref.shape[0], dh), 1) == 0).astype(BF)

    qs = (q_ref[...] * (scale * log2e)).astype(BF)
    m_ref[...] = jnp.full_like(m_ref, NEG_INF)
    acc_ref[...] = jnp.zeros_like(acc_ref)
    reps = t // LANES
    rc = min(t, 64)

    def scores(kj, slot):
        off = pl.multiple_of(kj * t, t)
        s_ref[slot] = _dot_nt(qs, kb_ref[pl.ds(off, t), :]) - ck_ref[kj] * log2e

    def update(kj, slot, masked):
        off = pl.multiple_of(kj * t, t)
        for r in range(t // rc):
            rows = slice(r * rc, (r + 1) * rc)
            s = s_ref[slot, rows, :]
            if masked:
                s = jnp.where(_iota((rc, t), 1) <= r * rc + _iota((rc, t), 0), s, NEG_INF)
            cq = cq_ref[rows, :] * log2e
            m_prev = m_ref[rows, :]
            m_new = jnp.maximum(m_prev, jnp.max(s, axis=1, keepdims=True) + cq)
            shift = m_new - cq
            p = jnp.exp2(s - jnp.concatenate([shift] * reps, axis=1))
            alpha = jnp.exp2(m_prev - m_new)
            acc_ref[rows, :] = (jnp.concatenate([alpha] * (2 * dh // LANES), axis=1) * acc_ref[rows, :]
                                + _dot(p.astype(BF), vb_ref[pl.ds(off, t), :]))
            m_ref[rows, :] = m_new

    scores(0, 0)

    def body(k2, carry):
        update(2 * k2, 0, False)
        scores(2 * k2 + 1, 1)
        update(2 * k2 + 1, 1, False)
        scores(2 * k2 + 2, 0)
        return carry

    lax.fori_loop(0, qi // 2, body, 0)

    @pl.when(qi % 2 == 0)
    def _():
        update(qi, 0, True)

    @pl.when(qi % 2 == 1)
    def _():
        update(qi - 1, 0, False)
        scores(qi, 1)
        update(qi, 1, True)

    o_ref[...] = acc_ref[:, 0:dh] / acc_ref[:, dh:dh + 1]


def _fox_prompt(q, k, v, cq, ck, nh, dh):
    m = q.shape[0]
    t = _tile(m, 512)
    n = m // t
    return pl.pallas_call(
        functools.partial(_fox_prompt_body, scale=dh ** -0.5),
        grid=(nh, n),
        in_specs=[
            pl.BlockSpec((t, dh), lambda h, i: (i, h)),
            pl.BlockSpec((m, dh), lambda h, i: (0, h)),
            pl.BlockSpec((m, dh), lambda h, i: (0, h)),
            pl.BlockSpec((None, t, LANES), lambda h, i: (h, i, 0)),
            pl.BlockSpec((None, n, 1, t), lambda h, i: (h, 0, 0, 0)),
        ],
        out_specs=pl.BlockSpec((t, dh), lambda h, i: (i, h)),
        out_shape=jax.ShapeDtypeStruct((m, nh * dh), F32),
        scratch_shapes=[pltpu.VMEM((m, dh), BF), pltpu.VMEM((m, 2 * dh), BF),
                        pltpu.VMEM((t, LANES), F32), pltpu.VMEM((t, 2 * dh), F32),
                        pltpu.VMEM((2, t, t), F32)],
        compiler_params=_cparams("parallel", "arbitrary"),
        name="fox_prompt",
    )(q, k, v, cq, ck.reshape(nh, n, 1, t))


def _fox_sample_body(pt_ref, q_ref, kn_ref, vn_ref, lfn_ref, *rest, npg, nh, dh, scale):
    k_refs = rest[:npg]
    v_refs = rest[npg:2 * npg]
    lf_refs = rest[2 * npg:3 * npg]
    o_ref, s_ref, acc_ref = rest[3 * npg:]
    t, fw = q_ref.shape
    page = k_refs[0].shape[0] // nh
    ht = nh * t
    ppr = LANES // nh

    def load_page(ref):
        return jnp.concatenate([ref[pl.ds(h, page, stride=nh), :] for h in range(nh)], axis=1)

    q = q_ref[...]
    qt = jnp.concatenate([q] * nh + [jnp.zeros((LANES - ht, fw), F32)], axis=0)
    qbd = jnp.where(_iota((LANES, fw), 0) // t == _iota((LANES, fw), 1) // dh, qt, 0.0).astype(BF)

    r2 = _iota((LANES, LANES), 0)
    c2 = _iota((LANES, LANES), 1)
    e_past = ((r2 % nh == c2 // t) & (c2 < ht)).astype(BF)
    e_new = ((r2 == c2 // t) & (c2 < ht)).astype(BF)
    tri = _tri(page)
    rp = _iota((page, LANES), 0)
    cp = _iota((page, LANES), 1)

    carry = jnp.zeros((1, LANES), F32)
    for p in range(npg):
        lfp = lf_refs[p][...]
        b = jnp.concatenate([jnp.broadcast_to(lfp[r:r + 1, :], (ppr, LANES))
                             for r in range(page // ppr)], axis=0)
        b = jnp.where(cp // nh == rp % ppr, b, 0.0)
        cpage = _sel_left(tri, _sel_right(b, e_past)) + carry
        carry = cpage[page - 1:page, :]
        st = _dot_nt(load_page(k_refs[p]).astype(BF), qbd) * scale
        s_ref[p * page:(p + 1) * page, :] = st - cpage

    y = _sel_right(lfn_ref[...], e_new)
    rt = _iota((t, LANES), 0)
    sh = 1
    while sh < t:
        y = y + jnp.where(rt >= sh, pltpu.roll(y, sh, 0), 0.0)
        sh *= 2
    cn = carry + y
    cq = jnp.sum(jnp.where(rt == _iota((t, LANES), 1) % t, cn, 0.0), axis=0, keepdims=True)
    knp = jnp.concatenate([kn_ref[...], jnp.zeros((page - t, fw), F32)], axis=0)
    stn = _dot_nt(knp.astype(BF), qbd) * scale
    cnp = jnp.concatenate([cn, jnp.zeros((page - t, LANES), F32)], axis=0)
    s_ref[npg * page:, :] = jnp.where((rp < t) & (rp <= cp % t), stn - cnp, NEG_INF)

    mx = jnp.max(s_ref[0:page, :], axis=0, keepdims=True)
    for p in range(1, npg + 1):
        mx = jnp.maximum(mx, jnp.max(s_ref[p * page:(p + 1) * page, :], axis=0, keepdims=True))
    shift = cq - (mx + cq)

    acc_ref[...] = jnp.zeros_like(acc_ref)
    lsum = jnp.zeros((LANES, 1), F32)
    for p in range(npg + 1):
        pt = jnp.exp(s_ref[p * page:(p + 1) * page, :] + shift).T
        lsum = lsum + jnp.sum(pt, axis=1, keepdims=True)
        if p < npg:
            vp = load_page(v_refs[p])
        else:
            vp = jnp.concatenate([vn_ref[...], jnp.zeros((page - t, fw), F32)], axis=0)
        acc_ref[...] += _dot(pt[:ht, :].astype(BF), vp.astype(BF))

    for h in range(nh):
        o_ref[:, h * dh:(h + 1) * dh] = (acc_ref[h * t:(h + 1) * t, h * dh:(h + 1) * dh]
                                          / lsum[h * t:(h + 1) * t, :])


def _fox_sample(q, kn, vn, lfn, ck, cv, clf, page_table, nh, dh):
    nb, npg = page_table.shape
    m, fw = q.shape
    t = m // nb
    prows = ck.shape[1]
    page = prows // nh
    rows = page * nh // LANES

    def pg_map(p):
        return lambda b, pt: (pt[b * npg + p], 0, 0)

    seq = pl.BlockSpec((t, fw), lambda b, pt: (b, 0))
    in_specs = [seq, seq, seq, pl.BlockSpec((t, LANES), lambda b, pt: (b, 0))]
    in_specs += [pl.BlockSpec((None, prows, dh), pg_map(p)) for p in range(npg)]
    in_specs += [pl.BlockSpec((None, prows, dh), pg_map(p)) for p in range(npg)]
    in_specs += [pl.BlockSpec((None, rows, LANES), pg_map(p)) for p in range(npg)]
    return pl.pallas_call(
        functools.partial(_fox_sample_body, npg=npg, nh=nh, dh=dh, scale=dh ** -0.5),
        grid_spec=pltpu.PrefetchScalarGridSpec(
            num_scalar_prefetch=1,
            grid=(nb,),
            in_specs=in_specs,
            out_specs=pl.BlockSpec((t, fw), lambda b, pt: (b, 0)),
            scratch_shapes=[pltpu.VMEM(((npg + 1) * page, LANES), F32),
                            pltpu.VMEM((nh * t, fw), F32)],
        ),
        out_shape=jax.ShapeDtypeStruct((m, fw), F32),
        compiler_params=_cparams("parallel"),
        name="fox_sample",
    )(page_table.reshape(-1), q, kn, vn, lfn, *([ck] * npg), *([cv] * npg), *([clf] * npg))


def _conv_seq_body(xm_ref, xh_ref, w_ref, b_ref, o_ref, xf_ref, *, width, act):
    tl = xm_ref.shape[0]
    hb = xh_ref.shape[0]
    xf_ref[0:hb, :] = jnp.where(pl.program_id(0) == 0, 0.0, xh_ref[...])
    xf_ref[hb:, :] = xm_ref[...]
    base = hb - (width - 1)
    acc = b_ref[...] + w_ref[0:1, :] * xf_ref[base:base + tl, :]
    for j in range(1, width):
        acc = acc + w_ref[j:j + 1, :] * xf_ref[base + j:base + j + tl, :]
    o_ref[...] = _silu(acc) if act else acc


def _conv_seq(x, w, b, act):
    l, c = x.shape
    width = w.shape[0]
    hb = -(-(width - 1) // SUBLANES) * SUBLANES
    tl = _tile(l, 256)
    tc = _tile(c, 256)
    r = tl // hb
    return pl.pallas_call(
        functools.partial(_conv_seq_body, width=width, act=act),
        grid=(l // tl, c // tc),
        in_specs=[
            pl.BlockSpec((tl, tc), lambda i, j: (i, j)),
            pl.BlockSpec((hb, tc), lambda i, j: (jnp.maximum(i * r - 1, 0), j)),
            pl.BlockSpec((width, tc), lambda i, j: (0, j)),
            pl.BlockSpec((1, tc), lambda i, j: (0, j)),
        ],
        out_specs=pl.BlockSpec((tl, tc), lambda i, j: (i, j)),
        out_shape=jax.ShapeDtypeStruct((l, c), F32),
        scratch_shapes=[pltpu.VMEM((tl + hb, tc), F32)],
        compiler_params=_cparams("parallel", "parallel"),
        name="conv_seq",
    )(x, x, w, b.reshape(1, c))


def _conv_step_body(xp_ref, w_ref, b_ref, o_ref, *, width, act):
    t = o_ref.shape[1]
    acc = b_ref[...] + w_ref[0:1, :] * xp_ref[:, 0:t, :]
    for j in range(1, width):
        acc = acc + w_ref[j:j + 1, :] * xp_ref[:, j:j + t, :]
    o_ref[...] = _silu(acc) if act else acc


def _conv_step(xp, w, b, act):
    nb, rows, c = xp.shape
    width = w.shape[0]
    t = rows - (width - 1)
    bb = 16 if nb % 16 == 0 else nb
    tc = _tile(c, 512)
    return pl.pallas_call(
        functools.partial(_conv_step_body, width=width, act=act),
        grid=(nb // bb, c // tc),
        in_specs=[
            pl.BlockSpec((bb, rows, tc), lambda i, j: (i, 0, j)),
            pl.BlockSpec((width, tc), lambda i, j: (0, j)),
            pl.BlockSpec((1, tc), lambda i, j: (0, j)),
        ],
        out_specs=pl.BlockSpec((bb, t, tc), lambda i, j: (i, 0, j)),
        out_shape=jax.ShapeDtypeStruct((nb, t, c), F32),
        compiler_params=_cparams("parallel", "parallel"),
        name="conv_step",
    )(xp, w, b.reshape(1, c))


def _out_even_body(x_ref, att_ref, cv_ref, lg_ref, lb_ref, wa_ref, wc_ref, o_ref, ab_ref, cb_ref):
    @pl.when(pl.program_id(1) == 0)
    def _():
        u = cv_ref[...]
        xc = u - jnp.mean(u, axis=-1, keepdims=True)
        var = jnp.mean(xc * xc, axis=-1, keepdims=True)
        cb_ref[...] = _silu(xc * lax.rsqrt(var + EPS) * lg_ref[...] + lb_ref[...]).astype(BF)
        ab_ref[...] = att_ref[...].astype(BF)

    o_ref[...] = x_ref[...] + _dot(ab_ref[...], wa_ref[...]) + _dot(cb_ref[...], wc_ref[...])


def _out_even(x, att, cv, lg, lb, wa, wc):
    m, d = x.shape
    fw, cc = att.shape[1], cv.shape[1]
    tm = _tile(m, PROJ_ROWS)
    tn = _tile(d, 512)
    return pl.pallas_call(
        _out_even_body,
        grid=(m // tm, d // tn),
        in_specs=[
            pl.BlockSpec((tm, tn), lambda i, j: (i, j)),
            pl.BlockSpec((tm, fw), lambda i, j: (i, 0)),
            pl.BlockSpec((tm, cc), lambda i, j: (i, 0)),
            pl.BlockSpec((1, cc), lambda i, j: (0, 0)),
            pl.BlockSpec((1, cc), lambda i, j: (0, 0)),
            pl.BlockSpec((fw, tn), lambda i, j: (0, j)),
            pl.BlockSpec((cc, tn), lambda i, j: (0, j)),
        ],
        out_specs=pl.BlockSpec((tm, tn), lambda i, j: (i, j)),
        out_shape=jax.ShapeDtypeStruct((m, d), F32),
        scratch_shapes=[pltpu.VMEM((tm, fw), BF), pltpu.VMEM((tm, cc), BF)],
        compiler_params=_cparams("parallel", "arbitrary"),
        name="out_even",
    )(x, att, cv, lg.reshape(1, cc), lb.reshape(1, cc), wa, wc)


def _mm_res_body(x_ref, a_ref, w_ref, o_ref):
    o_ref[...] = x_ref[...] + _dot(a_ref[...], w_ref[...])


def _mm_res(x, a, w):
    m, d = x.shape
    kk = a.shape[1]
    tm = _tile(m, PROJ_ROWS)
    tn = _tile(d, 512)
    return pl.pallas_call(
        _mm_res_body,
        grid=(m // tm, d // tn),
        in_specs=[
            pl.BlockSpec((tm, tn), lambda i, j: (i, j)),
            pl.BlockSpec((tm, kk), lambda i, j: (i, 0)),
            pl.BlockSpec((kk, tn), lambda i, j: (0, j)),
        ],
        out_specs=pl.BlockSpec((tm, tn), lambda i, j: (i, j)),
        out_shape=jax.ShapeDtypeStruct((m, d), F32),
        compiler_params=_cparams("parallel", "arbitrary"),
        name="mm_res",
    )(x, a, w)


def _causal_conv(u, w_ref, b_ref, prev, seq):
    tm = u.shape[0]
    width = w_ref.shape[0]
    row = _iota(prev.shape, 0)
    acc = b_ref[...] + w_ref[width - 1:width, :] * u
    for s in range(1, width):
        r = pltpu.roll(u, s, 0)
        if seq is None:
            head = jnp.where(row < s, pltpu.roll(prev, s, 0), r[:SUBLANES])
            r = jnp.concatenate([head, r[SUBLANES:]], axis=0)
        else:
            r = jnp.where(row % seq < s, pltpu.roll(prev, (tm - seq + s) % tm, 0), r)
        acc = acc + w_ref[width - 1 - s:width - s, :] * r
    return acc


def _save_seq_tails(u, us_ref, col, ub_ref, seq, keep):
    tm, c = u.shape
    for k in range(c // LANES):
        ub_ref[...] = u[:, k * LANES:(k + 1) * LANES]
        for r in range(keep):
            us_ref[r, :, col + k * LANES:col + (k + 1) * LANES] = (
                ub_ref[pl.ds(seq - keep + r, tm // seq, stride=seq), :])


def _ffn_body(*refs, stepwise, final, seq, nf):
    x_ref, g_ref, wa_ref, wv_ref, dwa_ref, dwv_ref, ba_ref, bv_ref, wd_ref = refs[:9]
    k = 9
    if stepwise:
        sa_ref, sv_ref = refs[k:k + 2]
        k += 2
    if final:
        fg_ref = refs[k]
        k += 1
    o_ref, us_ref, xn_ref, ua0_ref, uv0_ref, ua1_ref, uv1_ref = refs[k:k + 7]
    k += 7
    slots = ((ua0_ref, uv0_ref), (ua1_ref, uv1_ref))
    i = pl.program_id(0)
    j = pl.program_id(1)
    tm, tf = x_ref.shape[0], wa_ref.shape[1]

    def up(slot):
        xn = xn_ref[...]
        slots[slot][0][...] = _dot(xn, wa_ref[...])
        slots[slot][1][...] = _dot(xn, wv_ref[...])

    def tail(slot):
        jb = j - 1
        ua, uv = slots[slot][0][...], slots[slot][1][...]
        if stepwise:
            pa, pv = sa_ref[...], sv_ref[...]
            _save_seq_tails(ua, us_ref, 0, refs[k], seq, 2)
            _save_seq_tails(uv, us_ref, tf, refs[k], seq, 2)
        else:
            ca_ref, cv_ref = refs[k:k + 2]

            @pl.when(i == 0)
            def _():
                ca_ref[jb] = jnp.zeros((SUBLANES, tf), F32)
                cv_ref[jb] = jnp.zeros((SUBLANES, tf), F32)

            pa, pv = ca_ref[jb], cv_ref[jb]
            ca_ref[jb] = ua[tm - SUBLANES:, :]
            cv_ref[jb] = uv[tm - SUBLANES:, :]
            us_ref[:, 0:tf] = ua[tm - SUBLANES:, :]
            us_ref[:, tf:2 * tf] = uv[tm - SUBLANES:, :]
        cseq = seq if stepwise else None
        a = _causal_conv(ua, dwa_ref, ba_ref, pa, cseq)
        v = _causal_conv(uv, dwv_ref, bv_ref, pv, cseq)
        o_ref[...] += _dot((_silu(a) * v).astype(BF), wd_ref[...])

    @pl.when(j == 0)
    def _():
        xn_ref[...] = _rms(x_ref[...], g_ref[...]).astype(BF)
        o_ref[...] = x_ref[...]
        up(0)

    for parity in (0, 1):
        @pl.when((j > 0) & (j < nf) & (j % 2 == parity))
        def _():
            up(parity)
            tail(1 - parity)

    @pl.when(j == nf)
    def _():
        tail((nf - 1) % 2)
        if final:
            o_ref[...] = _rms(o_ref[...], fg_ref[...])


def _ffn(x, g, w_up, dw_w, dw_b, w_down, state, final_g, seq):
    m, d = x.shape
    f = w_down.shape[0]
    tm = _tile(m, FFN_ROWS)
    tf = _tile(f, 512)
    nf = f // tf
    stepwise = state is not None
    final = final_g is not None
    row_spec = lambda i, j: (i, 0)
    ju = lambda j: jnp.minimum(j, nf - 1)
    jt = lambda j: jnp.maximum(j - 1, 0)
    in_specs = [
        pl.BlockSpec((tm, d), row_spec),
        pl.BlockSpec((1, d), lambda i, j: (0, 0)),
        pl.BlockSpec((d, tf), lambda i, j: (0, ju(j))),
        pl.BlockSpec((d, tf), lambda i, j: (0, ju(j) + nf)),
        pl.BlockSpec((3, tf), lambda i, j: (0, jt(j))),
        pl.BlockSpec((3, tf), lambda i, j: (0, jt(j) + nf)),
        pl.BlockSpec((1, tf), lambda i, j: (0, jt(j))),
        pl.BlockSpec((1, tf), lambda i, j: (0, jt(j) + nf)),
        pl.BlockSpec((tf, d), lambda i, j: (jt(j), 0)),
    ]
    args = [x, g.reshape(1, d), w_up, w_up, dw_w, dw_w, dw_b.reshape(1, 2 * f), dw_b.reshape(1, 2 * f), w_down]
    scratch = [pltpu.VMEM((tm, d), BF)] + [pltpu.VMEM((tm, tf), F32)] * 4
    if stepwise:
        in_specs += [pl.BlockSpec((tm, tf), lambda i, j: (i, jt(j))),
                     pl.BlockSpec((tm, tf), lambda i, j: (i, jt(j) + nf))]
        args += [state, state]
        us_shape = jax.ShapeDtypeStruct((2, nf, m // seq, 2 * tf), F32)
        us_spec = pl.BlockSpec((2, None, tm // seq, 2 * tf), lambda i, j: (0, jt(j), i, 0))
        scratch += [pltpu.VMEM((tm, LANES), F32)]
    else:
        us_shape = jax.ShapeDtypeStruct((m // tm, nf, SUBLANES, 2 * tf), F32)
        us_spec = pl.BlockSpec((None, None, SUBLANES, 2 * tf), lambda i, j: (i, jt(j), 0, 0))
        scratch += [pltpu.VMEM((nf, SUBLANES, tf), F32), pltpu.VMEM((nf, SUBLANES, tf), F32)]
    if final:
        in_specs.append(pl.BlockSpec((1, d), lambda i, j: (0, 0)))
        args.append(final_g.reshape(1, d))
    out, us = pl.pallas_call(
        functools.partial(_ffn_body, stepwise=stepwise, final=final, seq=seq, nf=nf),
        grid=(m // tm, nf + 1),
        in_specs=in_specs,
        out_specs=[pl.BlockSpec((tm, d), row_spec), us_spec],
        out_shape=[jax.ShapeDtypeStruct((m, d), F32), us_shape],
        scratch_shapes=scratch,
        compiler_params=_cparams("arbitrary", "arbitrary"),
        name="conv_ffn",
    )(*args)
    if stepwise:
        halves = [jnp.transpose(h, (2, 0, 1, 3)).reshape(m // seq, 2, f) for h in (us[..., :tf], us[..., tf:])]
    else:
        us = us[m // tm - 1]
        halves = [jnp.swapaxes(h, 0, 1).reshape(SUBLANES, f)[None, SUBLANES - 2:]
                  for h in (us[..., :tf], us[..., tf:])]
    return out, jnp.concatenate(halves, axis=-1)


def _ssm_in_body(*refs, nz, stepwise, seq):
    x_ref, g_ref, w_ref, wdt_ref, dtb_ref, cw_ref, cb_ref = refs[:7]
    k = 7
    if stepwise:
        st_ref = refs[k]
        k += 1
    z_ref, xbc_ref, dt_ref, us_ref, xn_ref, aux_ref = refs[k:k + 6]
    i = pl.program_id(0)
    j = pl.program_id(1)
    tm = x_ref.shape[0]
    keep = cw_ref.shape[0] - 1

    @pl.when(j == 0)
    def _():
        xn = _rms(x_ref[...], g_ref[...]).astype(BF)
        xn_ref[...] = xn
        dt_ref[...] = _softplus(_dot(xn, wdt_ref[...]) + dtb_ref[...])

    z = _dot(xn_ref[...], w_ref[...])

    @pl.when(j < nz)
    def _():
        z_ref[...] = z

    @pl.when(j >= nz)
    def _():
        if stepwise:
            prev = st_ref[...]
            _save_seq_tails(z, us_ref, 0, aux_ref, seq, keep)
        else:
            jx = j - nz

            @pl.when(i == 0)
            def _():
                aux_ref[jx] = jnp.zeros(aux_ref.shape[1:], F32)

            prev = aux_ref[jx]
            aux_ref[jx] = z[tm - SUBLANES:, :]
            us_ref[...] = z[tm - SUBLANES:, :]
        xbc_ref[...] = _silu(_causal_conv(z, cw_ref, cb_ref, prev, seq if stepwise else None))


def _ssm_in(x, g, w, wdt, dtb, cw, cb, state, di, cch, seq):
    m, d = x.shape
    tm = _tile(m, PROJ_ROWS)
    tn = min(_tile(di, 512), _tile(cch, 512))
    nz, nx = di // tn, cch // tn
    width = cw.shape[0]
    keep = width - 1
    stepwise = state is not None
    xcol = lambda i, j: (0, jnp.clip(j - nz, 0, nx - 1))
    in_specs = [
        pl.BlockSpec((tm, d), lambda i, j: (i, 0)),
        pl.BlockSpec((1, d), lambda i, j: (0, 0)),
        pl.BlockSpec((d, tn), lambda i, j: (0, j)),
        pl.BlockSpec((d, LANES), lambda i, j: (0, 0)),
        pl.BlockSpec((1, LANES), lambda i, j: (0, 0)),
        pl.BlockSpec((width, tn), xcol),
        pl.BlockSpec((1, tn), xcol),
    ]
    args = [x, g, w, wdt, dtb, cw, cb.reshape(1, cch)]
    if stepwise:
        in_specs.append(pl.BlockSpec((tm, tn), lambda i, j: (i, jnp.clip(j - nz, 0, nx - 1))))
        args.append(state)
        us_shape = jax.ShapeDtypeStruct((keep, nx, m // seq, tn), F32)
        us_spec = pl.BlockSpec((keep, None, tm // seq, tn), lambda i, j: (0, jnp.clip(j - nz, 0, nx - 1), i, 0))
        aux = pltpu.VMEM((tm, LANES), F32)
    else:
        us_shape = jax.ShapeDtypeStruct((m // tm, nx, SUBLANES, tn), F32)
        us_spec = pl.BlockSpec((None, None, SUBLANES, tn), lambda i, j: (i, jnp.clip(j - nz, 0, nx - 1), 0, 0))
        aux = pltpu.VMEM((nx, SUBLANES, tn), F32)
    z, xbc, dt, us = pl.pallas_call(
        functools.partial(_ssm_in_body, nz=nz, stepwise=stepwise, seq=seq),
        grid=(m // tm, nz + nx),
        in_specs=in_specs,
        out_specs=[
            pl.BlockSpec((tm, tn), lambda i, j: (i, jnp.clip(j, 0, nz - 1))),
            pl.BlockSpec((tm, tn), lambda i, j: (i, jnp.clip(j - nz, 0, nx - 1))),
            pl.BlockSpec((tm, LANES), lambda i, j: (i, 0)),
            us_spec,
        ],
        out_shape=[
            jax.ShapeDtypeStruct((m, di), F32),
            jax.ShapeDtypeStruct((m, cch), F32),
            jax.ShapeDtypeStruct((m, LANES), F32),
            us_shape,
        ],
        scratch_shapes=[pltpu.VMEM((tm, d), BF), aux],
        compiler_params=_cparams("arbitrary", "arbitrary"),
        name="ssm_in",
    )(*args)
    if stepwise:
        conv_state = jnp.transpose(us, (2, 0, 1, 3)).reshape(m // seq, keep, cch)
    else:
        conv_state = jnp.swapaxes(us[m // tm - 1], 0, 1).reshape(SUBLANES, cch)[None, SUBLANES - keep:]
    return z, xbc, dt, conv_state


def _gate_norm(y, z, ng):
    yz = y * _silu(z)
    return yz * lax.rsqrt(jnp.mean(yz * yz, axis=-1, keepdims=True) + EPS) * ng


def _ssd_seq_body(xbc_ref, dt_ref, z_ref, alog_ref, dsk_ref, ng_ref, y_ref, hout_ref, ht_ref,
                  *, nheads, groups, hd, ns):
    c = pl.program_id(0)
    cs = dt_ref.shape[0]
    hpg = nheads // groups
    gp = hpg * hd
    di = nheads * hd

    @pl.when(c == 0)
    def _():
        ht_ref[...] = jnp.zeros_like(ht_ref)

    dt = dt_ref[...]
    a_neg = jnp.where(_iota((1, LANES), 1) < nheads, -jnp.exp(alog_ref[...]), 0.0)
    acum = _sel_left(_tri(cs), dt * a_neg)
    acum_t = acum.T
    dt_t = dt.T
    tot = acum[cs - 1:cs, :]
    ex = (_iota((LANES, di), 1) // hd == _iota((LANES, di), 0)).astype(BF)
    coefx = _sel_right(jnp.exp(tot - acum) * dt, ex, parts=2)
    cdx = _sel_right(jnp.broadcast_to(jnp.exp(tot), (SUBLANES, LANES)), ex, parts=2)[0:1, :]
    causal = _iota((cs, cs), 1) <= _iota((cs, cs), 0)

    for g in range(groups):
        xs = xbc_ref[:, g * gp:(g + 1) * gp]
        bm = xbc_ref[:, di + g * ns:di + (g + 1) * ns]
        cm = xbc_ref[:, di + (groups + g) * ns:di + (groups + g + 1) * ns]
        cb = _dot_nt(cm.astype(BF), bm.astype(BF))
        ht = ht_ref[g]
        htb = ht.astype(BF)
        xb = xs.astype(BF)
        ys = []
        for r in range(hpg):
            hh = g * hpg + r
            acol = acum[:, hh:hh + 1]
            seg = jnp.broadcast_to(acol, (cs, cs)) - acum_t[hh:hh + 1, :]
            mp = cb * jnp.exp(jnp.where(causal, seg, NEG_INF)) * dt_t[hh:hh + 1, :]
            csc = cm * jnp.exp(jnp.broadcast_to(acol, (cs, ns)))
            lhs = jnp.concatenate([mp, csc], axis=1).astype(BF)
            rhs = jnp.concatenate([xb[:, r * hd:(r + 1) * hd], htb[:, r * hd:(r + 1) * hd]], axis=0)
            ys.append(_dot(lhs, rhs))
        yg = jnp.concatenate(ys, axis=1) + dsk_ref[:, g * gp:(g + 1) * gp] * xs
        y_ref[:, g * gp:(g + 1) * gp] = _gate_norm(
            yg, z_ref[:, g * gp:(g + 1) * gp], ng_ref[:, g * gp:(g + 1) * gp]).astype(BF)
        wg = (coefx[:, g * gp:(g + 1) * gp] * xs).astype(BF)
        ht_ref[g] = ht * cdx[:, g * gp:(g + 1) * gp] + _dot(bm.T.astype(BF), wg)

    @pl.when(c == pl.num_programs(0) - 1)
    def _():
        hout_ref[...] = ht_ref[...]


def _ssd_seq(xbc, dt, z, alog, dsk, ng, nheads, groups, hd, ns, chunk):
    l, cch = xbc.shape
    di = nheads * hd
    gp = di // groups
    return pl.pallas_call(
        functools.partial(_ssd_seq_body, nheads=nheads, groups=groups, hd=hd, ns=ns),
        grid=(l // chunk,),
        in_specs=[
            pl.BlockSpec((chunk, cch), lambda c: (c, 0)),
            pl.BlockSpec((chunk, LANES), lambda c: (c, 0)),
            pl.BlockSpec((chunk, di), lambda c: (c, 0)),
            pl.BlockSpec((1, LANES), lambda c: (0, 0)),
            pl.BlockSpec((1, di), lambda c: (0, 0)),
            pl.BlockSpec((1, di), lambda c: (0, 0)),
        ],
        out_specs=[pl.BlockSpec((chunk, di), lambda c: (c, 0)),
                   pl.BlockSpec((groups, ns, gp), lambda c: (0, 0, 0))],
        out_shape=[jax.ShapeDtypeStruct((l, di), BF),
                   jax.ShapeDtypeStruct((groups, ns, gp), F32)],
        scratch_shapes=[pltpu.VMEM((groups, ns, gp), F32)],
        compiler_params=_cparams("arbitrary"),
        name="ssd_seq",
    )(xbc, dt, z, alog, dsk, ng)


def _ssd_step_body(xbc_ref, dt_ref, z_ref, alog_ref, alogc_ref, dsk_ref, ng_ref, h0_ref, y_ref, hn_ref,
                   *, nheads, groups, hd, ns):
    t = dt_ref.shape[0]
    hpg = nheads // groups
    gp = hpg * hd
    di = nheads * hd
    gn = groups * ns

    dt = dt_ref[...]
    a_neg = jnp.where(_iota((1, LANES), 1) < nheads, -jnp.exp(alog_ref[...]), 0.0)
    rt = _iota((t, LANES), 0)
    acum = dt * a_neg
    sh = 1
    while sh < t:
        acum = acum + jnp.where(rt >= sh, pltpu.roll(acum, sh, 0), 0.0)
        sh *= 2
    tot = acum[t - 1:t, :]
    coef = jnp.exp(tot - acum) * dt
    eac = jnp.exp(acum)

    xs = xbc_ref[:, 0:di]
    bm = xbc_ref[:, di:di + gn]
    cm = xbc_ref[:, di + gn:di + 2 * gn]

    prods = jnp.concatenate([cm * bm[s:s + 1, :] for s in range(t)], axis=0)
    rsel = ((_iota((gn, LANES), 1) // hpg == _iota((gn, LANES), 0) // ns)
            & (_iota((gn, LANES), 1) < nheads)).astype(BF)
    cbe = _dot(prods.astype(BF), rsel)
    acl = jnp.concatenate([acum] * t, axis=0)
    acs = jnp.concatenate([jnp.broadcast_to(acum[s:s + 1, :], (t, LANES)) for s in range(t)], axis=0)
    dts = jnp.concatenate([jnp.broadcast_to(dt[s:s + 1, :], (t, LANES)) for s in range(t)], axis=0)
    rr = _iota((t * t, LANES), 0)
    ms = cbe * jnp.exp(jnp.where(rr % t >= rr // t, acl - acs, NEG_INF)) * dts
    ex = (_iota((LANES, di), 1) // hd == _iota((LANES, di), 0)).astype(BF)
    big = _sel_right(jnp.concatenate([ms, eac, coef], axis=0), ex, parts=2)
    y = big[0:t, :] * xs[0:1, :]
    for s in range(1, t):
        y = y + big[s * t:(s + 1) * t, :] * xs[s:s + 1, :]
    eacx = big[t * t:t * t + t, :]
    coefx = big[t * t + t:, :]

    dt_t = jnp.concatenate([dt, jnp.zeros((LANES - t, LANES), F32)], axis=0).T
    a_col = -jnp.exp(alogc_ref[...])
    cd_col = jnp.exp(jnp.sum(dt_t * a_col, axis=1, keepdims=True))

    for g in range(groups):
        gs = slice(g * gp, (g + 1) * gp)
        h0 = h0_ref[gs, :]
        yoff = _dot_nt(cm[:, g * ns:(g + 1) * ns].astype(BF), h0.astype(BF))
        yg = y[:, gs] + yoff * eacx[:, gs] + dsk_ref[:, gs] * xs[:, gs]
        y_ref[:, gs] = _gate_norm(yg, z_ref[:, gs], ng_ref[:, gs]).astype(BF)
        wg = (coefx[:, gs] * xs[:, gs]).astype(BF)
        wpad = jnp.concatenate([wg.astype(F32), jnp.zeros((LANES - t, gp), F32)], axis=0)
        wt = jnp.concatenate([wpad[:, k * LANES:(k + 1) * LANES].T for k in range(gp // LANES)], axis=0)
        bpad = jnp.concatenate([bm[:, g * ns:(g + 1) * ns], jnp.zeros((LANES - t, ns), F32)], axis=0)
        upd = _dot(wt.astype(BF), bpad.astype(BF))
        for r in range(hpg):
            hh = g * hpg + r
            rs = slice(r * hd, (r + 1) * hd)
            hn_ref[g * gp + r * hd:g * gp + (r + 1) * hd, :] = (
                h0[rs, :] * jnp.broadcast_to(cd_col[hh:hh + 1, :], (hd, ns)) + upd[rs, :])


def _ssd_step(xbc, dt, z, alog, alogc, dsk, ng, h0, nheads, groups, hd, ns):
    nb = h0.shape[0]
    m, cch = xbc.shape
    t = m // nb
    di = nheads * hd
    return pl.pallas_call(
        functools.partial(_ssd_step_body, nheads=nheads, groups=groups, hd=hd, ns=ns),
        grid=(nb,),
        in_specs=[
            pl.BlockSpec((t, cch), lambda b: (b, 0)),
            pl.BlockSpec((t, LANES), lambda b: (b, 0)),
            pl.BlockSpec((t, di), lambda b: (b, 0)),
            pl.BlockSpec((1, LANES), lambda b: (0, 0)),
            pl.BlockSpec((LANES, 1), lambda b: (0, 0)),
            pl.BlockSpec((1, di), lambda b: (0, 0)),
            pl.BlockSpec((1, di), lambda b: (0, 0)),
            pl.BlockSpec((None, di, ns), lambda b: (b, 0, 0)),
        ],
        out_specs=[pl.BlockSpec((t, di), lambda b: (b, 0)),
                   pl.BlockSpec((None, di, ns), lambda b: (b, 0, 0))],
        out_shape=[jax.ShapeDtypeStruct((m, di), BF),
                   jax.ShapeDtypeStruct((nb, di, ns), F32)],
        compiler_params=_cparams("parallel"),
        name="ssd_step",
    )(xbc, dt, z, alog, alogc, dsk, ng, h0)


def _pad_lanes(a):
    return jnp.pad(a, [(0, 0)] * (a.ndim - 1) + [(0, LANES - a.shape[-1])])


def _trunk(x, nb, page_table, cache_k, cache_v, cache_lf, conf_prev, mconv_prev, ssm_prev, ffn_prev, w):
    _, t, d = x.shape
    m = nb * t
    stepwise = conf_prev is not None
    nh, dh = cache_k.shape[-2:]
    fw = nh * dh
    x = x.reshape(m, d)

    w_in = w["att_w_in"][0]
    cc = w["conf_dw_w"].shape[2]
    cw = w["conf_dw_w"].shape[1]
    w5 = jnp.concatenate([w_in[:, :3 * fw], w_in[:, 3 * fw + nh:]], axis=1).astype(BF)
    wf = _pad_lanes(w_in[:, 3 * fw:3 * fw + nh]).astype(BF)
    bfp = _pad_lanes(w["att_b_f"][0].reshape(1, nh))
    q, k, v, u, lf = _even_in(x, w["ln_mix"][0].reshape(1, d), w5, wf, bfp, fw, cc)
    logf = lf[:, :nh]
    if stepwise:
        npool, page = cache_k.shape[1:3]
        att = _fox_sample(q, k, v, lf,
                          cache_k[0].reshape(npool, page * nh, dh), cache_v[0].reshape(npool, page * nh, dh),
                          cache_lf[0].reshape(npool, page * nh // LANES, LANES), page_table, nh, dh)
        xp = jnp.concatenate([conf_prev[0], u.reshape(nb, t, cc)], axis=1)
        conf_state = xp[:, t:]
        cv = _conv_step(xp, w["conf_dw_w"][0], w["conf_dw_b"][0], act=False).reshape(m, cc)
    else:
        cq, ckt = _fox_cumsum(lf, nh)
        att = _fox_prompt(q, k, v, cq, ckt.reshape(nh, 1, m), nh, dh)
        conf_state = u[m - (cw - 1):].reshape(1, cw - 1, cc)
        cv = _conv_seq(u, w["conf_dw_w"][0], w["conf_dw_b"][0], act=False)
    w_out = w["att_w_out"][0].astype(BF)
    x = _out_even(x, att, cv, w["conf_ln_g"][0], w["conf_ln_b"][0], w_out[:fw], w_out[fw:])

    ffn_states = []

    def ffn(x, i, final_g):
        if stepwise:
            two_f = ffn_prev.shape[-1]
            st = jnp.zeros((nb, t, two_f), F32).at[:, t - 2:].set(ffn_prev[i]).reshape(m, two_f)
        else:
            st = None
        x, us = _ffn(x, w["ln_ffn"][i], w["ffn_w_up"][i].astype(BF), w["ffn_dw_w"][i], w["ffn_dw_b"][i],
                     w["ffn_w_down"][i].astype(BF), st, final_g, t)
        ffn_states.append(us)
        return x

    x = ffn(x, 0, None)

    nheads, hd, ns = w["ssm_state_shape"]
    di = nheads * hd
    cch = w["ssm_conv_w"].shape[2]
    groups = (cch - di) // (2 * ns)
    w_in = w["ssm_w_in"][0]
    wdt = _pad_lanes(w_in[:, di + cch:]).astype(BF)
    dtb = _pad_lanes(w["ssm_dt_bias"][0].reshape(1, nheads))
    sw = w["ssm_conv_w"].shape[1]
    if stepwise:
        st = jnp.zeros((nb, t, cch), F32).at[:, t - (sw - 1):].set(mconv_prev[0]).reshape(m, cch)
    else:
        st = None
    z, xbc, dt, mconv_state = _ssm_in(x, w["ln_mix"][1].reshape(1, d), w_in[:, :di + cch].astype(BF), wdt, dtb,
                                      w["ssm_conv_w"][0], w["ssm_conv_b"][0], st, di, cch, t)
    alog = _pad_lanes(w["ssm_a_log"][0].reshape(1, nheads))
    dsk = jnp.repeat(w["ssm_d"][0], hd).reshape(1, di)
    ng = w["ssm_norm_g"][0].reshape(1, di)
    if stepwise:
        yn, hn = _ssd_step(xbc, dt, z, alog, alog.reshape(LANES, 1), dsk, ng,
                           ssm_prev[0].reshape(nb, di, ns), nheads, groups, hd, ns)
        ssm_state = hn.reshape(nb, nheads, hd, ns)
    else:
        chunk = _tile(m, 128)
        yn, hout = _ssd_seq(xbc, dt, z, alog, dsk, ng, nheads, groups, hd, ns, chunk)
        hpg = nheads // groups
        ssm_state = jnp.transpose(hout.reshape(groups, ns, hpg, hd), (0, 2, 3, 1)).reshape(1, nheads, hd, ns)
    x = _mm_res(x, yn, w["ssm_w_out"][0].astype(BF))
    x = ffn(x, 1, w["ln_final"])

    return (x.reshape(nb, t, d),
            k.reshape(1, nb, t, nh, dh), v.reshape(1, nb, t, nh, dh), logf.reshape(1, nb, t, nh),
            conf_state[None], mconv_state[None], ssm_state[None], jnp.stack(ffn_states))


def kernel(x_prompt, x_sample, cache_k, cache_v, cache_logf, state_conf_conv, state_ssm_conv, state_ssm,
           state_ffn_conv, page_table,
           ln_mix, ln_ffn, ln_final,
           att_w_in, att_b_f, conf_dw_w, conf_dw_b, conf_ln_g, conf_ln_b, att_w_out,
           ssm_w_in, ssm_conv_w, ssm_conv_b, ssm_dt_bias, ssm_a_log, ssm_d, ssm_norm_g, ssm_w_out,
           ffn_w_up, ffn_dw_w, ffn_dw_b, ffn_w_down):
    assert ln_mix.shape[0] == 2 and x_prompt.shape[0] == 1, "two layers, one prompt sequence"
    w = dict(ln_mix=ln_mix, ln_ffn=ln_ffn, ln_final=ln_final,
             att_w_in=att_w_in, att_b_f=att_b_f, conf_dw_w=conf_dw_w, conf_dw_b=conf_dw_b,
             conf_ln_g=conf_ln_g, conf_ln_b=conf_ln_b, att_w_out=att_w_out,
             ssm_w_in=ssm_w_in, ssm_conv_w=ssm_conv_w, ssm_conv_b=ssm_conv_b, ssm_dt_bias=ssm_dt_bias,
             ssm_a_log=ssm_a_log, ssm_d=ssm_d, ssm_norm_g=ssm_norm_g, ssm_w_out=ssm_w_out,
             ffn_w_up=ffn_w_up, ffn_dw_w=ffn_dw_w, ffn_dw_b=ffn_dw_b, ffn_w_down=ffn_w_down,
             ssm_state_shape=state_ssm.shape[2:])
    nb = x_sample.shape[0]
    outs_p = _trunk(x_prompt, 1, None, cache_k, cache_v, cache_logf, None, None, None, None, w)
    outs_s = _trunk(x_sample, nb, page_table, cache_k, cache_v, cache_logf,
                    state_conf_conv, state_ssm_conv, state_ssm, state_ffn_conv, w)
    return (outs_p[0], outs_s[0]) + tuple(outs_p[1:]) + tuple(outs_s[1:])
```

```python
import functools

import jax
import jax.numpy as jnp
from jax import lax
from jax.experimental import pallas as pl
from jax.experimental.pallas import tpu as pltpu

EPS = 1e-6
BF = jnp.bfloat16
F32 = jnp.float32
LANES = 128
SUBLANES = 8
VMEM_LIMIT = 56 * 1024 * 1024
NEG_INF = float("-inf")
PROJ_ROWS = 1024
FFN_ROWS = 512


def _cparams(*sem):
    return pltpu.CompilerParams(dimension_semantics=sem, vmem_limit_bytes=VMEM_LIMIT)


def _tile(n, pref):
    t = pref
    while t >= LANES and t % LANES == 0:
        if n % t == 0:
            return t
        t //= 2
    return n


def _iota(shape, dim):
    return lax.broadcasted_iota(jnp.int32, shape, dim)


def _split3(x):
    hi = x.astype(BF)
    r1 = x - hi.astype(F32)
    mid = r1.astype(BF)
    lo = (r1 - mid.astype(F32)).astype(BF)
    return hi, mid, lo


def _dot(a, b):
    return jnp.dot(a, b, preferred_element_type=F32)


def _dot_nt(a, b):
    return lax.dot_general(a, b, (((1,), (1,)), ((), ())), preferred_element_type=F32)


def _sel_left(sel, x, parts=3):
    ps = _split3(x)[:parts]
    out = _dot(sel, ps[0])
    for p in ps[1:]:
        out = out + _dot(sel, p)
    return out


def _sel_right(x, sel, parts=3):
    ps = _split3(x)[:parts]
    out = _dot(ps[0], sel)
    for p in ps[1:]:
        out = out + _dot(p, sel)
    return out


def _log_sigmoid(x):
    return jnp.minimum(x, 0.0) - jnp.log(1.0 + jnp.exp(-jnp.abs(x)))


def _softplus(x):
    return jnp.maximum(x, 0.0) + jnp.log(1.0 + jnp.exp(-jnp.abs(x)))


def _silu(x):
    return x * jax.nn.sigmoid(x)


def _rms(x, g):
    return x * lax.rsqrt(jnp.mean(x * x, axis=-1, keepdims=True) + EPS) * g


def _col_blocks(w, tn):
    k, n = w.shape
    return jnp.transpose(w.reshape(k, n // tn, tn), (1, 0, 2)).astype(BF)


def _tri(n):
    return (_iota((n, n), 1) <= _iota((n, n), 0)).astype(BF)


def _even_in_body(x_ref, g_ref, w_ref, wf_ref, bf_ref,
                  q_ref, k_ref, v_ref, u_ref, lf_ref, xn_ref, a_ref, *, nf, nc):
    j = pl.program_id(1)

    @pl.when(j == 0)
    def _():
        xn = _rms(x_ref[...], g_ref[...]).astype(BF)
        xn_ref[...] = xn
        lf_ref[...] = _log_sigmoid(_dot(xn, wf_ref[...]) + bf_ref[...])

    z = _dot(xn_ref[...], w_ref[...])

    @pl.when(j < nf)
    def _():
        q_ref[...] = z

    @pl.when((j >= nf) & (j < 2 * nf))
    def _():
        k_ref[...] = z

    @pl.when((j >= 2 * nf) & (j < 3 * nf))
    def _():
        v_ref[...] = z

    @pl.when((j >= 3 * nf) & ((j - 3 * nf) % 2 == 0))
    def _():
        a_ref[...] = z

    @pl.when((j >= 3 * nf) & ((j - 3 * nf) % 2 == 1))
    def _():
        u_ref[...] = a_ref[...] * jax.nn.sigmoid(z)


def _even_in(x, g, w5, wf, bfp, fw, cc):
    m, d = x.shape
    tm = _tile(m, PROJ_ROWS)
    tn = min(_tile(fw, 512), _tile(cc, 512))
    nf, nc = fw // tn, cc // tn
    nj = 3 * nf + 2 * nc

    def w_map(i, j):
        jj = j - 3 * nf
        return (jnp.where(j < 3 * nf, j, 3 * nf + (jj % 2) * nc + jj // 2), 0, 0)

    def seg_map(base, n, div=1):
        return lambda i, j: (i, jnp.clip((j - base) // div, 0, n - 1))

    return pl.pallas_call(
        functools.partial(_even_in_body, nf=nf, nc=nc),
        grid=(m // tm, nj),
        in_specs=[
            pl.BlockSpec((tm, d), lambda i, j: (i, 0)),
            pl.BlockSpec((1, d), lambda i, j: (0, 0)),
            pl.BlockSpec((None, d, tn), w_map),
            pl.BlockSpec((d, LANES), lambda i, j: (0, 0)),
            pl.BlockSpec((1, LANES), lambda i, j: (0, 0)),
        ],
        out_specs=[
            pl.BlockSpec((tm, tn), seg_map(0, nf)),
            pl.BlockSpec((tm, tn), seg_map(nf, nf)),
            pl.BlockSpec((tm, tn), seg_map(2 * nf, nf)),
            pl.BlockSpec((tm, tn), seg_map(3 * nf, nc, 2)),
            pl.BlockSpec((tm, LANES), lambda i, j: (i, 0)),
        ],
        out_shape=[
            jax.ShapeDtypeStruct((m, fw), F32),
            jax.ShapeDtypeStruct((m, fw), F32),
            jax.ShapeDtypeStruct((m, fw), F32),
            jax.ShapeDtypeStruct((m, cc), F32),
            jax.ShapeDtypeStruct((m, LANES), F32),
        ],
        scratch_shapes=[pltpu.VMEM((tm, d), BF), pltpu.VMEM((tm, tn), F32)],
        compiler_params=_cparams("parallel", "arbitrary"),
        name="even_in",
    )(x, g, _col_blocks(w5, tn), wf, bfp)


def _fox_cumsum_body(lf_ref, cq_ref, ckt_ref, carry_ref, *, nh):
    @pl.when(pl.program_id(0) == 0)
    def _():
        carry_ref[...] = jnp.zeros_like(carry_ref)

    lf = lf_ref[...]
    tb = lf.shape[0]
    cs = _sel_left(_tri(tb), lf) + carry_ref[...]
    carry_ref[...] = cs[tb - 1:tb, :]
    for h in range(nh):
        cq_ref[h] = jnp.broadcast_to(cs[:, h:h + 1], (tb, LANES))
    ckt_ref[...] = cs.T[:nh, :]


def _fox_cumsum(lf, nh):
    m = lf.shape[0]
    tb = _tile(m, 256)
    return pl.pallas_call(
        functools.partial(_fox_cumsum_body, nh=nh),
        grid=(m // tb,),
        in_specs=[pl.BlockSpec((tb, LANES), lambda i: (i, 0))],
        out_specs=[pl.BlockSpec((nh, tb, LANES), lambda i: (0, i, 0)),
                   pl.BlockSpec((nh, tb), lambda i: (0, i))],
        out_shape=[jax.ShapeDtypeStruct((nh, m, LANES), F32),
                   jax.ShapeDtypeStruct((nh, m), F32)],
        scratch_shapes=[pltpu.VMEM((1, LANES), F32)],
        compiler_params=_cparams("arbitrary"),
        name="fox_cumsum",
    )(lf)


def _fox_prompt_body(q_ref, k_ref, v_ref, cq_ref, ck_ref, o_ref, kb_ref, vb_ref, m_ref, acc_ref, s_ref, *, scale):
    qi = pl.program_id(1)
    t, dh = q_ref.shape
    log2e = 1.4426950408889634

    @pl.when(qi == 0)
    def _():
        kb_ref[...] = k_ref[...].astype(BF)
        vb_ref[:, 0:dh] = v_ref[...].astype(BF)
        vb_ref[:, dh:] = (_iota((k_ref.shape[0], dh), 1) == 0).astype(BF)

    qs = (q_ref[...] * (scale * log2e)).astype(BF)
    m_ref[...] = jnp.full_like(m_ref, NEG_INF)
    acc_ref[...] = jnp.zeros_like(acc_ref)
    reps = t // LANES
    rc = min(t, 64)

    def scores(kj, slot):
        off = pl.multiple_of(kj * t, t)
        s_ref[slot] = _dot_nt(qs, kb_ref[pl.ds(off, t), :]) - ck_ref[kj] * log2e

    def update(kj, slot, masked):
        off = pl.multiple_of(kj * t, t)
        for r in range(t // rc):
            rows = slice(r * rc, (r + 1) * rc)
            s = s_ref[slot, rows, :]
            if masked:
                s = jnp.where(_iota((rc, t), 1) <= r * rc + _iota((rc, t), 0), s, NEG_INF)
            cq = cq_ref[rows, :] * log2e
            m_prev = m_ref[rows, :]
            m_new = jnp.maximum(m_prev, jnp.max(s, axis=1, keepdims=True) + cq)
            shift = m_new - cq
            p = jnp.exp2(s - jnp.concatenate([shift] * reps, axis=1))
            alpha = jnp.exp2(m_prev - m_new)
            acc_ref[rows, :] = (jnp.concatenate([alpha] * (2 * dh // LANES), axis=1) * acc_ref[rows, :]
                                + _dot(p.astype(BF), vb_ref[pl.ds(off, t), :]))
            m_ref[rows, :] = m_new

    scores(0, 0)

    def body(k2, carry):
        update(2 * k2, 0, False)
        scores(2 * k2 + 1, 1)
        update(2 * k2 + 1, 1, False)
        scores(2 * k2 + 2, 0)
        return carry

    lax.fori_loop(0, qi // 2, body, 0)

    @pl.when(qi % 2 == 0)
    def _():
        update(qi, 0, True)

    @pl.when(qi % 2 == 1)
    def _():
        update(qi - 1, 0, False)
        scores(qi, 1)
        update(qi, 1, True)

    o_ref[...] = acc_ref[:, 0:dh] / acc_ref[:, dh:dh + 1]


def _fox_prompt(q, k, v, cq, ck, nh, dh):
    m = q.shape[0]
    t = _tile(m, 512)
    n = m // t
    return pl.pallas_call(
        functools.partial(_fox_prompt_body, scale=dh ** -0.5),
        grid=(nh, n),
        in_specs=[
            pl.BlockSpec((t, dh), lambda h, i: (i, h)),
            pl.BlockSpec((m, dh), lambda h, i: (0, h)),
            pl.BlockSpec((m, dh), lambda h, i: (0, h)),
            pl.BlockSpec((None, t, LANES), lambda h, i: (h, i, 0)),
            pl.BlockSpec((None, n, 1, t), lambda h, i: (h, 0, 0, 0)),
        ],
        out_specs=pl.BlockSpec((t, dh), lambda h, i: (i, h)),
        out_shape=jax.ShapeDtypeStruct((m, nh * dh), F32),
        scratch_shapes=[pltpu.VMEM((m, dh), BF), pltpu.VMEM((m, 2 * dh), BF),
                        pltpu.VMEM((t, LANES), F32), pltpu.VMEM((t, 2 * dh), F32),
                        pltpu.VMEM((2, t, t), F32)],
        compiler_params=_cparams("parallel", "arbitrary"),
        name="fox_prompt",
    )(q, k, v, cq, ck.reshape(nh, n, 1, t))


def _fox_sample_body(pt_ref, q_ref, kn_ref, vn_ref, lfn_ref, *rest, npg, nh, dh, scale):
    k_refs = rest[:npg]
    v_refs = rest[npg:2 * npg]
    lf_refs = rest[2 * npg:3 * npg]
    o_ref, s_ref, acc_ref = rest[3 * npg:]
    t, fw = q_ref.shape
    page = k_refs[0].shape[0] // nh
    ht = nh * t
    ppr = LANES // nh

    def load_page(ref):
        return jnp.concatenate([ref[pl.ds(h, page, stride=nh), :] for h in range(nh)], axis=1)

    q = q_ref[...]
    qt = jnp.concatenate([q] * nh + [jnp.zeros((LANES - ht, fw), F32)], axis=0)
    qbd = jnp.where(_iota((LANES, fw), 0) // t == _iota((LANES, fw), 1) // dh, qt, 0.0).astype(BF)

    r2 = _iota((LANES, LANES), 0)
    c2 = _iota((LANES, LANES), 1)
    e_past = ((r2 % nh == c2 // t) & (c2 < ht)).astype(BF)
    e_new = ((r2 == c2 // t) & (c2 < ht)).astype(BF)
    tri = _tri(page)
    rp = _iota((page, LANES), 0)
    cp = _iota((page, LANES), 1)

    carry = jnp.zeros((1, LANES), F32)
    for p in range(npg):
        lfp = lf_refs[p][...]
        b = jnp.concatenate([jnp.broadcast_to(lfp[r:r + 1, :], (ppr, LANES))
                             for r in range(page // ppr)], axis=0)
        b = jnp.where(cp // nh == rp % ppr, b, 0.0)
        clocal = _sel_left(tri, _sel_right(b, e_past))
        st = _dot_nt(load_page(k_refs[p]).astype(BF), qbd) * scale
        s_ref[p * page:(p + 1) * page, :] = st - (clocal + carry)
        carry = carry + clocal[page - 1:page, :]

    y = _sel_right(lfn_ref[...], e_new)
    rt = _iota((t, LANES), 0)
    sh = 1
    while sh < t:
        y = y + jnp.where(rt >= sh, pltpu.roll(y, sh, 0), 0.0)
        sh *= 2
    cn = carry + y
    cq = jnp.sum(jnp.where(rt == _iota((t, LANES), 1) % t, cn, 0.0), axis=0, keepdims=True)
    knp = jnp.concatenate([kn_ref[...], jnp.zeros((page - t, fw), F32)], axis=0)
    stn = _dot_nt(knp.astype(BF), qbd) * scale
    cnp = jnp.concatenate([cn, jnp.zeros((page - t, LANES), F32)], axis=0)
    s_ref[npg * page:, :] = jnp.where((rp < t) & (rp <= cp % t), stn - cnp, NEG_INF)

    mx = jnp.max(s_ref[0:page, :], axis=0, keepdims=True)
    for p in range(1, npg + 1):
        mx = jnp.maximum(mx, jnp.max(s_ref[p * page:(p + 1) * page, :], axis=0, keepdims=True))
    shift = cq - (mx + cq)

    acc_ref[...] = jnp.zeros_like(acc_ref)
    lsum = jnp.zeros((LANES, 1), F32)
    for p in range(npg + 1):
        pt = jnp.exp(s_ref[p * page:(p + 1) * page, :] + shift).T
        lsum = lsum + jnp.sum(pt, axis=1, keepdims=True)
        if p < npg:
            vp = load_page(v_refs[p])
        else:
            vp = jnp.concatenate([vn_ref[...], jnp.zeros((page - t, fw), F32)], axis=0)
        acc_ref[...] += _dot(pt[:ht, :].astype(BF), vp.astype(BF))

    for h in range(nh):
        o_ref[:, h * dh:(h + 1) * dh] = (acc_ref[h * t:(h + 1) * t, h * dh:(h + 1) * dh]
                                          / lsum[h * t:(h + 1) * t, :])


def _fox_sample(q, kn, vn, lfn, ck, cv, clf, page_table, nh, dh):
    nb, npg = page_table.shape
    m, fw = q.shape
    t = m // nb
    prows = ck.shape[1]
    page = prows // nh
    rows = page * nh // LANES

    def pg_map(p):
        return lambda b, pt: (pt[b * npg + p], 0, 0)

    seq = pl.BlockSpec((t, fw), lambda b, pt: (b, 0))
    in_specs = [seq, seq, seq, pl.BlockSpec((t, LANES), lambda b, pt: (b, 0))]
    in_specs += [pl.BlockSpec((None, prows, dh), pg_map(p)) for p in range(npg)]
    in_specs += [pl.BlockSpec((None, prows, dh), pg_map(p)) for p in range(npg)]
    in_specs += [pl.BlockSpec((None, rows, LANES), pg_map(p)) for p in range(npg)]
    return pl.pallas_call(
        functools.partial(_fox_sample_body, npg=npg, nh=nh, dh=dh, scale=dh ** -0.5),
        grid_spec=pltpu.PrefetchScalarGridSpec(
            num_scalar_prefetch=1,
            grid=(nb,),
            in_specs=in_specs,
            out_specs=pl.BlockSpec((t, fw), lambda b, pt: (b, 0)),
            scratch_shapes=[pltpu.VMEM(((npg + 1) * page, LANES), F32),
                            pltpu.VMEM((nh * t, fw), F32)],
        ),
        out_shape=jax.ShapeDtypeStruct((m, fw), F32),
        compiler_params=_cparams("parallel"),
        name="fox_sample",
    )(page_table.reshape(-1), q, kn, vn, lfn, *([ck] * npg), *([cv] * npg), *([clf] * npg))


def _conv_seq_body(xm_ref, xh_ref, w_ref, b_ref, o_ref, xf_ref, *, width, act):
    tl = xm_ref.shape[0]
    hb = xh_ref.shape[0]
    xf_ref[0:hb, :] = jnp.where(pl.program_id(0) == 0, 0.0, xh_ref[...])
    xf_ref[hb:, :] = xm_ref[...]
    base = hb - (width - 1)
    acc = b_ref[...] + w_ref[0:1, :] * xf_ref[base:base + tl, :]
    for j in range(1, width):
        acc = acc + w_ref[j:j + 1, :] * xf_ref[base + j:base + j + tl, :]
    o_ref[...] = _silu(acc) if act else acc


def _conv_seq(x, w, b, act):
    l, c = x.shape
    width = w.shape[0]
    hb = -(-(width - 1) // SUBLANES) * SUBLANES
    tl = _tile(l, 256)
    tc = _tile(c, 256)
    r = tl // hb
    return pl.pallas_call(
        functools.partial(_conv_seq_body, width=width, act=act),
        grid=(l // tl, c // tc),
        in_specs=[
            pl.BlockSpec((tl, tc), lambda i, j: (i, j)),
            pl.BlockSpec((hb, tc), lambda i, j: (jnp.maximum(i * r - 1, 0), j)),
            pl.BlockSpec((width, tc), lambda i, j: (0, j)),
            pl.BlockSpec((1, tc), lambda i, j: (0, j)),
        ],
        out_specs=pl.BlockSpec((tl, tc), lambda i, j: (i, j)),
        out_shape=jax.ShapeDtypeStruct((l, c), F32),
        scratch_shapes=[pltpu.VMEM((tl + hb, tc), F32)],
        compiler_params=_cparams("parallel", "parallel"),
        name="conv_seq",
    )(x, x, w, b.reshape(1, c))


def _conv_step_body(xp_ref, w_ref, b_ref, o_ref, *, width, act):
    t = o_ref.shape[1]
    acc = b_ref[...] + w_ref[0:1, :] * xp_ref[:, 0:t, :]
    for j in range(1, width):
        acc = acc + w_ref[j:j + 1, :] * xp_ref[:, j:j + t, :]
    o_ref[...] = _silu(acc) if act else acc


def _conv_step(xp, w, b, act):
    nb, rows, c = xp.shape
    width = w.shape[0]
    t = rows - (width - 1)
    bb = 16 if nb % 16 == 0 else nb
    tc = _tile(c, 512)
    return pl.pallas_call(
        functools.partial(_conv_step_body, width=width, act=act),
        grid=(nb // bb, c // tc),
        in_specs=[
            pl.BlockSpec((bb, rows, tc), lambda i, j: (i, 0, j)),
            pl.BlockSpec((width, tc), lambda i, j: (0, j)),
            pl.BlockSpec((1, tc), lambda i, j: (0, j)),
        ],
        out_specs=pl.BlockSpec((bb, t, tc), lambda i, j: (i, 0, j)),
        out_shape=jax.ShapeDtypeStruct((nb, t, c), F32),
        compiler_params=_cparams("parallel", "parallel"),
        name="conv_step",
    )(xp, w, b.reshape(1, c))


def _out_even_body(x_ref, att_ref, cv_ref, lg_ref, lb_ref, wa_ref, wc_ref, o_ref, ab_ref, cb_ref):
    @pl.when(pl.program_id(1) == 0)
    def _():
        u = cv_ref[...]
        xc = u - jnp.mean(u, axis=-1, keepdims=True)
        var = jnp.mean(xc * xc, axis=-1, keepdims=True)
        cb_ref[...] = _silu(xc * lax.rsqrt(var + EPS) * lg_ref[...] + lb_ref[...]).astype(BF)
        ab_ref[...] = att_ref[...].astype(BF)

    o_ref[...] = x_ref[...] + _dot(ab_ref[...], wa_ref[...]) + _dot(cb_ref[...], wc_ref[...])


def _out_even(x, att, cv, lg, lb, wa, wc):
    m, d = x.shape
    fw, cc = att.shape[1], cv.shape[1]
    tm = _tile(m, PROJ_ROWS)
    tn = _tile(d, 512)
    return pl.pallas_call(
        _out_even_body,
        grid=(m // tm, d // tn),
        in_specs=[
            pl.BlockSpec((tm, tn), lambda i, j: (i, j)),
            pl.BlockSpec((tm, fw), lambda i, j: (i, 0)),
            pl.BlockSpec((tm, cc), lambda i, j: (i, 0)),
            pl.BlockSpec((1, cc), lambda i, j: (0, 0)),
            pl.BlockSpec((1, cc), lambda i, j: (0, 0)),
            pl.BlockSpec((None, fw, tn), lambda i, j: (j, 0, 0)),
            pl.BlockSpec((None, cc, tn), lambda i, j: (j, 0, 0)),
        ],
        out_specs=pl.BlockSpec((tm, tn), lambda i, j: (i, j)),
        out_shape=jax.ShapeDtypeStruct((m, d), F32),
        scratch_shapes=[pltpu.VMEM((tm, fw), BF), pltpu.VMEM((tm, cc), BF)],
        compiler_params=_cparams("parallel", "arbitrary"),
        name="out_even",
    )(x, att, cv, lg.reshape(1, cc), lb.reshape(1, cc), _col_blocks(wa, tn), _col_blocks(wc, tn))


def _mm_res_body(x_ref, a_ref, w_ref, o_ref):
    o_ref[...] = x_ref[...] + _dot(a_ref[...], w_ref[...])


def _mm_res(x, a, w):
    m, d = x.shape
    kk = a.shape[1]
    tm = _tile(m, PROJ_ROWS)
    tn = _tile(d, 512)
    return pl.pallas_call(
        _mm_res_body,
        grid=(m // tm, d // tn),
        in_specs=[
            pl.BlockSpec((tm, tn), lambda i, j: (i, j)),
            pl.BlockSpec((tm, kk), lambda i, j: (i, 0)),
            pl.BlockSpec((None, kk, tn), lambda i, j: (j, 0, 0)),
        ],
        out_specs=pl.BlockSpec((tm, tn), lambda i, j: (i, j)),
        out_shape=jax.ShapeDtypeStruct((m, d), F32),
        compiler_params=_cparams("parallel", "arbitrary"),
        name="mm_res",
    )(x, a, _col_blocks(w, tn))


def _causal_conv(u, w_ref, b_ref, prev, seq):
    tm = u.shape[0]
    width = w_ref.shape[0]
    row = _iota(prev.shape, 0)
    acc = b_ref[...] + w_ref[width - 1:width, :] * u
    for s in range(1, width):
        r = pltpu.roll(u, s, 0)
        if seq is None:
            head = jnp.where(row < s, pltpu.roll(prev, s, 0), r[:SUBLANES])
            r = jnp.concatenate([head, r[SUBLANES:]], axis=0)
        else:
            r = jnp.where(row % seq < s, pltpu.roll(prev, (tm - seq + s) % tm, 0), r)
        acc = acc + w_ref[width - 1 - s:width - s, :] * r
    return acc


def _save_seq_tails(u, us_ref, col, ub_ref, seq, keep):
    tm, c = u.shape
    for k in range(c // LANES):
        ub_ref[...] = u[:, k * LANES:(k + 1) * LANES]
        for r in range(keep):
            us_ref[r, :, col + k * LANES:col + (k + 1) * LANES] = (
                ub_ref[pl.ds(seq - keep + r, tm // seq, stride=seq), :])


def _ffn_body(*refs, stepwise, final, seq, nf):
    x_ref, g_ref, wa_ref, wv_ref, dwa_ref, dwv_ref, ba_ref, bv_ref, wd_ref = refs[:9]
    k = 9
    if stepwise:
        sa_ref, sv_ref = refs[k:k + 2]
        k += 2
    if final:
        fg_ref = refs[k]
        k += 1
    o_ref, us_ref, xn_ref, ua0_ref, uv0_ref, ua1_ref, uv1_ref = refs[k:k + 7]
    k += 7
    slots = ((ua0_ref, uv0_ref), (ua1_ref, uv1_ref))
    i = pl.program_id(0)
    j = pl.program_id(1)
    tm, tf = x_ref.shape[0], wa_ref.shape[1]

    def up(slot):
        xn = xn_ref[...]
        slots[slot][0][...] = _dot(xn, wa_ref[...])
        slots[slot][1][...] = _dot(xn, wv_ref[...])

    def tail(slot):
        jb = j - 1
        ua, uv = slots[slot][0][...], slots[slot][1][...]
        if stepwise:
            pa, pv = sa_ref[...], sv_ref[...]
            _save_seq_tails(ua, us_ref, 0, refs[k], seq, 2)
            _save_seq_tails(uv, us_ref, tf, refs[k], seq, 2)
        else:
            ca_ref, cv_ref = refs[k:k + 2]

            @pl.when(i == 0)
            def _():
                ca_ref[jb] = jnp.zeros((SUBLANES, tf), F32)
                cv_ref[jb] = jnp.zeros((SUBLANES, tf), F32)

            pa, pv = ca_ref[jb], cv_ref[jb]
            ca_ref[jb] = ua[tm - SUBLANES:, :]
            cv_ref[jb] = uv[tm - SUBLANES:, :]
            us_ref[:, 0:tf] = ua[tm - SUBLANES:, :]
            us_ref[:, tf:2 * tf] = uv[tm - SUBLANES:, :]
        cseq = seq if stepwise else None
        a = _causal_conv(ua, dwa_ref, ba_ref, pa, cseq)
        v = _causal_conv(uv, dwv_ref, bv_ref, pv, cseq)
        o_ref[...] += _dot((_silu(a) * v).astype(BF), wd_ref[...])

    @pl.when(j == 0)
    def _():
        xn_ref[...] = _rms(x_ref[...], g_ref[...]).astype(BF)
        o_ref[...] = x_ref[...]
        up(0)

    for parity in (0, 1):
        @pl.when((j > 0) & (j < nf) & (j % 2 == parity))
        def _():
            up(parity)
            tail(1 - parity)

    @pl.when(j == nf)
    def _():
        tail((nf - 1) % 2)
        if final:
            o_ref[...] = _rms(o_ref[...], fg_ref[...])


def _ffn(x, g, w_up, dw_w, dw_b, w_down, state, final_g, seq):
    m, d = x.shape
    f = w_down.shape[0]
    tm = _tile(m, FFN_ROWS)
    tf = _tile(f, 512)
    nf = f // tf
    stepwise = state is not None
    final = final_g is not None
    row_spec = lambda i, j: (i, 0)
    ju = lambda j: jnp.minimum(j, nf - 1)
    jt = lambda j: jnp.maximum(j - 1, 0)
    in_specs = [
        pl.BlockSpec((tm, d), row_spec),
        pl.BlockSpec((1, d), lambda i, j: (0, 0)),
        pl.BlockSpec((None, d, tf), lambda i, j: (ju(j), 0, 0)),
        pl.BlockSpec((None, d, tf), lambda i, j: (ju(j) + nf, 0, 0)),
        pl.BlockSpec((3, tf), lambda i, j: (0, jt(j))),
        pl.BlockSpec((3, tf), lambda i, j: (0, jt(j) + nf)),
        pl.BlockSpec((1, tf), lambda i, j: (0, jt(j))),
        pl.BlockSpec((1, tf), lambda i, j: (0, jt(j) + nf)),
        pl.BlockSpec((tf, d), lambda i, j: (jt(j), 0)),
    ]
    w_up = _col_blocks(w_up, tf)
    w_down = w_down.astype(BF)
    args = [x, g.reshape(1, d), w_up, w_up, dw_w, dw_w, dw_b.reshape(1, 2 * f), dw_b.reshape(1, 2 * f), w_down]
    scratch = [pltpu.VMEM((tm, d), BF)] + [pltpu.VMEM((tm, tf), F32)] * 4
    if stepwise:
        in_specs += [pl.BlockSpec((tm, tf), lambda i, j: (i, jt(j))),
                     pl.BlockSpec((tm, tf), lambda i, j: (i, jt(j) + nf))]
        args += [state, state]
        us_shape = jax.ShapeDtypeStruct((2, nf, m // seq, 2 * tf), F32)
        us_spec = pl.BlockSpec((2, None, tm // seq, 2 * tf), lambda i, j: (0, jt(j), i, 0))
        scratch += [pltpu.VMEM((tm, LANES), F32)]
    else:
        us_shape = jax.ShapeDtypeStruct((m // tm, nf, SUBLANES, 2 * tf), F32)
        us_spec = pl.BlockSpec((None, None, SUBLANES, 2 * tf), lambda i, j: (i, jt(j), 0, 0))
        scratch += [pltpu.VMEM((nf, SUBLANES, tf), F32), pltpu.VMEM((nf, SUBLANES, tf), F32)]
    if final:
        in_specs.append(pl.BlockSpec((1, d), lambda i, j: (0, 0)))
        args.append(final_g.reshape(1, d))
    out, us = pl.pallas_call(
        functools.partial(_ffn_body, stepwise=stepwise, final=final, seq=seq, nf=nf),
        grid=(m // tm, nf + 1),
        in_specs=in_specs,
        out_specs=[pl.BlockSpec((tm, d), row_spec), us_spec],
        out_shape=[jax.ShapeDtypeStruct((m, d), F32), us_shape],
        scratch_shapes=scratch,
        compiler_params=_cparams("arbitrary", "arbitrary"),
        name="conv_ffn",
    )(*args)
    if stepwise:
        halves = [jnp.transpose(h, (2, 0, 1, 3)).reshape(m // seq, 2, f) for h in (us[..., :tf], us[..., tf:])]
    else:
        us = us[m // tm - 1]
        halves = [jnp.swapaxes(h, 0, 1).reshape(SUBLANES, f)[None, SUBLANES - 2:]
                  for h in (us[..., :tf], us[..., tf:])]
    return out, jnp.concatenate(halves, axis=-1)


def _ssm_in_body(*refs, nz, stepwise, seq):
    x_ref, g_ref, w_ref, wdt_ref, dtb_ref, cw_ref, cb_ref = refs[:7]
    k = 7
    if stepwise:
        st_ref = refs[k]
        k += 1
    z_ref, xbc_ref, dt_ref, us_ref, xn_ref, aux_ref = refs[k:k + 6]
    i = pl.program_id(0)
    j = pl.program_id(1)
    tm = x_ref.shape[0]
    keep = cw_ref.shape[0] - 1

    @pl.when(j == 0)
    def _():
        xn = _rms(x_ref[...], g_ref[...]).astype(BF)
        xn_ref[...] = xn
        dt_ref[...] = _softplus(_dot(xn, wdt_ref[...]) + dtb_ref[...])

    z = _dot(xn_ref[...], w_ref[...])

    @pl.when(j < nz)
    def _():
        z_ref[...] = z

    @pl.when(j >= nz)
    def _():
        if stepwise:
            prev = st_ref[...]
            _save_seq_tails(z, us_ref, 0, aux_ref, seq, keep)
        else:
            jx = j - nz

            @pl.when(i == 0)
            def _():
                aux_ref[jx] = jnp.zeros(aux_ref.shape[1:], F32)

            prev = aux_ref[jx]
            aux_ref[jx] = z[tm - SUBLANES:, :]
            us_ref[...] = z[tm - SUBLANES:, :]
        xbc_ref[...] = _silu(_causal_conv(z, cw_ref, cb_ref, prev, seq if stepwise else None))


def _ssm_in(x, g, w, wdt, dtb, cw, cb, state, di, cch, seq):
    m, d = x.shape
    tm = _tile(m, PROJ_ROWS)
    tn = min(_tile(di, 512), _tile(cch, 512))
    nz, nx = di // tn, cch // tn
    width = cw.shape[0]
    keep = width - 1
    stepwise = state is not None
    xcol = lambda i, j: (0, jnp.clip(j - nz, 0, nx - 1))
    in_specs = [
        pl.BlockSpec((tm, d), lambda i, j: (i, 0)),
        pl.BlockSpec((1, d), lambda i, j: (0, 0)),
        pl.BlockSpec((None, d, tn), lambda i, j: (j, 0, 0)),
        pl.BlockSpec((d, LANES), lambda i, j: (0, 0)),
        pl.BlockSpec((1, LANES), lambda i, j: (0, 0)),
        pl.BlockSpec((width, tn), xcol),
        pl.BlockSpec((1, tn), xcol),
    ]
    args = [x, g, _col_blocks(w, tn), wdt, dtb, cw, cb.reshape(1, cch)]
    if stepwise:
        in_specs.append(pl.BlockSpec((tm, tn), lambda i, j: (i, jnp.clip(j - nz, 0, nx - 1))))
        args.append(state)
        us_shape = jax.ShapeDtypeStruct((keep, nx, m // seq, tn), F32)
        us_spec = pl.BlockSpec((keep, None, tm // seq, tn), lambda i, j: (0, jnp.clip(j - nz, 0, nx - 1), i, 0))
        aux = pltpu.VMEM((tm, LANES), F32)
    else:
        us_shape = jax.ShapeDtypeStruct((m // tm, nx, SUBLANES, tn), F32)
        us_spec = pl.BlockSpec((None, None, SUBLANES, tn), lambda i, j: (i, jnp.clip(j - nz, 0, nx - 1), 0, 0))
        aux = pltpu.VMEM((nx, SUBLANES, tn), F32)
    z, xbc, dt, us = pl.pallas_call(
        functools.partial(_ssm_in_body, nz=nz, stepwise=stepwise, seq=seq),
        grid=(m // tm, nz + nx),
        in_specs=in_specs,
        out_specs=[
            pl.BlockSpec((tm, tn), lambda i, j: (i, jnp.clip(j, 0, nz - 1))),
            pl.BlockSpec((tm, tn), lambda i, j: (i, jnp.clip(j - nz, 0, nx - 1))),
            pl.BlockSpec((tm, LANES), lambda i, j: (i, 0)),
            us_spec,
        ],
        out_shape=[
            jax.ShapeDtypeStruct((m, di), F32),
            jax.ShapeDtypeStruct((m, cch), F32),
            jax.ShapeDtypeStruct((m, LANES), F32),
            us_shape,
        ],
        scratch_shapes=[pltpu.VMEM((tm, d), BF), aux],
        compiler_params=_cparams("arbitrary", "arbitrary"),
        name="ssm_in",
    )(*args)
    if stepwise:
        conv_state = jnp.transpose(us, (2, 0, 1, 3)).reshape(m // seq, keep, cch)
    else:
        conv_state = jnp.swapaxes(us[m // tm - 1], 0, 1).reshape(SUBLANES, cch)[None, SUBLANES - keep:]
    return z, xbc, dt, conv_state


def _gate_norm(y, z, ng):
    yz = y * _silu(z)
    return yz * lax.rsqrt(jnp.mean(yz * yz, axis=-1, keepdims=True) + EPS) * ng


def _ssd_seq_body(xbc_ref, dt_ref, z_ref, alog_ref, dsk_ref, ng_ref, y_ref, hout_ref, ht_ref,
                  *, nheads, groups, hd, ns):
    c = pl.program_id(0)
    cs = dt_ref.shape[0]
    hpg = nheads // groups
    gp = hpg * hd
    di = nheads * hd

    @pl.when(c == 0)
    def _():
        ht_ref[...] = jnp.zeros_like(ht_ref)

    dt = dt_ref[...]
    a_neg = jnp.where(_iota((1, LANES), 1) < nheads, -jnp.exp(alog_ref[...]), 0.0)
    acum = _sel_left(_tri(cs), dt * a_neg)
    acum_t = acum.T
    dt_t = dt.T
    tot = acum[cs - 1:cs, :]
    ex = (_iota((LANES, di), 1) // hd == _iota((LANES, di), 0)).astype(BF)
    coefx = _sel_right(jnp.exp(tot - acum) * dt, ex, parts=2)
    cdx = _sel_right(jnp.broadcast_to(jnp.exp(tot), (SUBLANES, LANES)), ex, parts=2)[0:1, :]
    causal = _iota((cs, cs), 1) <= _iota((cs, cs), 0)

    for g in range(groups):
        xs = xbc_ref[:, g * gp:(g + 1) * gp]
        bm = xbc_ref[:, di + g * ns:di + (g + 1) * ns]
        cm = xbc_ref[:, di + (groups + g) * ns:di + (groups + g + 1) * ns]
        cb = _dot_nt(cm.astype(BF), bm.astype(BF))
        ht = ht_ref[g]
        htb = ht.astype(BF)
        xb = xs.astype(BF)
        ys = []
        for r in range(hpg):
            hh = g * hpg + r
            acol = acum[:, hh:hh + 1]
            seg = jnp.broadcast_to(acol, (cs, cs)) - acum_t[hh:hh + 1, :]
            mp = cb * jnp.exp(jnp.where(causal, seg, NEG_INF)) * dt_t[hh:hh + 1, :]
            csc = cm * jnp.exp(jnp.broadcast_to(acol, (cs, ns)))
            lhs = jnp.concatenate([mp, csc], axis=1).astype(BF)
            rhs = jnp.concatenate([xb[:, r * hd:(r + 1) * hd], htb[:, r * hd:(r + 1) * hd]], axis=0)
            ys.append(_dot(lhs, rhs))
        yg = jnp.concatenate(ys, axis=1) + dsk_ref[:, g * gp:(g + 1) * gp] * xs
        y_ref[:, g * gp:(g + 1) * gp] = _gate_norm(
            yg, z_ref[:, g * gp:(g + 1) * gp], ng_ref[:, g * gp:(g + 1) * gp]).astype(BF)
        wg = (coefx[:, g * gp:(g + 1) * gp] * xs).astype(BF)
        ht_ref[g] = ht * cdx[:, g * gp:(g + 1) * gp] + _dot(bm.T.astype(BF), wg)

    @pl.when(c == pl.num_programs(0) - 1)
    def _():
        hout_ref[...] = ht_ref[...]


def _ssd_seq(xbc, dt, z, alog, dsk, ng, nheads, groups, hd, ns, chunk):
    l, cch = xbc.shape
    di = nheads * hd
    gp = di // groups
    return pl.pallas_call(
        functools.partial(_ssd_seq_body, nheads=nheads, groups=groups, hd=hd, ns=ns),
        grid=(l // chunk,),
        in_specs=[
            pl.BlockSpec((chunk, cch), lambda c: (c, 0)),
            pl.BlockSpec((chunk, LANES), lambda c: (c, 0)),
            pl.BlockSpec((chunk, di), lambda c: (c, 0)),
            pl.BlockSpec((1, LANES), lambda c: (0, 0)),
            pl.BlockSpec((1, di), lambda c: (0, 0)),
            pl.BlockSpec((1, di), lambda c: (0, 0)),
        ],
        out_specs=[pl.BlockSpec((chunk, di), lambda c: (c, 0)),
                   pl.BlockSpec((groups, ns, gp), lambda c: (0, 0, 0))],
        out_shape=[jax.ShapeDtypeStruct((l, di), BF),
                   jax.ShapeDtypeStruct((groups, ns, gp), F32)],
        scratch_shapes=[pltpu.VMEM((groups, ns, gp), F32)],
        compiler_params=_cparams("arbitrary"),
        name="ssd_seq",
    )(xbc, dt, z, alog, dsk, ng)


def _ssd_step_body(xbc_ref, dt_ref, z_ref, alog_ref, alogc_ref, dsk_ref, ng_ref, h0_ref, y_ref, hn_ref,
                   *, nheads, groups, hd, ns):
    t = dt_ref.shape[0]
    hpg = nheads // groups
    gp = hpg * hd
    di = nheads * hd
    gn = groups * ns

    dt = dt_ref[...]
    a_neg = jnp.where(_iota((1, LANES), 1) < nheads, -jnp.exp(alog_ref[...]), 0.0)
    rt = _iota((t, LANES), 0)
    acum = dt * a_neg
    sh = 1
    while sh < t:
        acum = acum + jnp.where(rt >= sh, pltpu.roll(acum, sh, 0), 0.0)
        sh *= 2
    tot = acum[t - 1:t, :]
    coef = jnp.exp(tot - acum) * dt
    eac = jnp.exp(acum)

    xs = xbc_ref[:, 0:di]
    bm = xbc_ref[:, di:di + gn]
    cm = xbc_ref[:, di + gn:di + 2 * gn]

    prods = jnp.concatenate([cm * bm[s:s + 1, :] for s in range(t)], axis=0)
    rsel = ((_iota((gn, LANES), 1) // hpg == _iota((gn, LANES), 0) // ns)
            & (_iota((gn, LANES), 1) < nheads)).astype(BF)
    cbe = _dot(prods.astype(BF), rsel)
    acl = jnp.concatenate([acum] * t, axis=0)
    acs = jnp.concatenate([jnp.broadcast_to(acum[s:s + 1, :], (t, LANES)) for s in range(t)], axis=0)
    dts = jnp.concatenate([jnp.broadcast_to(dt[s:s + 1, :], (t, LANES)) for s in range(t)], axis=0)
    rr = _iota((t * t, LANES), 0)
    ms = cbe * jnp.exp(jnp.where(rr % t >= rr // t, acl - acs, NEG_INF)) * dts
    ex = (_iota((LANES, di), 1) // hd == _iota((LANES, di), 0)).astype(BF)
    big = _sel_right(jnp.concatenate([ms, eac, coef], axis=0), ex, parts=2)
    y = big[0:t, :] * xs[0:1, :]
    for s in range(1, t):
        y = y + big[s * t:(s + 1) * t, :] * xs[s:s + 1, :]
    eacx = big[t * t:t * t + t, :]
    coefx = big[t * t + t:, :]

    dt_t = jnp.concatenate([dt, jnp.zeros((LANES - t, LANES), F32)], axis=0).T
    a_col = -jnp.exp(alogc_ref[...])
    cd_col = jnp.exp(jnp.sum(dt_t * a_col, axis=1, keepdims=True))

    for g in range(groups):
        gs = slice(g * gp, (g + 1) * gp)
        h0 = h0_ref[gs, :]
        yoff = _dot_nt(cm[:, g * ns:(g + 1) * ns].astype(BF), h0.astype(BF))
        yg = y[:, gs] + yoff * eacx[:, gs] + dsk_ref[:, gs] * xs[:, gs]
        y_ref[:, gs] = _gate_norm(yg, z_ref[:, gs], ng_ref[:, gs]).astype(BF)
        wg = (coefx[:, gs] * xs[:, gs]).astype(BF)
        wpad = jnp.concatenate([wg.astype(F32), jnp.zeros((LANES - t, gp), F32)], axis=0)
        wt = jnp.concatenate([wpad[:, k * LANES:(k + 1) * LANES].T for k in range(gp // LANES)], axis=0)
        bpad = jnp.concatenate([bm[:, g * ns:(g + 1) * ns], jnp.zeros((LANES - t, ns), F32)], axis=0)
        upd = _dot(wt.astype(BF), bpad.astype(BF))
        for r in range(hpg):
            hh = g * hpg + r
            rs = slice(r * hd, (r + 1) * hd)
            hn_ref[g * gp + r * hd:g * gp + (r + 1) * hd, :] = (
                h0[rs, :] * jnp.broadcast_to(cd_col[hh:hh + 1, :], (hd, ns)) + upd[rs, :])


def _ssd_step(xbc, dt, z, alog, alogc, dsk, ng, h0, nheads, groups, hd, ns):
    nb = h0.shape[0]
    m, cch = xbc.shape
    t = m // nb
    di = nheads * hd
    return pl.pallas_call(
        functools.partial(_ssd_step_body, nheads=nheads, groups=groups, hd=hd, ns=ns),
        grid=(nb,),
        in_specs=[
            pl.BlockSpec((t, cch), lambda b: (b, 0)),
            pl.BlockSpec((t, LANES), lambda b: (b, 0)),
            pl.BlockSpec((t, di), lambda b: (b, 0)),
            pl.BlockSpec((1, LANES), lambda b: (0, 0)),
            pl.BlockSpec((LANES, 1), lambda b: (0, 0)),
            pl.BlockSpec((1, di), lambda b: (0, 0)),
            pl.BlockSpec((1, di), lambda b: (0, 0)),
            pl.BlockSpec((None, di, ns), lambda b: (b, 0, 0)),
        ],
        out_specs=[pl.BlockSpec((t, di), lambda b: (b, 0)),
                   pl.BlockSpec((None, di, ns), lambda b: (b, 0, 0))],
        out_shape=[jax.ShapeDtypeStruct((m, di), BF),
                   jax.ShapeDtypeStruct((nb, di, ns), F32)],
        compiler_params=_cparams("parallel"),
        name="ssd_step",
    )(xbc, dt, z, alog, alogc, dsk, ng, h0)


def _pad_lanes(a):
    return jnp.pad(a, [(0, 0)] * (a.ndim - 1) + [(0, LANES - a.shape[-1])])


def _trunk(x, nb, page_table, cache_k, cache_v, cache_lf, conf_prev, mconv_prev, ssm_prev, ffn_prev, w):
    _, t, d = x.shape
    m = nb * t
    stepwise = conf_prev is not None
    nh, dh = cache_k.shape[-2:]
    fw = nh * dh
    x = x.reshape(m, d)

    w_in = w["att_w_in"][0]
    cc = w["conf_dw_w"].shape[2]
    cw = w["conf_dw_w"].shape[1]
    w5 = jnp.concatenate([w_in[:, :3 * fw], w_in[:, 3 * fw + nh:]], axis=1)
    wf = _pad_lanes(w_in[:, 3 * fw:3 * fw + nh]).astype(BF)
    bfp = _pad_lanes(w["att_b_f"][0].reshape(1, nh))
    q, k, v, u, lf = _even_in(x, w["ln_mix"][0].reshape(1, d), w5, wf, bfp, fw, cc)
    logf = lf[:, :nh]
    if stepwise:
        npool, page = cache_k.shape[1:3]
        att = _fox_sample(q, k, v, lf,
                          cache_k[0].reshape(npool, page * nh, dh), cache_v[0].reshape(npool, page * nh, dh),
                          cache_lf[0].reshape(npool, page * nh // LANES, LANES), page_table, nh, dh)
        xp = jnp.concatenate([conf_prev[0], u.reshape(nb, t, cc)], axis=1)
        conf_state = xp[:, t:]
        cv = _conv_step(xp, w["conf_dw_w"][0], w["conf_dw_b"][0], act=False).reshape(m, cc)
    else:
        cq, ckt = _fox_cumsum(lf, nh)
        att = _fox_prompt(q, k, v, cq, ckt.reshape(nh, 1, m), nh, dh)
        conf_state = u[m - (cw - 1):].reshape(1, cw - 1, cc)
        cv = _conv_seq(u, w["conf_dw_w"][0], w["conf_dw_b"][0], act=False)
    w_out = w["att_w_out"][0]
    x = _out_even(x, att, cv, w["conf_ln_g"][0], w["conf_ln_b"][0], w_out[:fw], w_out[fw:])

    ffn_states = []

    def ffn(x, i, final_g):
        if stepwise:
            two_f = ffn_prev.shape[-1]
            st = jnp.zeros((nb, t, two_f), F32).at[:, t - 2:].set(ffn_prev[i]).reshape(m, two_f)
        else:
            st = None
        x, us = _ffn(x, w["ln_ffn"][i], w["ffn_w_up"][i], w["ffn_dw_w"][i], w["ffn_dw_b"][i],
                     w["ffn_w_down"][i], st, final_g, t)
        ffn_states.append(us)
        return x

    x = ffn(x, 0, None)

    nheads, hd, ns = w["ssm_state_shape"]
    di = nheads * hd
    cch = w["ssm_conv_w"].shape[2]
    groups = (cch - di) // (2 * ns)
    w_in = w["ssm_w_in"][0]
    wdt = _pad_lanes(w_in[:, di + cch:]).astype(BF)
    dtb = _pad_lanes(w["ssm_dt_bias"][0].reshape(1, nheads))
    sw = w["ssm_conv_w"].shape[1]
    if stepwise:
        st = jnp.zeros((nb, t, cch), F32).at[:, t - (sw - 1):].set(mconv_prev[0]).reshape(m, cch)
    else:
        st = None
    z, xbc, dt, mconv_state = _ssm_in(x, w["ln_mix"][1].reshape(1, d), w_in[:, :di + cch], wdt, dtb,
                                      w["ssm_conv_w"][0], w["ssm_conv_b"][0], st, di, cch, t)
    alog = _pad_lanes(w["ssm_a_log"][0].reshape(1, nheads))
    dsk = jnp.repeat(w["ssm_d"][0], hd).reshape(1, di)
    ng = w["ssm_norm_g"][0].reshape(1, di)
    if stepwise:
        yn, hn = _ssd_step(xbc, dt, z, alog, alog.reshape(LANES, 1), dsk, ng,
                           ssm_prev[0].reshape(nb, di, ns), nheads, groups, hd, ns)
        ssm_state = hn.reshape(nb, nheads, hd, ns)
    else:
        chunk = _tile(m, 128)
        yn, hout = _ssd_seq(xbc, dt, z, alog, dsk, ng, nheads, groups, hd, ns, chunk)
        hpg = nheads // groups
        ssm_state = jnp.transpose(hout.reshape(groups, ns, hpg, hd), (0, 2, 3, 1)).reshape(1, nheads, hd, ns)
    x = _mm_res(x, yn, w["ssm_w_out"][0])
    x = ffn(x, 1, w["ln_final"])

    return (x.reshape(nb, t, d),
            k.reshape(1, nb, t, nh, dh), v.reshape(1, nb, t, nh, dh), logf.reshape(1, nb, t, nh),
            conf_state[None], mconv_state[None], ssm_state[None], jnp.stack(ffn_states))


def kernel(x_prompt, x_sample, cache_k, cache_v, cache_logf, state_conf_conv, state_ssm_conv, state_ssm,
           state_ffn_conv, page_table,
           ln_mix, ln_ffn, ln_final,
           att_w_in, att_b_f, conf_dw_w, conf_dw_b, conf_ln_g, conf_ln_b, att_w_out,
           ssm_w_in, ssm_conv_w, ssm_conv_b, ssm_dt_bias, ssm_a_log, ssm_d, ssm_norm_g, ssm_w_out,
           ffn_w_up, ffn_dw_w, ffn_dw_b, ffn_w_down):
    assert ln_mix.shape[0] == 2 and x_prompt.shape[0] == 1, "two layers, one prompt sequence"
    w = dict(ln_mix=ln_mix, ln_ffn=ln_ffn, ln_final=ln_final,
             att_w_in=att_w_in, att_b_f=att_b_f, conf_dw_w=conf_dw_w, conf_dw_b=conf_dw_b,
             conf_ln_g=conf_ln_g, conf_ln_b=conf_ln_b, att_w_out=att_w_out,
             ssm_w_in=ssm_w_in, ssm_conv_w=ssm_conv_w, ssm_conv_b=ssm_conv_b, ssm_dt_bias=ssm_dt_bias,
             ssm_a_log=ssm_a_log, ssm_d=ssm_d, ssm_norm_g=ssm_norm_g, ssm_w_out=ssm_w_out,
             ffn_w_up=ffn_w_up, ffn_dw_w=ffn_dw_w, ffn_dw_b=ffn_dw_b, ffn_w_down=ffn_w_down,
             ssm_state_shape=state_ssm.shape[2:])
    nb = x_sample.shape[0]
    outs_p = _trunk(x_prompt, 1, None, cache_k, cache_v, cache_logf, None, None, None, None, w)
    outs_s = _trunk(x_sample, nb, page_table, cache_k, cache_v, cache_logf,
                    state_conf_conv, state_ssm_conv, state_ssm, state_ffn_conv, w)
    return (outs_p[0], outs_s[0]) + tuple(outs_p[1:]) + tuple(outs_s[1:])
```

```python
import functools

import jax
import jax.numpy as jnp
from jax import lax
from jax.experimental import pallas as pl
from jax.experimental.pallas import tpu as pltpu

EPS = 1e-6
BF = jnp.bfloat16
F32 = jnp.float32
LANES = 128
SUBLANES = 8
VMEM_LIMIT = 56 * 1024 * 1024
NEG_INF = float("-inf")
PROJ_ROWS = 1024
FFN_ROWS = 512


def _cparams(*sem):
    return pltpu.CompilerParams(dimension_semantics=sem, vmem_limit_bytes=VMEM_LIMIT)


def _tile(n, pref):
    t = pref
    while t >= LANES and t % LANES == 0:
        if n % t == 0:
            return t
        t //= 2
    return n


def _iota(shape, dim):
    return lax.broadcasted_iota(jnp.int32, shape, dim)


def _split3(x):
    hi = x.astype(BF)
    r1 = x - hi.astype(F32)
    mid = r1.astype(BF)
    lo = (r1 - mid.astype(F32)).astype(BF)
    return hi, mid, lo


def _dot(a, b):
    return jnp.dot(a, b, preferred_element_type=F32)


def _dot_nt(a, b):
    return lax.dot_general(a, b, (((1,), (1,)), ((), ())), preferred_element_type=F32)


def _sel_left(sel, x, parts=3):
    ps = _split3(x)[:parts]
    out = _dot(sel, ps[0])
    for p in ps[1:]:
        out = out + _dot(sel, p)
    return out


def _sel_right(x, sel, parts=3):
    ps = _split3(x)[:parts]
    out = _dot(ps[0], sel)
    for p in ps[1:]:
        out = out + _dot(p, sel)
    return out


def _log_sigmoid(x):
    return jnp.minimum(x, 0.0) - jnp.log(1.0 + jnp.exp(-jnp.abs(x)))


def _softplus(x):
    return jnp.maximum(x, 0.0) + jnp.log(1.0 + jnp.exp(-jnp.abs(x)))


def _silu(x):
    return x * jax.nn.sigmoid(x)


def _rms(x, g):
    return x * lax.rsqrt(jnp.mean(x * x, axis=-1, keepdims=True) + EPS) * g


def _tri(n):
    return (_iota((n, n), 1) <= _iota((n, n), 0)).astype(BF)


def _even_in_body(x_ref, g_ref, w_ref, wf_ref, bf_ref,
                  q_ref, k_ref, v_ref, u_ref, lf_ref, xn_ref, a_ref, *, nf, nc):
    j = pl.program_id(1)

    @pl.when(j == 0)
    def _():
        xn = _rms(x_ref[...], g_ref[...]).astype(BF)
        xn_ref[...] = xn
        lf_ref[...] = _log_sigmoid(_dot(xn, wf_ref[...]) + bf_ref[...])

    z = _dot(xn_ref[...], w_ref[...])

    @pl.when(j < nf)
    def _():
        q_ref[...] = z

    @pl.when((j >= nf) & (j < 2 * nf))
    def _():
        k_ref[...] = z

    @pl.when((j >= 2 * nf) & (j < 3 * nf))
    def _():
        v_ref[...] = z

    @pl.when((j >= 3 * nf) & ((j - 3 * nf) % 2 == 0))
    def _():
        a_ref[...] = z

    @pl.when((j >= 3 * nf) & ((j - 3 * nf) % 2 == 1))
    def _():
        u_ref[...] = a_ref[...] * jax.nn.sigmoid(z)


def _even_in(x, g, w5, wf, bfp, fw, cc):
    m, d = x.shape
    tm = _tile(m, PROJ_ROWS)
    tn = min(_tile(fw, 512), _tile(cc, 512))
    nf, nc = fw // tn, cc // tn
    nj = 3 * nf + 2 * nc

    def w_map(i, j):
        jj = j - 3 * nf
        return (0, jnp.where(j < 3 * nf, j, 3 * nf + (jj % 2) * nc + jj // 2))

    def seg_map(base, n, div=1):
        return lambda i, j: (i, jnp.clip((j - base) // div, 0, n - 1))

    return pl.pallas_call(
        functools.partial(_even_in_body, nf=nf, nc=nc),
        grid=(m // tm, nj),
        in_specs=[
            pl.BlockSpec((tm, d), lambda i, j: (i, 0)),
            pl.BlockSpec((1, d), lambda i, j: (0, 0)),
            pl.BlockSpec((d, tn), w_map),
            pl.BlockSpec((d, LANES), lambda i, j: (0, 0)),
            pl.BlockSpec((1, LANES), lambda i, j: (0, 0)),
        ],
        out_specs=[
            pl.BlockSpec((tm, tn), seg_map(0, nf)),
            pl.BlockSpec((tm, tn), seg_map(nf, nf)),
            pl.BlockSpec((tm, tn), seg_map(2 * nf, nf)),
            pl.BlockSpec((tm, tn), seg_map(3 * nf, nc, 2)),
            pl.BlockSpec((tm, LANES), lambda i, j: (i, 0)),
        ],
        out_shape=[
            jax.ShapeDtypeStruct((m, fw), F32),
            jax.ShapeDtypeStruct((m, fw), F32),
            jax.ShapeDtypeStruct((m, fw), F32),
            jax.ShapeDtypeStruct((m, cc), F32),
            jax.ShapeDtypeStruct((m, LANES), F32),
        ],
        scratch_shapes=[pltpu.VMEM((tm, d), BF), pltpu.VMEM((tm, tn), F32)],
        compiler_params=_cparams("parallel", "arbitrary"),
        name="even_in",
    )(x, g, w5.astype(BF), wf, bfp)


def _fox_cumsum_body(lf_ref, cq_ref, ckt_ref, carry_ref, *, nh):
    @pl.when(pl.program_id(0) == 0)
    def _():
        carry_ref[...] = jnp.zeros_like(carry_ref)

    lf = lf_ref[...]
    tb = lf.shape[0]
    cs = _sel_left(_tri(tb), lf) + carry_ref[...]
    carry_ref[...] = cs[tb - 1:tb, :]
    for h in range(nh):
        cq_ref[h] = jnp.broadcast_to(cs[:, h:h + 1], (tb, LANES))
    ckt_ref[...] = cs.T[:nh, :]


def _fox_cumsum(lf, nh):
    m = lf.shape[0]
    tb = _tile(m, 256)
    return pl.pallas_call(
        functools.partial(_fox_cumsum_body, nh=nh),
        grid=(m // tb,),
        in_specs=[pl.BlockSpec((tb, LANES), lambda i: (i, 0))],
        out_specs=[pl.BlockSpec((nh, tb, LANES), lambda i: (0, i, 0)),
                   pl.BlockSpec((nh, tb), lambda i: (0, i))],
        out_shape=[jax.ShapeDtypeStruct((nh, m, LANES), F32),
                   jax.ShapeDtypeStruct((nh, m), F32)],
        scratch_shapes=[pltpu.VMEM((1, LANES), F32)],
        compiler_params=_cparams("arbitrary"),
        name="fox_cumsum",
    )(lf)


def _fox_prompt_body(q_ref, k_ref, v_ref, cq_ref, ck_ref, o_ref, kb_ref, vb_ref, m_ref, acc_ref, s_ref, *, scale):
    qi = pl.program_id(1)
    t, dh = q_ref.shape
    log2e = 1.4426950408889634

    @pl.when(qi == 0)
    def _():
        kb_ref[...] = k_ref[...].astype(BF)
        vb_ref[:, 0:dh] = v_ref[...].astype(BF)
        vb_ref[:, dh:] = (_iota((k_ref.shape[0], dh), 1) == 0).astype(BF)

    qs = (q_ref[...] * (scale * log2e)).astype(BF)
    m_ref[...] = jnp.full_like(m_ref, NEG_INF)
    acc_ref[...] = jnp.zeros_like(acc_ref)
    reps = t // LANES
    rc = min(t, 64)

    def scores(kj, slot):
        off = pl.multiple_of(kj * t, t)
        s_ref[slot] = _dot_nt(qs, kb_ref[pl.ds(off, t), :]) - ck_ref[kj] * log2e

    def update(kj, slot, masked):
        off = pl.multiple_of(kj * t, t)
        for r in range(t // rc):
            rows = slice(r * rc, (r + 1) * rc)
            s = s_ref[slot, rows, :]
            if masked:
                s = jnp.where(_iota((rc, t), 1) <= r * rc + _iota((rc, t), 0), s, NEG_INF)
            cq = cq_ref[rows, :] * log2e
            m_prev = m_ref[rows, :]
            m_new = jnp.maximum(m_prev, jnp.max(s, axis=1, keepdims=True) + cq)
            shift = m_new - cq
            p = jnp.exp2(s - jnp.concatenate([shift] * reps, axis=1))
            alpha = jnp.exp2(m_prev - m_new)
            acc_ref[rows, :] = (jnp.concatenate([alpha] * (2 * dh // LANES), axis=1) * acc_ref[rows, :]
                                + _dot(p.astype(BF), vb_ref[pl.ds(off, t), :]))
            m_ref[rows, :] = m_new

    scores(0, 0)

    def body(k2, carry):
        update(2 * k2, 0, False)
        scores(2 * k2 + 1, 1)
        update(2 * k2 + 1, 1, False)
        scores(2 * k2 + 2, 0)
        return carry

    lax.fori_loop(0, qi // 2, body, 0)

    @pl.when(qi % 2 == 0)
    def _():
        update(qi, 0, True)

    @pl.when(qi % 2 == 1)
    def _():
        update(qi - 1, 0, False)
        scores(qi, 1)
        update(qi, 1, True)

    o_ref[...] = acc_ref[:, 0:dh] / acc_ref[:, dh:dh + 1]


def _fox_prompt(q, k, v, cq, ck, nh, dh):
    m = q.shape[0]
    t = _tile(m, 512)
    n = m // t
    return pl.pallas_call(
        functools.partial(_fox_prompt_body, scale=dh ** -0.5),
        grid=(nh, n),
        in_specs=[
            pl.BlockSpec((t, dh), lambda h, i: (i, h)),
            pl.BlockSpec((m, dh), lambda h, i: (0, h)),
            pl.BlockSpec((m, dh), lambda h, i: (0, h)),
            pl.BlockSpec((None, t, LANES), lambda h, i: (h, i, 0)),
            pl.BlockSpec((None, n, 1, t), lambda h, i: (h, 0, 0, 0)),
        ],
        out_specs=pl.BlockSpec((t, dh), lambda h, i: (i, h)),
        out_shape=jax.ShapeDtypeStruct((m, nh * dh), F32),
        scratch_shapes=[pltpu.VMEM((m, dh), BF), pltpu.VMEM((m, 2 * dh), BF),
                        pltpu.VMEM((t, LANES), F32), pltpu.VMEM((t, 2 * dh), F32),
                        pltpu.VMEM((2, t, t), F32)],
        compiler_params=_cparams("parallel", "arbitrary"),
        name="fox_prompt",
    )(q, k, v, cq, ck.reshape(nh, n, 1, t))


def _fox_sample_body(pt_ref, q_ref, kn_ref, vn_ref, lfn_ref, *rest, npg, nh, dh, scale):
    k_refs = rest[:npg]
    v_refs = rest[npg:2 * npg]
    lf_refs = rest[2 * npg:3 * npg]
    o_ref, s_ref, acc_ref = rest[3 * npg:]
    t, fw = q_ref.shape
    page = k_refs[0].shape[0] // nh
    ht = nh * t
    ppr = LANES // nh

    def load_page(ref):
        return jnp.concatenate([ref[pl.ds(h, page, stride=nh), :] for h in range(nh)], axis=1)

    q = q_ref[...]
    qt = jnp.concatenate([q] * nh + [jnp.zeros((LANES - ht, fw), F32)], axis=0)
    qbd = jnp.where(_iota((LANES, fw), 0) // t == _iota((LANES, fw), 1) // dh, qt, 0.0).astype(BF)

    r2 = _iota((LANES, LANES), 0)
    c2 = _iota((LANES, LANES), 1)
    e_past = ((r2 % nh == c2 // t) & (c2 < ht)).astype(BF)
    e_new = ((r2 == c2 // t) & (c2 < ht)).astype(BF)
    tri = _tri(page)
    rp = _iota((page, LANES), 0)
    cp = _iota((page, LANES), 1)

    carry = jnp.zeros((1, LANES), F32)
    for p in range(npg):
        lfp = lf_refs[p][...]
        b = jnp.concatenate([jnp.broadcast_to(lfp[r:r + 1, :], (ppr, LANES))
                             for r in range(page // ppr)], axis=0)
        b = jnp.where(cp // nh == rp % ppr, b, 0.0)
        clocal = _sel_left(tri, _sel_right(b, e_past))
        st = _dot_nt(load_page(k_refs[p]).astype(BF), qbd) * scale
        s_ref[p * page:(p + 1) * page, :] = st - (clocal + carry)
        carry = carry + clocal[page - 1:page, :]

    y = _sel_right(lfn_ref[...], e_new)
    rt = _iota((t, LANES), 0)
    sh = 1
    while sh < t:
        y = y + jnp.where(rt >= sh, pltpu.roll(y, sh, 0), 0.0)
        sh *= 2
    cn = carry + y
    cq = jnp.sum(jnp.where(rt == _iota((t, LANES), 1) % t, cn, 0.0), axis=0, keepdims=True)
    knp = jnp.concatenate([kn_ref[...], jnp.zeros((page - t, fw), F32)], axis=0)
    stn = _dot_nt(knp.astype(BF), qbd) * scale
    cnp = jnp.concatenate([cn, jnp.zeros((page - t, LANES), F32)], axis=0)
    s_ref[npg * page:, :] = jnp.where((rp < t) & (rp <= cp % t), stn - cnp, NEG_INF)

    mx = jnp.max(s_ref[0:page, :], axis=0, keepdims=True)
    for p in range(1, npg + 1):
        mx = jnp.maximum(mx, jnp.max(s_ref[p * page:(p + 1) * page, :], axis=0, keepdims=True))
    shift = cq - (mx + cq)

    acc_ref[...] = jnp.zeros_like(acc_ref)
    lsum = jnp.zeros((LANES, 1), F32)
    for p in range(npg + 1):
        pt = jnp.exp(s_ref[p * page:(p + 1) * page, :] + shift).T
        lsum = lsum + jnp.sum(pt, axis=1, keepdims=True)
        if p < npg:
            vp = load_page(v_refs[p])
        else:
            vp = jnp.concatenate([vn_ref[...], jnp.zeros((page - t, fw), F32)], axis=0)
        acc_ref[...] += _dot(pt[:ht, :].astype(BF), vp.astype(BF))

    for h in range(nh):
        o_ref[:, h * dh:(h + 1) * dh] = (acc_ref[h * t:(h + 1) * t, h * dh:(h + 1) * dh]
                                          / lsum[h * t:(h + 1) * t, :])


def _fox_sample(q, kn, vn, lfn, ck, cv, clf, page_table, nh, dh):
    nb, npg = page_table.shape
    m, fw = q.shape
    t = m // nb
    prows = ck.shape[1]
    page = prows // nh
    rows = page * nh // LANES

    def pg_map(p):
        return lambda b, pt: (pt[b * npg + p], 0, 0)

    seq = pl.BlockSpec((t, fw), lambda b, pt: (b, 0))
    in_specs = [seq, seq, seq, pl.BlockSpec((t, LANES), lambda b, pt: (b, 0))]
    in_specs += [pl.BlockSpec((None, prows, dh), pg_map(p)) for p in range(npg)]
    in_specs += [pl.BlockSpec((None, prows, dh), pg_map(p)) for p in range(npg)]
    in_specs += [pl.BlockSpec((None, rows, LANES), pg_map(p)) for p in range(npg)]
    return pl.pallas_call(
        functools.partial(_fox_sample_body, npg=npg, nh=nh, dh=dh, scale=dh ** -0.5),
        grid_spec=pltpu.PrefetchScalarGridSpec(
            num_scalar_prefetch=1,
            grid=(nb,),
            in_specs=in_specs,
            out_specs=pl.BlockSpec((t, fw), lambda b, pt: (b, 0)),
            scratch_shapes=[pltpu.VMEM(((npg + 1) * page, LANES), F32),
                            pltpu.VMEM((nh * t, fw), F32)],
        ),
        out_shape=jax.ShapeDtypeStruct((m, fw), F32),
        compiler_params=_cparams("parallel"),
        name="fox_sample",
    )(page_table.reshape(-1), q, kn, vn, lfn, *([ck] * npg), *([cv] * npg), *([clf] * npg))


def _conv_seq_body(xm_ref, xh_ref, w_ref, b_ref, o_ref, xf_ref, *, width, act):
    tl = xm_ref.shape[0]
    hb = xh_ref.shape[0]
    xf_ref[0:hb, :] = jnp.where(pl.program_id(0) == 0, 0.0, xh_ref[...])
    xf_ref[hb:, :] = xm_ref[...]
    base = hb - (width - 1)
    acc = b_ref[...] + w_ref[0:1, :] * xf_ref[base:base + tl, :]
    for j in range(1, width):
        acc = acc + w_ref[j:j + 1, :] * xf_ref[base + j:base + j + tl, :]
    o_ref[...] = _silu(acc) if act else acc


def _conv_seq(x, w, b, act):
    l, c = x.shape
    width = w.shape[0]
    hb = -(-(width - 1) // SUBLANES) * SUBLANES
    tl = _tile(l, 256)
    tc = _tile(c, 256)
    r = tl // hb
    return pl.pallas_call(
        functools.partial(_conv_seq_body, width=width, act=act),
        grid=(l // tl, c // tc),
        in_specs=[
            pl.BlockSpec((tl, tc), lambda i, j: (i, j)),
            pl.BlockSpec((hb, tc), lambda i, j: (jnp.maximum(i * r - 1, 0), j)),
            pl.BlockSpec((width, tc), lambda i, j: (0, j)),
            pl.BlockSpec((1, tc), lambda i, j: (0, j)),
        ],
        out_specs=pl.BlockSpec((tl, tc), lambda i, j: (i, j)),
        out_shape=jax.ShapeDtypeStruct((l, c), F32),
        scratch_shapes=[pltpu.VMEM((tl + hb, tc), F32)],
        compiler_params=_cparams("parallel", "parallel"),
        name="conv_seq",
    )(x, x, w, b.reshape(1, c))


def _conv_step_body(xp_ref, w_ref, b_ref, o_ref, *, width, act):
    t = o_ref.shape[1]
    acc = b_ref[...] + w_ref[0:1, :] * xp_ref[:, 0:t, :]
    for j in range(1, width):
        acc = acc + w_ref[j:j + 1, :] * xp_ref[:, j:j + t, :]
    o_ref[...] = _silu(acc) if act else acc


def _conv_step(xp, w, b, act):
    nb, rows, c = xp.shape
    width = w.shape[0]
    t = rows - (width - 1)
    bb = 16 if nb % 16 == 0 else nb
    tc = _tile(c, 512)
    return pl.pallas_call(
        functools.partial(_conv_step_body, width=width, act=act),
        grid=(nb // bb, c // tc),
        in_specs=[
            pl.BlockSpec((bb, rows, tc), lambda i, j: (i, 0, j)),
            pl.BlockSpec((width, tc), lambda i, j: (0, j)),
            pl.BlockSpec((1, tc), lambda i, j: (0, j)),
        ],
        out_specs=pl.BlockSpec((bb, t, tc), lambda i, j: (i, 0, j)),
        out_shape=jax.ShapeDtypeStruct((nb, t, c), F32),
        compiler_params=_cparams("parallel", "parallel"),
        name="conv_step",
    )(xp, w, b.reshape(1, c))


def _out_even_body(x_ref, att_ref, cv_ref, lg_ref, lb_ref, wa_ref, wc_ref, o_ref, ab_ref, cb_ref):
    @pl.when(pl.program_id(1) == 0)
    def _():
        u = cv_ref[...]
        xc = u - jnp.mean(u, axis=-1, keepdims=True)
        var = jnp.mean(xc * xc, axis=-1, keepdims=True)
        cb_ref[...] = _silu(xc * lax.rsqrt(var + EPS) * lg_ref[...] + lb_ref[...]).astype(BF)
        ab_ref[...] = att_ref[...].astype(BF)

    o_ref[...] = x_ref[...] + _dot(ab_ref[...], wa_ref[...]) + _dot(cb_ref[...], wc_ref[...])


def _out_even(x, att, cv, lg, lb, wa, wc):
    m, d = x.shape
    fw, cc = att.shape[1], cv.shape[1]
    tm = _tile(m, PROJ_ROWS)
    tn = _tile(d, 512)
    return pl.pallas_call(
        _out_even_body,
        grid=(m // tm, d // tn),
        in_specs=[
            pl.BlockSpec((tm, tn), lambda i, j: (i, j)),
            pl.BlockSpec((tm, fw), lambda i, j: (i, 0)),
            pl.BlockSpec((tm, cc), lambda i, j: (i, 0)),
            pl.BlockSpec((1, cc), lambda i, j: (0, 0)),
            pl.BlockSpec((1, cc), lambda i, j: (0, 0)),
            pl.BlockSpec((fw, tn), lambda i, j: (0, j)),
            pl.BlockSpec((cc, tn), lambda i, j: (0, j)),
        ],
        out_specs=pl.BlockSpec((tm, tn), lambda i, j: (i, j)),
        out_shape=jax.ShapeDtypeStruct((m, d), F32),
        scratch_shapes=[pltpu.VMEM((tm, fw), BF), pltpu.VMEM((tm, cc), BF)],
        compiler_params=_cparams("parallel", "arbitrary"),
        name="out_even",
    )(x, att, cv, lg.reshape(1, cc), lb.reshape(1, cc), wa.astype(BF), wc.astype(BF))


def _mm_res_body(x_ref, a_ref, w_ref, o_ref):
    o_ref[...] = x_ref[...] + _dot(a_ref[...], w_ref[...])


def _mm_res(x, a, w):
    m, d = x.shape
    kk = a.shape[1]
    tm = _tile(m, PROJ_ROWS)
    tn = _tile(d, 512)
    return pl.pallas_call(
        _mm_res_body,
        grid=(m // tm, d // tn),
        in_specs=[
            pl.BlockSpec((tm, tn), lambda i, j: (i, j)),
            pl.BlockSpec((tm, kk), lambda i, j: (i, 0)),
            pl.BlockSpec((kk, tn), lambda i, j: (0, j)),
        ],
        out_specs=pl.BlockSpec((tm, tn), lambda i, j: (i, j)),
        out_shape=jax.ShapeDtypeStruct((m, d), F32),
        compiler_params=_cparams("parallel", "arbitrary"),
        name="mm_res",
    )(x, a, w.astype(BF))


def _causal_conv(u, w_ref, b_ref, prev, seq):
    tm = u.shape[0]
    width = w_ref.shape[0]
    row = _iota(prev.shape, 0)
    acc = b_ref[...] + w_ref[width - 1:width, :] * u
    for s in range(1, width):
        r = pltpu.roll(u, s, 0)
        if seq is None:
            head = jnp.where(row < s, pltpu.roll(prev, s, 0), r[:SUBLANES])
            r = jnp.concatenate([head, r[SUBLANES:]], axis=0)
        else:
            r = jnp.where(row % seq < s, pltpu.roll(prev, (tm - seq + s) % tm, 0), r)
        acc = acc + w_ref[width - 1 - s:width - s, :] * r
    return acc


def _save_seq_tails(u, us_ref, col, ub_ref, seq, keep):
    tm, c = u.shape
    for k in range(c // LANES):
        ub_ref[...] = u[:, k * LANES:(k + 1) * LANES]
        for r in range(keep):
            us_ref[r, :, col + k * LANES:col + (k + 1) * LANES] = (
                ub_ref[pl.ds(seq - keep + r, tm // seq, stride=seq), :])


def _ffn_body(*refs, stepwise, final, seq, nf):
    x_ref, g_ref, wa_ref, wv_ref, dwa_ref, dwv_ref, ba_ref, bv_ref, wd_ref = refs[:9]
    k = 9
    if stepwise:
        sa_ref, sv_ref = refs[k:k + 2]
        k += 2
    if final:
        fg_ref = refs[k]
        k += 1
    o_ref, us_ref, xn_ref = refs[k:k + 3]
    k += 3
    i = pl.program_id(0)
    j = pl.program_id(1)
    tm, tf = x_ref.shape[0], wa_ref.shape[1]
    width = dwa_ref.shape[0]

    @pl.when(j == 0)
    def _():
        xn_ref[...] = _rms(x_ref[...], g_ref[...]).astype(BF)
        o_ref[...] = x_ref[...]

    xn = xn_ref[...]
    if stepwise:
        ua = _dot(xn, wa_ref[...])
        uv = _dot(xn, wv_ref[...])
        _save_seq_tails(ua, us_ref, 0, refs[k], seq, width - 1)
        _save_seq_tails(uv, us_ref, tf, refs[k], seq, width - 1)
        a = _causal_conv(ua, dwa_ref, ba_ref, sa_ref[...], seq)
        v = _causal_conv(uv, dwv_ref, bv_ref, sv_ref[...], seq)
        o_ref[...] += _dot((_silu(a) * v).astype(BF), wd_ref[...])
    else:
        ua_ref, uv_ref, h_ref, ca_ref, cv_ref = refs[k:k + 5]

        @pl.when(i == 0)
        def _():
            ca_ref[j] = jnp.zeros((SUBLANES, tf), F32)
            cv_ref[j] = jnp.zeros((SUBLANES, tf), F32)

        ua_ref[0:SUBLANES, :] = ca_ref[j]
        uv_ref[0:SUBLANES, :] = cv_ref[j]
        ua_ref[SUBLANES:, :] = _dot(xn, wa_ref[...])
        uv_ref[SUBLANES:, :] = _dot(xn, wv_ref[...])
        ca_ref[j] = ua_ref[tm:, :]
        cv_ref[j] = uv_ref[tm:, :]
        us_ref[:, 0:tf] = ua_ref[tm:, :]
        us_ref[:, tf:2 * tf] = uv_ref[tm:, :]
        rc = min(tm, 16)

        def conv_rows(u_ref, w_ref, b_ref, r0):
            win = u_ref[r0:r0 + SUBLANES + rc, :]
            acc = b_ref[...] + w_ref[width - 1:width, :] * win[SUBLANES:, :]
            for s in range(1, width):
                acc = acc + w_ref[width - 1 - s:width - s, :] * pltpu.roll(win, s, 0)[SUBLANES:, :]
            return acc

        for c in range(tm // rc):
            a = conv_rows(ua_ref, dwa_ref, ba_ref, c * rc)
            v = conv_rows(uv_ref, dwv_ref, bv_ref, c * rc)
            h_ref[c * rc:(c + 1) * rc, :] = (_silu(a) * v).astype(BF)
        o_ref[...] += _dot(h_ref[...], wd_ref[...])

    if final:
        @pl.when(j == nf - 1)
        def _():
            o_ref[...] = _rms(o_ref[...], fg_ref[...])


def _ffn(x, g, w_up, dw_w, dw_b, w_down, state, final_g, seq):
    m, d = x.shape
    f = w_down.shape[0]
    tm = _tile(m, FFN_ROWS)
    tf = _tile(f, 512)
    nf = f // tf
    stepwise = state is not None
    final = final_g is not None
    row_spec = lambda i, j: (i, 0)
    a_col = lambda i, j: (0, j)
    v_col = lambda i, j: (0, j + nf)
    in_specs = [
        pl.BlockSpec((tm, d), row_spec),
        pl.BlockSpec((1, d), lambda i, j: (0, 0)),
        pl.BlockSpec((d, tf), a_col),
        pl.BlockSpec((d, tf), v_col),
        pl.BlockSpec((3, tf), a_col),
        pl.BlockSpec((3, tf), v_col),
        pl.BlockSpec((1, tf), a_col),
        pl.BlockSpec((1, tf), v_col),
        pl.BlockSpec((tf, d), lambda i, j: (j, 0)),
    ]
    w_up = w_up.astype(BF)
    w_down = w_down.astype(BF)
    args = [x, g.reshape(1, d), w_up, w_up, dw_w, dw_w, dw_b.reshape(1, 2 * f), dw_b.reshape(1, 2 * f), w_down]
    scratch = [pltpu.VMEM((tm, d), BF)]
    if stepwise:
        in_specs += [pl.BlockSpec((tm, tf), lambda i, j: (i, j)),
                     pl.BlockSpec((tm, tf), lambda i, j: (i, j + nf))]
        args += [state, state]
        us_shape = jax.ShapeDtypeStruct((2, nf, m // seq, 2 * tf), F32)
        us_spec = pl.BlockSpec((2, None, tm // seq, 2 * tf), lambda i, j: (0, j, i, 0))
        scratch += [pltpu.VMEM((tm, LANES), F32)]
    else:
        us_shape = jax.ShapeDtypeStruct((m // tm, nf, SUBLANES, 2 * tf), F32)
        us_spec = pl.BlockSpec((None, None, SUBLANES, 2 * tf), lambda i, j: (i, j, 0, 0))
        scratch += [pltpu.VMEM((SUBLANES + tm, tf), F32), pltpu.VMEM((SUBLANES + tm, tf), F32),
                    pltpu.VMEM((tm, tf), BF),
                    pltpu.VMEM((nf, SUBLANES, tf), F32), pltpu.VMEM((nf, SUBLANES, tf), F32)]
    if final:
        in_specs.append(pl.BlockSpec((1, d), lambda i, j: (0, 0)))
        args.append(final_g.reshape(1, d))
    out, us = pl.pallas_call(
        functools.partial(_ffn_body, stepwise=stepwise, final=final, seq=seq, nf=nf),
        grid=(m // tm, nf),
        in_specs=in_specs,
        out_specs=[pl.BlockSpec((tm, d), row_spec), us_spec],
        out_shape=[jax.ShapeDtypeStruct((m, d), F32), us_shape],
        scratch_shapes=scratch,
        compiler_params=_cparams("arbitrary", "arbitrary"),
        name="conv_ffn",
    )(*args)
    if stepwise:
        halves = [jnp.transpose(h, (2, 0, 1, 3)).reshape(m // seq, 2, f) for h in (us[..., :tf], us[..., tf:])]
    else:
        us = us[m // tm - 1]
        halves = [jnp.swapaxes(h, 0, 1).reshape(SUBLANES, f)[None, SUBLANES - 2:]
                  for h in (us[..., :tf], us[..., tf:])]
    return out, jnp.concatenate(halves, axis=-1)


def _ssm_in_body(*refs, nz, stepwise, seq):
    x_ref, g_ref, w_ref, wdt_ref, dtb_ref, cw_ref, cb_ref = refs[:7]
    k = 7
    if stepwise:
        st_ref = refs[k]
        k += 1
    z_ref, xbc_ref, dt_ref, us_ref, xn_ref, aux_ref = refs[k:k + 6]
    i = pl.program_id(0)
    j = pl.program_id(1)
    tm = x_ref.shape[0]
    keep = cw_ref.shape[0] - 1

    @pl.when(j == 0)
    def _():
        xn = _rms(x_ref[...], g_ref[...]).astype(BF)
        xn_ref[...] = xn
        dt_ref[...] = _softplus(_dot(xn, wdt_ref[...]) + dtb_ref[...])

    z = _dot(xn_ref[...], w_ref[...])

    @pl.when(j < nz)
    def _():
        z_ref[...] = z

    @pl.when(j >= nz)
    def _():
        if stepwise:
            prev = st_ref[...]
            _save_seq_tails(z, us_ref, 0, aux_ref, seq, keep)
        else:
            jx = j - nz

            @pl.when(i == 0)
            def _():
                aux_ref[jx] = jnp.zeros(aux_ref.shape[1:], F32)

            prev = aux_ref[jx]
            aux_ref[jx] = z[tm - SUBLANES:, :]
            us_ref[...] = z[tm - SUBLANES:, :]
        xbc_ref[...] = _silu(_causal_conv(z, cw_ref, cb_ref, prev, seq if stepwise else None))


def _ssm_in(x, g, w, wdt, dtb, cw, cb, state, di, cch, seq):
    m, d = x.shape
    tm = _tile(m, PROJ_ROWS)
    tn = min(_tile(di, 512), _tile(cch, 512))
    nz, nx = di // tn, cch // tn
    width = cw.shape[0]
    keep = width - 1
    stepwise = state is not None
    xcol = lambda i, j: (0, jnp.clip(j - nz, 0, nx - 1))
    in_specs = [
        pl.BlockSpec((tm, d), lambda i, j: (i, 0)),
        pl.BlockSpec((1, d), lambda i, j: (0, 0)),
        pl.BlockSpec((d, tn), lambda i, j: (0, j)),
        pl.BlockSpec((d, LANES), lambda i, j: (0, 0)),
        pl.BlockSpec((1, LANES), lambda i, j: (0, 0)),
        pl.BlockSpec((width, tn), xcol),
        pl.BlockSpec((1, tn), xcol),
    ]
    args = [x, g, w.astype(BF), wdt, dtb, cw, cb.reshape(1, cch)]
    if stepwise:
        in_specs.append(pl.BlockSpec((tm, tn), lambda i, j: (i, jnp.clip(j - nz, 0, nx - 1))))
        args.append(state)
        us_shape = jax.ShapeDtypeStruct((keep, nx, m // seq, tn), F32)
        us_spec = pl.BlockSpec((keep, None, tm // seq, tn), lambda i, j: (0, jnp.clip(j - nz, 0, nx - 1), i, 0))
        aux = pltpu.VMEM((tm, LANES), F32)
    else:
        us_shape = jax.ShapeDtypeStruct((m // tm, nx, SUBLANES, tn), F32)
        us_spec = pl.BlockSpec((None, None, SUBLANES, tn), lambda i, j: (i, jnp.clip(j - nz, 0, nx - 1), 0, 0))
        aux = pltpu.VMEM((nx, SUBLANES, tn), F32)
    z, xbc, dt, us = pl.pallas_call(
        functools.partial(_ssm_in_body, nz=nz, stepwise=stepwise, seq=seq),
        grid=(m // tm, nz + nx),
        in_specs=in_specs,
        out_specs=[
            pl.BlockSpec((tm, tn), lambda i, j: (i, jnp.clip(j, 0, nz - 1))),
            pl.BlockSpec((tm, tn), lambda i, j: (i, jnp.clip(j - nz, 0, nx - 1))),
            pl.BlockSpec((tm, LANES), lambda i, j: (i, 0)),
            us_spec,
        ],
        out_shape=[
            jax.ShapeDtypeStruct((m, di), F32),
            jax.ShapeDtypeStruct((m, cch), F32),
            jax.ShapeDtypeStruct((m, LANES), F32),
            us_shape,
        ],
        scratch_shapes=[pltpu.VMEM((tm, d), BF), aux],
        compiler_params=_cparams("arbitrary", "arbitrary"),
        name="ssm_in",
    )(*args)
    if stepwise:
        conv_state = jnp.transpose(us, (2, 0, 1, 3)).reshape(m // seq, keep, cch)
    else:
        conv_state = jnp.swapaxes(us[m // tm - 1], 0, 1).reshape(SUBLANES, cch)[None, SUBLANES - keep:]
    return z, xbc, dt, conv_state


def _gate_norm(y, z, ng):
    yz = y * _silu(z)
    return yz * lax.rsqrt(jnp.mean(yz * yz, axis=-1, keepdims=True) + EPS) * ng


def _ssd_seq_body(xbc_ref, dt_ref, z_ref, alog_ref, dsk_ref, ng_ref, y_ref, hout_ref, ht_ref,
                  *, nheads, groups, hd, ns):
    c = pl.program_id(0)
    cs = dt_ref.shape[0]
    hpg = nheads // groups
    gp = hpg * hd
    di = nheads * hd

    @pl.when(c == 0)
    def _():
        ht_ref[...] = jnp.zeros_like(ht_ref)

    dt = dt_ref[...]
    a_neg = jnp.where(_iota((1, LANES), 1) < nheads, -jnp.exp(alog_ref[...]), 0.0)
    acum = _sel_left(_tri(cs), dt * a_neg)
    acum_t = acum.T
    dt_t = dt.T
    tot = acum[cs - 1:cs, :]
    ex = (_iota((LANES, di), 1) // hd == _iota((LANES, di), 0)).astype(BF)
    coefx = _sel_right(jnp.exp(tot - acum) * dt, ex, parts=2)
    cdx = _sel_right(jnp.broadcast_to(jnp.exp(tot), (SUBLANES, LANES)), ex, parts=2)[0:1, :]
    causal = _iota((cs, cs), 1) <= _iota((cs, cs), 0)

    for g in range(groups):
        xs = xbc_ref[:, g * gp:(g + 1) * gp]
        bm = xbc_ref[:, di + g * ns:di + (g + 1) * ns]
        cm = xbc_ref[:, di + (groups + g) * ns:di + (groups + g + 1) * ns]
        cb = _dot_nt(cm.astype(BF), bm.astype(BF))
        ht = ht_ref[g]
        htb = ht.astype(BF)
        xb = xs.astype(BF)
        ys = []
        for r in range(hpg):
            hh = g * hpg + r
            acol = acum[:, hh:hh + 1]
            seg = jnp.broadcast_to(acol, (cs, cs)) - acum_t[hh:hh + 1, :]
            mp = cb * jnp.exp(jnp.where(causal, seg, NEG_INF)) * dt_t[hh:hh + 1, :]
            csc = cm * jnp.exp(jnp.broadcast_to(acol, (cs, ns)))
            lhs = jnp.concatenate([mp, csc], axis=1).astype(BF)
            rhs = jnp.concatenate([xb[:, r * hd:(r + 1) * hd], htb[:, r * hd:(r + 1) * hd]], axis=0)
            ys.append(_dot(lhs, rhs))
        yg = jnp.concatenate(ys, axis=1) + dsk_ref[:, g * gp:(g + 1) * gp] * xs
        y_ref[:, g * gp:(g + 1) * gp] = _gate_norm(
            yg, z_ref[:, g * gp:(g + 1) * gp], ng_ref[:, g * gp:(g + 1) * gp]).astype(BF)
        wg = (coefx[:, g * gp:(g + 1) * gp] * xs).astype(BF)
        ht_ref[g] = ht * cdx[:, g * gp:(g + 1) * gp] + _dot(bm.T.astype(BF), wg)

    @pl.when(c == pl.num_programs(0) - 1)
    def _():
        hout_ref[...] = ht_ref[...]


def _ssd_seq(xbc, dt, z, alog, dsk, ng, nheads, groups, hd, ns, chunk):
    l, cch = xbc.shape
    di = nheads * hd
    gp = di // groups
    return pl.pallas_call(
        functools.partial(_ssd_seq_body, nheads=nheads, groups=groups, hd=hd, ns=ns),
        grid=(l // chunk,),
        in_specs=[
            pl.BlockSpec((chunk, cch), lambda c: (c, 0)),
            pl.BlockSpec((chunk, LANES), lambda c: (c, 0)),
            pl.BlockSpec((chunk, di), lambda c: (c, 0)),
            pl.BlockSpec((1, LANES), lambda c: (0, 0)),
            pl.BlockSpec((1, di), lambda c: (0, 0)),
            pl.BlockSpec((1, di), lambda c: (0, 0)),
        ],
        out_specs=[pl.BlockSpec((chunk, di), lambda c: (c, 0)),
                   pl.BlockSpec((groups, ns, gp), lambda c: (0, 0, 0))],
        out_shape=[jax.ShapeDtypeStruct((l, di), BF),
                   jax.ShapeDtypeStruct((groups, ns, gp), F32)],
        scratch_shapes=[pltpu.VMEM((groups, ns, gp), F32)],
        compiler_params=_cparams("arbitrary"),
        name="ssd_seq",
    )(xbc, dt, z, alog, dsk, ng)


def _ssd_step_body(xbc_ref, dt_ref, z_ref, alog_ref, alogc_ref, dsk_ref, ng_ref, h0_ref, y_ref, hn_ref,
                   *, nheads, groups, hd, ns):
    t = dt_ref.shape[0]
    hpg = nheads // groups
    gp = hpg * hd
    di = nheads * hd
    gn = groups * ns

    dt = dt_ref[...]
    a_neg = jnp.where(_iota((1, LANES), 1) < nheads, -jnp.exp(alog_ref[...]), 0.0)
    rt = _iota((t, LANES), 0)
    acum = dt * a_neg
    sh = 1
    while sh < t:
        acum = acum + jnp.where(rt >= sh, pltpu.roll(acum, sh, 0), 0.0)
        sh *= 2
    tot = acum[t - 1:t, :]
    coef = jnp.exp(tot - acum) * dt
    eac = jnp.exp(acum)

    xs = xbc_ref[:, 0:di]
    bm = xbc_ref[:, di:di + gn]
    cm = xbc_ref[:, di + gn:di + 2 * gn]

    prods = jnp.concatenate([cm * bm[s:s + 1, :] for s in range(t)], axis=0)
    rsel = ((_iota((gn, LANES), 1) // hpg == _iota((gn, LANES), 0) // ns)
            & (_iota((gn, LANES), 1) < nheads)).astype(BF)
    cbe = _dot(prods.astype(BF), rsel)
    acl = jnp.concatenate([acum] * t, axis=0)
    acs = jnp.concatenate([jnp.broadcast_to(acum[s:s + 1, :], (t, LANES)) for s in range(t)], axis=0)
    dts = jnp.concatenate([jnp.broadcast_to(dt[s:s + 1, :], (t, LANES)) for s in range(t)], axis=0)
    rr = _iota((t * t, LANES), 0)
    ms = cbe * jnp.exp(jnp.where(rr % t >= rr // t, acl - acs, NEG_INF)) * dts
    ex = (_iota((LANES, di), 1) // hd == _iota((LANES, di), 0)).astype(BF)
    big = _sel_right(jnp.concatenate([ms, eac, coef], axis=0), ex, parts=2)
    y = big[0:t, :] * xs[0:1, :]
    for s in range(1, t):
        y = y + big[s * t:(s + 1) * t, :] * xs[s:s + 1, :]
    eacx = big[t * t:t * t + t, :]
    coefx = big[t * t + t:, :]

    dt_t = jnp.concatenate([dt, jnp.zeros((LANES - t, LANES), F32)], axis=0).T
    a_col = -jnp.exp(alogc_ref[...])
    cd_col = jnp.exp(jnp.sum(dt_t * a_col, axis=1, keepdims=True))

    for g in range(groups):
        gs = slice(g * gp, (g + 1) * gp)
        h0 = h0_ref[gs, :]
        yoff = _dot_nt(cm[:, g * ns:(g + 1) * ns].astype(BF), h0.astype(BF))
        yg = y[:, gs] + yoff * eacx[:, gs] + dsk_ref[:, gs] * xs[:, gs]
        y_ref[:, gs] = _gate_norm(yg, z_ref[:, gs], ng_ref[:, gs]).astype(BF)
        wg = (coefx[:, gs] * xs[:, gs]).astype(BF)
        wpad = jnp.concatenate([wg.astype(F32), jnp.zeros((LANES - t, gp), F32)], axis=0)
        wt = jnp.concatenate([wpad[:, k * LANES:(k + 1) * LANES].T for k in range(gp // LANES)], axis=0)
        bpad = jnp.concatenate([bm[:, g * ns:(g + 1) * ns], jnp.zeros((LANES - t, ns), F32)], axis=0)
        upd = _dot(wt.astype(BF), bpad.astype(BF))
        for r in range(hpg):
            hh = g * hpg + r
            rs = slice(r * hd, (r + 1) * hd)
            hn_ref[g * gp + r * hd:g * gp + (r + 1) * hd, :] = (
                h0[rs, :] * jnp.broadcast_to(cd_col[hh:hh + 1, :], (hd, ns)) + upd[rs, :])


def _ssd_step(xbc, dt, z, alog, alogc, dsk, ng, h0, nheads, groups, hd, ns):
    nb = h0.shape[0]
    m, cch = xbc.shape
    t = m // nb
    di = nheads * hd
    return pl.pallas_call(
        functools.partial(_ssd_step_body, nheads=nheads, groups=groups, hd=hd, ns=ns),
        grid=(nb,),
        in_specs=[
            pl.BlockSpec((t, cch), lambda b: (b, 0)),
            pl.BlockSpec((t, LANES), lambda b: (b, 0)),
            pl.BlockSpec((t, di), lambda b: (b, 0)),
            pl.BlockSpec((1, LANES), lambda b: (0, 0)),
            pl.BlockSpec((LANES, 1), lambda b: (0, 0)),
            pl.BlockSpec((1, di), lambda b: (0, 0)),
            pl.BlockSpec((1, di), lambda b: (0, 0)),
            pl.BlockSpec((None, di, ns), lambda b: (b, 0, 0)),
        ],
        out_specs=[pl.BlockSpec((t, di), lambda b: (b, 0)),
                   pl.BlockSpec((None, di, ns), lambda b: (b, 0, 0))],
        out_shape=[jax.ShapeDtypeStruct((m, di), BF),
                   jax.ShapeDtypeStruct((nb, di, ns), F32)],
        compiler_params=_cparams("parallel"),
        name="ssd_step",
    )(xbc, dt, z, alog, alogc, dsk, ng, h0)


def _pad_lanes(a):
    return jnp.pad(a, [(0, 0)] * (a.ndim - 1) + [(0, LANES - a.shape[-1])])


def _trunk(x, nb, page_table, cache_k, cache_v, cache_lf, conf_prev, mconv_prev, ssm_prev, ffn_prev, w):
    _, t, d = x.shape
    m = nb * t
    stepwise = conf_prev is not None
    nh, dh = cache_k.shape[-2:]
    fw = nh * dh
    x = x.reshape(m, d)

    w_in = w["att_w_in"][0]
    cc = w["conf_dw_w"].shape[2]
    cw = w["conf_dw_w"].shape[1]
    w5 = jnp.concatenate([w_in[:, :3 * fw], w_in[:, 3 * fw + nh:]], axis=1)
    wf = _pad_lanes(w_in[:, 3 * fw:3 * fw + nh]).astype(BF)
    bfp = _pad_lanes(w["att_b_f"][0].reshape(1, nh))
    q, k, v, u, lf = _even_in(x, w["ln_mix"][0].reshape(1, d), w5, wf, bfp, fw, cc)
    logf = lf[:, :nh]
    if stepwise:
        npool, page = cache_k.shape[1:3]
        att = _fox_sample(q, k, v, lf,
                          cache_k[0].reshape(npool, page * nh, dh), cache_v[0].reshape(npool, page * nh, dh),
                          cache_lf[0].reshape(npool, page * nh // LANES, LANES), page_table, nh, dh)
        xp = jnp.concatenate([conf_prev[0], u.reshape(nb, t, cc)], axis=1)
        conf_state = xp[:, t:]
        cv = _conv_step(xp, w["conf_dw_w"][0], w["conf_dw_b"][0], act=False).reshape(m, cc)
    else:
        cq, ckt = _fox_cumsum(lf, nh)
        att = _fox_prompt(q, k, v, cq, ckt.reshape(nh, 1, m), nh, dh)
        conf_state = u[m - (cw - 1):].reshape(1, cw - 1, cc)
        cv = _conv_seq(u, w["conf_dw_w"][0], w["conf_dw_b"][0], act=False)
    w_out = w["att_w_out"][0]
    x = _out_even(x, att, cv, w["conf_ln_g"][0], w["conf_ln_b"][0], w_out[:fw], w_out[fw:])

    ffn_states = []

    def ffn(x, i, final_g):
        if stepwise:
            two_f = ffn_prev.shape[-1]
            st = jnp.zeros((nb, t, two_f), F32).at[:, t - 2:].set(ffn_prev[i]).reshape(m, two_f)
        else:
            st = None
        x, us = _ffn(x, w["ln_ffn"][i], w["ffn_w_up"][i], w["ffn_dw_w"][i], w["ffn_dw_b"][i],
                     w["ffn_w_down"][i], st, final_g, t)
        ffn_states.append(us)
        return x

    x = ffn(x, 0, None)

    nheads, hd, ns = w["ssm_state_shape"]
    di = nheads * hd
    cch = w["ssm_conv_w"].shape[2]
    groups = (cch - di) // (2 * ns)
    w_in = w["ssm_w_in"][0]
    wdt = _pad_lanes(w_in[:, di + cch:]).astype(BF)
    dtb = _pad_lanes(w["ssm_dt_bias"][0].reshape(1, nheads))
    sw = w["ssm_conv_w"].shape[1]
    if stepwise:
        st = jnp.zeros((nb, t, cch), F32).at[:, t - (sw - 1):].set(mconv_prev[0]).reshape(m, cch)
    else:
        st = None
    z, xbc, dt, mconv_state = _ssm_in(x, w["ln_mix"][1].reshape(1, d), w_in, wdt, dtb,
                                      w["ssm_conv_w"][0], w["ssm_conv_b"][0], st, di, cch, t)
    alog = _pad_lanes(w["ssm_a_log"][0].reshape(1, nheads))
    dsk = jnp.repeat(w["ssm_d"][0], hd).reshape(1, di)
    ng = w["ssm_norm_g"][0].reshape(1, di)
    if stepwise:
        yn, hn = _ssd_step(xbc, dt, z, alog, alog.reshape(LANES, 1), dsk, ng,
                           ssm_prev[0].reshape(nb, di, ns), nheads, groups, hd, ns)
        ssm_state = hn.reshape(nb, nheads, hd, ns)
    else:
        chunk = _tile(m, 128)
        yn, hout = _ssd_seq(xbc, dt, z, alog, dsk, ng, nheads, groups, hd, ns, chunk)
        hpg = nheads // groups
        ssm_state = jnp.transpose(hout.reshape(groups, ns, hpg, hd), (0, 2, 3, 1)).reshape(1, nheads, hd, ns)
    x = _mm_res(x, yn, w["ssm_w_out"][0])
    x = ffn(x, 1, w["ln_final"])

    return (x.reshape(nb, t, d),
            k.reshape(1, nb, t, nh, dh), v.reshape(1, nb, t, nh, dh), logf.reshape(1, nb, t, nh),
            conf_state[None], mconv_state[None], ssm_state[None], jnp.stack(ffn_states))


def kernel(x_prompt, x_sample, cache_k, cache_v, cache_logf, state_conf_conv, state_ssm_conv, state_ssm,
           state_ffn_conv, page_table,
           ln_mix, ln_ffn, ln_final,
           att_w_in, att_b_f, conf_dw_w, conf_dw_b, conf_ln_g, conf_ln_b, att_w_out,
           ssm_w_in, ssm_conv_w, ssm_conv_b, ssm_dt_bias, ssm_a_log, ssm_d, ssm_norm_g, ssm_w_out,
           ffn_w_up, ffn_dw_w, ffn_dw_b, ffn_w_down):
    assert ln_mix.shape[0] == 2 and x_prompt.shape[0] == 1, "two layers, one prompt sequence"
    w = dict(ln_mix=ln_mix, ln_ffn=ln_ffn, ln_final=ln_final,
             att_w_in=att_w_in, att_b_f=att_b_f, conf_dw_w=conf_dw_w, conf_dw_b=conf_dw_b,
             conf_ln_g=conf_ln_g, conf_ln_b=conf_ln_b, att_w_out=att_w_out,
             ssm_w_in=ssm_w_in, ssm_conv_w=ssm_conv_w, ssm_conv_b=ssm_conv_b, ssm_dt_bias=ssm_dt_bias,
             ssm_a_log=ssm_a_log, ssm_d=ssm_d, ssm_norm_g=ssm_norm_g, ssm_w_out=ssm_w_out,
             ffn_w_up=ffn_w_up, ffn_dw_w=ffn_dw_w, ffn_dw_b=ffn_dw_b, ffn_w_down=ffn_w_down,
             ssm_state_shape=state_ssm.shape[2:])
    nb = x_sample.shape[0]
    outs_p = _trunk(x_prompt, 1, None, cache_k, cache_v, cache_logf, None, None, None, None, w)
    outs_s = _trunk(x_sample, nb, page_table, cache_k, cache_v, cache_logf,
                    state_conf_conv, state_ssm_conv, state_ssm, state_ffn_conv, w)
    return (outs_p[0], outs_s[0]) + tuple(outs_p[1:]) + tuple(outs_s[1:])
```

```python
import functools

import jax
import jax.numpy as jnp
from jax import lax
from jax.experimental import pallas as pl
from jax.experimental.pallas import tpu as pltpu

EPS = 1e-6
BF = jnp.bfloat16
F32 = jnp.float32
LANES = 128
SUBLANES = 8
VMEM_LIMIT = 56 * 1024 * 1024
NEG_INF = float("-inf")
PROJ_ROWS = 1024
FFN_ROWS = 512


def _cparams(*sem):
    return pltpu.CompilerParams(dimension_semantics=sem, vmem_limit_bytes=VMEM_LIMIT)


def _tile(n, pref):
    t = pref
    while t >= LANES and t % LANES == 0:
        if n % t == 0:
            return t
        t //= 2
    return n


def _iota(shape, dim):
    return lax.broadcasted_iota(jnp.int32, shape, dim)


def _split3(x):
    hi = x.astype(BF)
    r1 = x - hi.astype(F32)
    mid = r1.astype(BF)
    lo = (r1 - mid.astype(F32)).astype(BF)
    return hi, mid, lo


def _dot(a, b):
    return jnp.dot(a, b, preferred_element_type=F32)


def _dot_nt(a, b):
    return lax.dot_general(a, b, (((1,), (1,)), ((), ())), preferred_element_type=F32)


def _sel_left(sel, x, parts=3):
    ps = _split3(x)[:parts]
    out = _dot(sel, ps[0])
    for p in ps[1:]:
        out = out + _dot(sel, p)
    return out


def _sel_right(x, sel, parts=3):
    ps = _split3(x)[:parts]
    out = _dot(ps[0], sel)
    for p in ps[1:]:
        out = out + _dot(p, sel)
    return out


def _log_sigmoid(x):
    return jnp.minimum(x, 0.0) - jnp.log(1.0 + jnp.exp(-jnp.abs(x)))


def _softplus(x):
    return jnp.maximum(x, 0.0) + jnp.log(1.0 + jnp.exp(-jnp.abs(x)))


def _silu(x):
    return x * jax.nn.sigmoid(x)


def _rms(x, g):
    return x * lax.rsqrt(jnp.mean(x * x, axis=-1, keepdims=True) + EPS) * g


def _tri(n):
    return (_iota((n, n), 1) <= _iota((n, n), 0)).astype(BF)


def _even_in_body(x_ref, g_ref, w_ref, wf_ref, bf_ref,
                  q_ref, k_ref, v_ref, u_ref, lf_ref, xn_ref, a_ref, *, nf, nc):
    j = pl.program_id(1)

    @pl.when(j == 0)
    def _():
        xn = _rms(x_ref[...], g_ref[...]).astype(BF)
        xn_ref[...] = xn
        lf_ref[...] = _log_sigmoid(_dot(xn, wf_ref[...]) + bf_ref[...])

    z = _dot(xn_ref[...], w_ref[...])

    @pl.when(j < nf)
    def _():
        q_ref[...] = z

    @pl.when((j >= nf) & (j < 2 * nf))
    def _():
        k_ref[...] = z

    @pl.when((j >= 2 * nf) & (j < 3 * nf))
    def _():
        v_ref[...] = z

    @pl.when((j >= 3 * nf) & ((j - 3 * nf) % 2 == 0))
    def _():
        a_ref[...] = z

    @pl.when((j >= 3 * nf) & ((j - 3 * nf) % 2 == 1))
    def _():
        u_ref[...] = a_ref[...] * jax.nn.sigmoid(z)


def _even_in(x, g, w5, wf, bfp, fw, cc):
    m, d = x.shape
    tm = _tile(m, PROJ_ROWS)
    tn = min(_tile(fw, 512), _tile(cc, 512))
    nf, nc = fw // tn, cc // tn
    nj = 3 * nf + 2 * nc

    def w_map(i, j):
        jj = j - 3 * nf
        return (0, jnp.where(j < 3 * nf, j, 3 * nf + (jj % 2) * nc + jj // 2))

    def seg_map(base, n, div=1):
        return lambda i, j: (i, jnp.clip((j - base) // div, 0, n - 1))

    return pl.pallas_call(
        functools.partial(_even_in_body, nf=nf, nc=nc),
        grid=(m // tm, nj),
        in_specs=[
            pl.BlockSpec((tm, d), lambda i, j: (i, 0)),
            pl.BlockSpec((1, d), lambda i, j: (0, 0)),
            pl.BlockSpec((d, tn), w_map),
            pl.BlockSpec((d, LANES), lambda i, j: (0, 0)),
            pl.BlockSpec((1, LANES), lambda i, j: (0, 0)),
        ],
        out_specs=[
            pl.BlockSpec((tm, tn), seg_map(0, nf)),
            pl.BlockSpec((tm, tn), seg_map(nf, nf)),
            pl.BlockSpec((tm, tn), seg_map(2 * nf, nf)),
            pl.BlockSpec((tm, tn), seg_map(3 * nf, nc, 2)),
            pl.BlockSpec((tm, LANES), lambda i, j: (i, 0)),
        ],
        out_shape=[
            jax.ShapeDtypeStruct((m, fw), F32),
            jax.ShapeDtypeStruct((m, fw), F32),
            jax.ShapeDtypeStruct((m, fw), F32),
            jax.ShapeDtypeStruct((m, cc), F32),
            jax.ShapeDtypeStruct((m, LANES), F32),
        ],
        scratch_shapes=[pltpu.VMEM((tm, d), BF), pltpu.VMEM((tm, tn), F32)],
        compiler_params=_cparams("parallel", "arbitrary"),
        name="even_in",
    )(x, g, w5.astype(BF), wf, bfp)


def _fox_cumsum_body(lf_ref, cq_ref, ckt_ref, carry_ref, *, nh):
    @pl.when(pl.program_id(0) == 0)
    def _():
        carry_ref[...] = jnp.zeros_like(carry_ref)

    lf = lf_ref[...]
    tb = lf.shape[0]
    cs = _sel_left(_tri(tb), lf) + carry_ref[...]
    carry_ref[...] = cs[tb - 1:tb, :]
    for h in range(nh):
        cq_ref[h] = jnp.broadcast_to(cs[:, h:h + 1], (tb, LANES))
    ckt_ref[...] = cs.T[:nh, :]


def _fox_cumsum(lf, nh):
    m = lf.shape[0]
    tb = _tile(m, 256)
    return pl.pallas_call(
        functools.partial(_fox_cumsum_body, nh=nh),
        grid=(m // tb,),
        in_specs=[pl.BlockSpec((tb, LANES), lambda i: (i, 0))],
        out_specs=[pl.BlockSpec((nh, tb, LANES), lambda i: (0, i, 0)),
                   pl.BlockSpec((nh, tb), lambda i: (0, i))],
        out_shape=[jax.ShapeDtypeStruct((nh, m, LANES), F32),
                   jax.ShapeDtypeStruct((nh, m), F32)],
        scratch_shapes=[pltpu.VMEM((1, LANES), F32)],
        compiler_params=_cparams("arbitrary"),
        name="fox_cumsum",
    )(lf)


def _fox_prompt_body(q_ref, k_ref, v_ref, cq_ref, ck_ref, o_ref, kb_ref, vb_ref, m_ref, acc_ref, s_ref, *, scale):
    qi = pl.program_id(1)
    t, dh = q_ref.shape
    log2e = 1.4426950408889634

    @pl.when(qi == 0)
    def _():
        kb_ref[...] = k_ref[...].astype(BF)
        vb_ref[:, 0:dh] = v_ref[...].astype(BF)
        vb_ref[:, dh:] = (_iota((k_ref.shape[0], dh), 1) == 0).astype(BF)

    qs = (q_ref[...] * (scale * log2e)).astype(BF)
    m_ref[...] = jnp.full_like(m_ref, NEG_INF)
    acc_ref[...] = jnp.zeros_like(acc_ref)
    reps = t // LANES
    rc = min(t, 128)

    def scores(kj, slot):
        off = pl.multiple_of(kj * t, t)
        s_ref[slot] = _dot_nt(qs, kb_ref[pl.ds(off, t), :]) - ck_ref[kj] * log2e

    def update(kj, slot, masked):
        off = pl.multiple_of(kj * t, t)
        for r in range(t // rc):
            rows = slice(r * rc, (r + 1) * rc)
            s = s_ref[slot, rows, :]
            if masked:
                s = jnp.where(_iota((rc, t), 1) <= r * rc + _iota((rc, t), 0), s, NEG_INF)
            cq = cq_ref[rows, :] * log2e
            m_prev = m_ref[rows, :]
            m_new = jnp.maximum(m_prev, jnp.max(s, axis=1, keepdims=True) + cq)
            shift = m_new - cq
            p = jnp.exp2(s - jnp.concatenate([shift] * reps, axis=1))
            alpha = jnp.exp2(m_prev - m_new)
            acc_ref[rows, :] = (jnp.concatenate([alpha] * (2 * dh // LANES), axis=1) * acc_ref[rows, :]
                                + _dot(p.astype(BF), vb_ref[pl.ds(off, t), :]))
            m_ref[rows, :] = m_new

    scores(0, 0)

    def body(k2, carry):
        update(2 * k2, 0, False)
        scores(2 * k2 + 1, 1)
        update(2 * k2 + 1, 1, False)
        scores(2 * k2 + 2, 0)
        return carry

    lax.fori_loop(0, qi // 2, body, 0)

    @pl.when(qi % 2 == 0)
    def _():
        update(qi, 0, True)

    @pl.when(qi % 2 == 1)
    def _():
        update(qi - 1, 0, False)
        scores(qi, 1)
        update(qi, 1, True)

    o_ref[...] = acc_ref[:, 0:dh] / acc_ref[:, dh:dh + 1]


def _fox_prompt(q, k, v, cq, ck, nh, dh):
    m = q.shape[0]
    t = _tile(m, 512)
    n = m // t
    return pl.pallas_call(
        functools.partial(_fox_prompt_body, scale=dh ** -0.5),
        grid=(nh, n),
        in_specs=[
            pl.BlockSpec((t, dh), lambda h, i: (i, h)),
            pl.BlockSpec((m, dh), lambda h, i: (0, h)),
            pl.BlockSpec((m, dh), lambda h, i: (0, h)),
            pl.BlockSpec((None, t, LANES), lambda h, i: (h, i, 0)),
            pl.BlockSpec((None, n, 1, t), lambda h, i: (h, 0, 0, 0)),
        ],
        out_specs=pl.BlockSpec((t, dh), lambda h, i: (i, h)),
        out_shape=jax.ShapeDtypeStruct((m, nh * dh), F32),
        scratch_shapes=[pltpu.VMEM((m, dh), BF), pltpu.VMEM((m, 2 * dh), BF),
                        pltpu.VMEM((t, LANES), F32), pltpu.VMEM((t, 2 * dh), F32),
                        pltpu.VMEM((2, t, t), F32)],
        compiler_params=_cparams("parallel", "arbitrary"),
        name="fox_prompt",
    )(q, k, v, cq, ck.reshape(nh, n, 1, t))


def _fox_sample_body(pt_ref, q_ref, kn_ref, vn_ref, lfn_ref, *rest, npg, nh, dh, scale):
    k_refs = rest[:npg]
    v_refs = rest[npg:2 * npg]
    lf_refs = rest[2 * npg:3 * npg]
    o_ref, s_ref, acc_ref = rest[3 * npg:]
    t, fw = q_ref.shape
    page = k_refs[0].shape[0] // nh
    ht = nh * t
    ppr = LANES // nh

    def load_page(ref):
        return jnp.concatenate([ref[pl.ds(h, page, stride=nh), :] for h in range(nh)], axis=1)

    q = q_ref[...]
    qt = jnp.concatenate([q] * nh + [jnp.zeros((LANES - ht, fw), F32)], axis=0)
    qbd = jnp.where(_iota((LANES, fw), 0) // t == _iota((LANES, fw), 1) // dh, qt, 0.0).astype(BF)

    r2 = _iota((LANES, LANES), 0)
    c2 = _iota((LANES, LANES), 1)
    e_past = ((r2 % nh == c2 // t) & (c2 < ht)).astype(BF)
    e_new = ((r2 == c2 // t) & (c2 < ht)).astype(BF)
    tri = _tri(page)
    rp = _iota((page, LANES), 0)
    cp = _iota((page, LANES), 1)

    carry = jnp.zeros((1, LANES), F32)
    for p in range(npg):
        lfp = lf_refs[p][...]
        b = jnp.concatenate([jnp.broadcast_to(lfp[r:r + 1, :], (ppr, LANES))
                             for r in range(page // ppr)], axis=0)
        b = jnp.where(cp // nh == rp % ppr, b, 0.0)
        clocal = _sel_left(tri, _sel_right(b, e_past))
        st = _dot_nt(load_page(k_refs[p]).astype(BF), qbd) * scale
        s_ref[p * page:(p + 1) * page, :] = st - (clocal + carry)
        carry = carry + clocal[page - 1:page, :]

    y = _sel_right(lfn_ref[...], e_new)
    rt = _iota((t, LANES), 0)
    sh = 1
    while sh < t:
        y = y + jnp.where(rt >= sh, pltpu.roll(y, sh, 0), 0.0)
        sh *= 2
    cn = carry + y
    cq = jnp.sum(jnp.where(rt == _iota((t, LANES), 1) % t, cn, 0.0), axis=0, keepdims=True)
    knp = jnp.concatenate([kn_ref[...], jnp.zeros((page - t, fw), F32)], axis=0)
    stn = _dot_nt(knp.astype(BF), qbd) * scale
    cnp = jnp.concatenate([cn, jnp.zeros((page - t, LANES), F32)], axis=0)
    s_ref[npg * page:, :] = jnp.where((rp < t) & (rp <= cp % t), stn - cnp, NEG_INF)

    mx = jnp.max(s_ref[0:page, :], axis=0, keepdims=True)
    for p in range(1, npg + 1):
        mx = jnp.maximum(mx, jnp.max(s_ref[p * page:(p + 1) * page, :], axis=0, keepdims=True))
    shift = cq - (mx + cq)

    acc_ref[...] = jnp.zeros_like(acc_ref)
    lsum = jnp.zeros((LANES, 1), F32)
    for p in range(npg + 1):
        pt = jnp.exp(s_ref[p * page:(p + 1) * page, :] + shift).T
        lsum = lsum + jnp.sum(pt, axis=1, keepdims=True)
        if p < npg:
            vp = load_page(v_refs[p])
        else:
            vp = jnp.concatenate([vn_ref[...], jnp.zeros((page - t, fw), F32)], axis=0)
        acc_ref[...] += _dot(pt[:ht, :].astype(BF), vp.astype(BF))

    for h in range(nh):
        o_ref[:, h * dh:(h + 1) * dh] = (acc_ref[h * t:(h + 1) * t, h * dh:(h + 1) * dh]
                                          / lsum[h * t:(h + 1) * t, :])


def _fox_sample(q, kn, vn, lfn, ck, cv, clf, page_table, nh, dh):
    nb, npg = page_table.shape
    m, fw = q.shape
    t = m // nb
    prows = ck.shape[1]
    page = prows // nh
    rows = page * nh // LANES

    def pg_map(p):
        return lambda b, pt: (pt[b * npg + p], 0, 0)

    seq = pl.BlockSpec((t, fw), lambda b, pt: (b, 0))
    in_specs = [seq, seq, seq, pl.BlockSpec((t, LANES), lambda b, pt: (b, 0))]
    in_specs += [pl.BlockSpec((None, prows, dh), pg_map(p)) for p in range(npg)]
    in_specs += [pl.BlockSpec((None, prows, dh), pg_map(p)) for p in range(npg)]
    in_specs += [pl.BlockSpec((None, rows, LANES), pg_map(p)) for p in range(npg)]
    return pl.pallas_call(
        functools.partial(_fox_sample_body, npg=npg, nh=nh, dh=dh, scale=dh ** -0.5),
        grid_spec=pltpu.PrefetchScalarGridSpec(
            num_scalar_prefetch=1,
            grid=(nb,),
            in_specs=in_specs,
            out_specs=pl.BlockSpec((t, fw), lambda b, pt: (b, 0)),
            scratch_shapes=[pltpu.VMEM(((npg + 1) * page, LANES), F32),
                            pltpu.VMEM((nh * t, fw), F32)],
        ),
        out_shape=jax.ShapeDtypeStruct((m, fw), F32),
        compiler_params=_cparams("parallel"),
        name="fox_sample",
    )(page_table.reshape(-1), q, kn, vn, lfn, *([ck] * npg), *([cv] * npg), *([clf] * npg))


def _conv_seq_body(xm_ref, xh_ref, w_ref, b_ref, o_ref, xf_ref, xs_ref, *, width):
    tl = xm_ref.shape[0]
    hb = xh_ref.shape[0]
    n = hb + tl
    xf_ref[0:hb, :] = jnp.where(pl.program_id(0) == 0, 0.0, xh_ref[...])
    xf_ref[hb:n, :] = xm_ref[...]
    xf_ref[n:, :] = jnp.zeros((SUBLANES, xf_ref.shape[1]), F32)
    for b in range(SUBLANES):
        xs_ref[b] = xf_ref[b:b + n, :]
    base = hb - (width - 1)
    rc = min(tl, 32)
    for c in range(tl // rc):
        acc = b_ref[...]
        for j in range(width):
            a, b = divmod(base + c * rc + j, SUBLANES)
            acc = acc + w_ref[j:j + 1, :] * xs_ref[b, a * SUBLANES:a * SUBLANES + rc, :]
        o_ref[c * rc:(c + 1) * rc, :] = acc


def _conv_seq(x, w, b):
    l, c = x.shape
    width = w.shape[0]
    hb = -(-(width - 1) // SUBLANES) * SUBLANES
    tl = _tile(l, 512)
    tc = _tile(c, 256)
    r = tl // hb
    return pl.pallas_call(
        functools.partial(_conv_seq_body, width=width),
        grid=(l // tl, c // tc),
        in_specs=[
            pl.BlockSpec((tl, tc), lambda i, j: (i, j)),
            pl.BlockSpec((hb, tc), lambda i, j: (jnp.maximum(i * r - 1, 0), j)),
            pl.BlockSpec((width, tc), lambda i, j: (0, j)),
            pl.BlockSpec((1, tc), lambda i, j: (0, j)),
        ],
        out_specs=pl.BlockSpec((tl, tc), lambda i, j: (i, j)),
        out_shape=jax.ShapeDtypeStruct((l, c), F32),
        scratch_shapes=[pltpu.VMEM((hb + tl + SUBLANES, tc), F32),
                        pltpu.VMEM((SUBLANES, hb + tl, tc), F32)],
        compiler_params=_cparams("parallel", "parallel"),
        name="conv_seq",
    )(x, x, w, b.reshape(1, c))


def _conv_step_body(xp_ref, w_ref, b_ref, o_ref, *, width):
    t = o_ref.shape[1]
    acc = b_ref[...] + w_ref[0:1, :] * xp_ref[:, 0:t, :]
    for j in range(1, width):
        acc = acc + w_ref[j:j + 1, :] * xp_ref[:, j:j + t, :]
    o_ref[...] = acc


def _conv_step(xp, w, b):
    nb, rows, c = xp.shape
    width = w.shape[0]
    t = rows - (width - 1)
    bb = 16 if nb % 16 == 0 else nb
    tc = _tile(c, 512)
    return pl.pallas_call(
        functools.partial(_conv_step_body, width=width),
        grid=(nb // bb, c // tc),
        in_specs=[
            pl.BlockSpec((bb, rows, tc), lambda i, j: (i, 0, j)),
            pl.BlockSpec((width, tc), lambda i, j: (0, j)),
            pl.BlockSpec((1, tc), lambda i, j: (0, j)),
        ],
        out_specs=pl.BlockSpec((bb, t, tc), lambda i, j: (i, 0, j)),
        out_shape=jax.ShapeDtypeStruct((nb, t, c), F32),
        compiler_params=_cparams("parallel", "parallel"),
        name="conv_step",
    )(xp, w, b.reshape(1, c))


def _out_even_body(x_ref, att_ref, cv_ref, lg_ref, lb_ref, wa_ref, wc_ref, o_ref, ab_ref, cb_ref):
    @pl.when(pl.program_id(1) == 0)
    def _():
        u = cv_ref[...]
        xc = u - jnp.mean(u, axis=-1, keepdims=True)
        var = jnp.mean(xc * xc, axis=-1, keepdims=True)
        cb_ref[...] = _silu(xc * lax.rsqrt(var + EPS) * lg_ref[...] + lb_ref[...]).astype(BF)
        ab_ref[...] = att_ref[...].astype(BF)

    o_ref[...] = x_ref[...] + _dot(ab_ref[...], wa_ref[...]) + _dot(cb_ref[...], wc_ref[...])


def _out_even(x, att, cv, lg, lb, wa, wc):
    m, d = x.shape
    fw, cc = att.shape[1], cv.shape[1]
    tm = _tile(m, PROJ_ROWS)
    tn = _tile(d, 512)
    return pl.pallas_call(
        _out_even_body,
        grid=(m // tm, d // tn),
        in_specs=[
            pl.BlockSpec((tm, tn), lambda i, j: (i, j)),
            pl.BlockSpec((tm, fw), lambda i, j: (i, 0)),
            pl.BlockSpec((tm, cc), lambda i, j: (i, 0)),
            pl.BlockSpec((1, cc), lambda i, j: (0, 0)),
            pl.BlockSpec((1, cc), lambda i, j: (0, 0)),
            pl.BlockSpec((fw, tn), lambda i, j: (0, j)),
            pl.BlockSpec((cc, tn), lambda i, j: (0, j)),
        ],
        out_specs=pl.BlockSpec((tm, tn), lambda i, j: (i, j)),
        out_shape=jax.ShapeDtypeStruct((m, d), F32),
        scratch_shapes=[pltpu.VMEM((tm, fw), BF), pltpu.VMEM((tm, cc), BF)],
        compiler_params=_cparams("parallel", "arbitrary"),
        name="out_even",
    )(x, att, cv, lg.reshape(1, cc), lb.reshape(1, cc), wa.astype(BF), wc.astype(BF))


def _mm_res_body(x_ref, a_ref, w_ref, o_ref):
    o_ref[...] = x_ref[...] + _dot(a_ref[...], w_ref[...])


def _mm_res(x, a, w):
    m, d = x.shape
    kk = a.shape[1]
    tm = _tile(m, PROJ_ROWS)
    tn = _tile(d, 512)
    return pl.pallas_call(
        _mm_res_body,
        grid=(m // tm, d // tn),
        in_specs=[
            pl.BlockSpec((tm, tn), lambda i, j: (i, j)),
            pl.BlockSpec((tm, kk), lambda i, j: (i, 0)),
            pl.BlockSpec((kk, tn), lambda i, j: (0, j)),
        ],
        out_specs=pl.BlockSpec((tm, tn), lambda i, j: (i, j)),
        out_shape=jax.ShapeDtypeStruct((m, d), F32),
        compiler_params=_cparams("parallel", "arbitrary"),
        name="mm_res",
    )(x, a, w.astype(BF))


def _causal_conv(u, w_ref, b_ref, prev, seq):
    tm = u.shape[0]
    width = w_ref.shape[0]
    row = _iota(prev.shape, 0)
    acc = b_ref[...] + w_ref[width - 1:width, :] * u
    for s in range(1, width):
        r = pltpu.roll(u, s, 0)
        if seq is None:
            head = jnp.where(row < s, pltpu.roll(prev, s, 0), r[:SUBLANES])
            r = jnp.concatenate([head, r[SUBLANES:]], axis=0)
        else:
            r = jnp.where(row % seq < s, pltpu.roll(prev, (tm - seq + s) % tm, 0), r)
        acc = acc + w_ref[width - 1 - s:width - s, :] * r
    return acc


def _save_seq_tails(u, us_ref, col, ub_ref, seq, keep):
    tm, c = u.shape
    for k in range(c // LANES):
        ub_ref[...] = u[:, k * LANES:(k + 1) * LANES]
        for r in range(keep):
            us_ref[r, :, col + k * LANES:col + (k + 1) * LANES] = (
                ub_ref[pl.ds(seq - keep + r, tm // seq, stride=seq), :])


def _ffn_body(*refs, stepwise, final, seq, nf):
    x_ref, g_ref, wa_ref, wv_ref, dwa_ref, dwv_ref, ba_ref, bv_ref, wd_ref = refs[:9]
    k = 9
    if stepwise:
        sa_ref, sv_ref = refs[k:k + 2]
        k += 2
    if final:
        fg_ref = refs[k]
        k += 1
    o_ref, us_ref, xn_ref = refs[k:k + 3]
    k += 3
    i = pl.program_id(0)
    j = pl.program_id(1)
    tm, tf = x_ref.shape[0], wa_ref.shape[1]
    width = dwa_ref.shape[0]

    @pl.when(j == 0)
    def _():
        xn_ref[...] = _rms(x_ref[...], g_ref[...]).astype(BF)
        o_ref[...] = x_ref[...]

    xn = xn_ref[...]
    if stepwise:
        ua = _dot(xn, wa_ref[...])
        uv = _dot(xn, wv_ref[...])
        _save_seq_tails(ua, us_ref, 0, refs[k], seq, width - 1)
        _save_seq_tails(uv, us_ref, tf, refs[k], seq, width - 1)
        a = _causal_conv(ua, dwa_ref, ba_ref, sa_ref[...], seq)
        v = _causal_conv(uv, dwv_ref, bv_ref, sv_ref[...], seq)
        o_ref[...] += _dot((_silu(a) * v).astype(BF), wd_ref[...])
    else:
        ua_ref, uv_ref, h_ref, ca_ref, cv_ref = refs[k:k + 5]

        @pl.when(i == 0)
        def _():
            ca_ref[j] = jnp.zeros((SUBLANES, tf), F32)
            cv_ref[j] = jnp.zeros((SUBLANES, tf), F32)

        ua_ref[0:SUBLANES, :] = ca_ref[j]
        uv_ref[0:SUBLANES, :] = cv_ref[j]
        ua_ref[SUBLANES:, :] = _dot(xn, wa_ref[...])
        uv_ref[SUBLANES:, :] = _dot(xn, wv_ref[...])
        ca_ref[j] = ua_ref[tm:, :]
        cv_ref[j] = uv_ref[tm:, :]
        us_ref[:, 0:tf] = ua_ref[tm:, :]
        us_ref[:, tf:2 * tf] = uv_ref[tm:, :]
        rc = min(tm, 16)

        def conv_rows(u_ref, w_ref, b_ref, r0):
            win = u_ref[r0:r0 + SUBLANES + rc, :]
            acc = b_ref[...] + w_ref[width - 1:width, :] * win[SUBLANES:, :]
            for s in range(1, width):
                acc = acc + w_ref[width - 1 - s:width - s, :] * pltpu.roll(win, s, 0)[SUBLANES:, :]
            return acc

        for c in range(tm // rc):
            a = conv_rows(ua_ref, dwa_ref, ba_ref, c * rc)
            v = conv_rows(uv_ref, dwv_ref, bv_ref, c * rc)
            h_ref[c * rc:(c + 1) * rc, :] = (_silu(a) * v).astype(BF)
        o_ref[...] += _dot(h_ref[...], wd_ref[...])

    if final:
        @pl.when(j == nf - 1)
        def _():
            o_ref[...] = _rms(o_ref[...], fg_ref[...])


def _ffn(x, g, w_up, dw_w, dw_b, w_down, layer, state, final_g, seq):
    m, d = x.shape
    f = w_down.shape[1]
    tm = _tile(m, FFN_ROWS)
    tf = _tile(f, 512)
    nf = f // tf
    stepwise = state is not None
    final = final_g is not None
    row_spec = lambda i, j: (i, 0)
    a_col = lambda i, j: (0, j)
    v_col = lambda i, j: (0, j + nf)
    in_specs = [
        pl.BlockSpec((tm, d), row_spec),
        pl.BlockSpec((1, d), lambda i, j: (0, 0)),
        pl.BlockSpec((None, d, tf), lambda i, j: (layer, 0, j)),
        pl.BlockSpec((None, d, tf), lambda i, j: (layer, 0, j + nf)),
        pl.BlockSpec((3, tf), a_col),
        pl.BlockSpec((3, tf), v_col),
        pl.BlockSpec((1, tf), a_col),
        pl.BlockSpec((1, tf), v_col),
        pl.BlockSpec((None, tf, d), lambda i, j: (layer, j, 0)),
    ]
    w_up = w_up.astype(BF)
    w_down = w_down.astype(BF)
    args = [x, g.reshape(1, d), w_up, w_up, dw_w, dw_w, dw_b.reshape(1, 2 * f), dw_b.reshape(1, 2 * f), w_down]
    scratch = [pltpu.VMEM((tm, d), BF)]
    if stepwise:
        in_specs += [pl.BlockSpec((tm, tf), lambda i, j: (i, j)),
                     pl.BlockSpec((tm, tf), lambda i, j: (i, j + nf))]
        args += [state, state]
        us_shape = jax.ShapeDtypeStruct((2, nf, m // seq, 2 * tf), F32)
        us_spec = pl.BlockSpec((2, None, tm // seq, 2 * tf), lambda i, j: (0, j, i, 0))
        scratch += [pltpu.VMEM((tm, LANES), F32)]
    else:
        us_shape = jax.ShapeDtypeStruct((m // tm, nf, SUBLANES, 2 * tf), F32)
        us_spec = pl.BlockSpec((None, None, SUBLANES, 2 * tf), lambda i, j: (i, j, 0, 0))
        scratch += [pltpu.VMEM((SUBLANES + tm, tf), F32), pltpu.VMEM((SUBLANES + tm, tf), F32),
                    pltpu.VMEM((tm, tf), BF),
                    pltpu.VMEM((nf, SUBLANES, tf), F32), pltpu.VMEM((nf, SUBLANES, tf), F32)]
    if final:
        in_specs.append(pl.BlockSpec((1, d), lambda i, j: (0, 0)))
        args.append(final_g.reshape(1, d))
    out, us = pl.pallas_call(
        functools.partial(_ffn_body, stepwise=stepwise, final=final, seq=seq, nf=nf),
        grid=(m // tm, nf),
        in_specs=in_specs,
        out_specs=[pl.BlockSpec((tm, d), row_spec), us_spec],
        out_shape=[jax.ShapeDtypeStruct((m, d), F32), us_shape],
        scratch_shapes=scratch,
        compiler_params=_cparams("arbitrary", "arbitrary"),
        name="conv_ffn",
    )(*args)
    if stepwise:
        halves = [jnp.transpose(h, (2, 0, 1, 3)).reshape(m // seq, 2, f) for h in (us[..., :tf], us[..., tf:])]
    else:
        us = us[m // tm - 1]
        halves = [jnp.swapaxes(h, 0, 1).reshape(SUBLANES, f)[None, SUBLANES - 2:]
                  for h in (us[..., :tf], us[..., tf:])]
    return out, jnp.concatenate(halves, axis=-1)


def _ssm_in_body(*refs, nz, stepwise, seq):
    x_ref, g_ref, w_ref, wdt_ref, dtb_ref, cw_ref, cb_ref = refs[:7]
    k = 7
    if stepwise:
        st_ref = refs[k]
        k += 1
    z_ref, xbc_ref, dt_ref, us_ref, xn_ref, aux_ref = refs[k:k + 6]
    i = pl.program_id(0)
    j = pl.program_id(1)
    tm = x_ref.shape[0]
    keep = cw_ref.shape[0] - 1

    @pl.when(j == 0)
    def _():
        xn = _rms(x_ref[...], g_ref[...]).astype(BF)
        xn_ref[...] = xn
        dt_ref[...] = _softplus(_dot(xn, wdt_ref[...]) + dtb_ref[...])

    z = _dot(xn_ref[...], w_ref[...])

    @pl.when(j < nz)
    def _():
        z_ref[...] = z

    @pl.when(j >= nz)
    def _():
        if stepwise:
            prev = st_ref[...]
            _save_seq_tails(z, us_ref, 0, aux_ref, seq, keep)
        else:
            jx = j - nz

            @pl.when(i == 0)
            def _():
                aux_ref[jx] = jnp.zeros(aux_ref.shape[1:], F32)

            prev = aux_ref[jx]
            aux_ref[jx] = z[tm - SUBLANES:, :]
            us_ref[...] = z[tm - SUBLANES:, :]
        xbc_ref[...] = _silu(_causal_conv(z, cw_ref, cb_ref, prev, seq if stepwise else None))


def _ssm_in(x, g, w, wdt, dtb, cw, cb, state, di, cch, seq):
    m, d = x.shape
    tm = _tile(m, PROJ_ROWS)
    tn = min(_tile(di, 512), _tile(cch, 512))
    nz, nx = di // tn, cch // tn
    width = cw.shape[0]
    keep = width - 1
    stepwise = state is not None
    xcol = lambda i, j: (0, jnp.clip(j - nz, 0, nx - 1))
    in_specs = [
        pl.BlockSpec((tm, d), lambda i, j: (i, 0)),
        pl.BlockSpec((1, d), lambda i, j: (0, 0)),
        pl.BlockSpec((d, tn), lambda i, j: (0, j)),
        pl.BlockSpec((d, LANES), lambda i, j: (0, 0)),
        pl.BlockSpec((1, LANES), lambda i, j: (0, 0)),
        pl.BlockSpec((width, tn), xcol),
        pl.BlockSpec((1, tn), xcol),
    ]
    args = [x, g, w.astype(BF), wdt, dtb, cw, cb.reshape(1, cch)]
    if stepwise:
        in_specs.append(pl.BlockSpec((tm, tn), lambda i, j: (i, jnp.clip(j - nz, 0, nx - 1))))
        args.append(state)
        us_shape = jax.ShapeDtypeStruct((keep, nx, m // seq, tn), F32)
        us_spec = pl.BlockSpec((keep, None, tm // seq, tn), lambda i, j: (0, jnp.clip(j - nz, 0, nx - 1), i, 0))
        aux = pltpu.VMEM((tm, LANES), F32)
    else:
        us_shape = jax.ShapeDtypeStruct((m // tm, nx, SUBLANES, tn), F32)
        us_spec = pl.BlockSpec((None, None, SUBLANES, tn), lambda i, j: (i, jnp.clip(j - nz, 0, nx - 1), 0, 0))
        aux = pltpu.VMEM((nx, SUBLANES, tn), F32)
    z, xbc, dt, us = pl.pallas_call(
        functools.partial(_ssm_in_body, nz=nz, stepwise=stepwise, seq=seq),
        grid=(m // tm, nz + nx),
        in_specs=in_specs,
        out_specs=[
            pl.BlockSpec((tm, tn), lambda i, j: (i, jnp.clip(j, 0, nz - 1))),
            pl.BlockSpec((tm, tn), lambda i, j: (i, jnp.clip(j - nz, 0, nx - 1))),
            pl.BlockSpec((tm, LANES), lambda i, j: (i, 0)),
            us_spec,
        ],
        out_shape=[
            jax.ShapeDtypeStruct((m, di), F32),
            jax.ShapeDtypeStruct((m, cch), F32),
            jax.ShapeDtypeStruct((m, LANES), F32),
            us_shape,
        ],
        scratch_shapes=[pltpu.VMEM((tm, d), BF), aux],
        compiler_params=_cparams("arbitrary", "arbitrary"),
        name="ssm_in",
    )(*args)
    if stepwise:
        conv_state = jnp.transpose(us, (2, 0, 1, 3)).reshape(m // seq, keep, cch)
    else:
        conv_state = jnp.swapaxes(us[m // tm - 1], 0, 1).reshape(SUBLANES, cch)[None, SUBLANES - keep:]
    return z, xbc, dt, conv_state


def _gate_norm(y, z, ng):
    yz = y * _silu(z)
    return yz * lax.rsqrt(jnp.mean(yz * yz, axis=-1, keepdims=True) + EPS) * ng


def _ssd_seq_body(xbc_ref, dt_ref, z_ref, alog_ref, dsk_ref, ng_ref, y_ref, hout_ref, ht_ref,
                  *, nheads, groups, hd, ns):
    c = pl.program_id(0)
    cs = dt_ref.shape[0]
    hpg = nheads // groups
    gp = hpg * hd
    di = nheads * hd

    @pl.when(c == 0)
    def _():
        ht_ref[...] = jnp.zeros_like(ht_ref)

    dt = dt_ref[...]
    a_neg = jnp.where(_iota((1, LANES), 1) < nheads, -jnp.exp(alog_ref[...]), 0.0)
    acum = _sel_left(_tri(cs), dt * a_neg)
    acum_t = acum.T
    dt_t = dt.T
    tot = acum[cs - 1:cs, :]
    ex = (_iota((LANES, di), 1) // hd == _iota((LANES, di), 0)).astype(BF)
    coefx = _sel_right(jnp.exp(tot - acum) * dt, ex, parts=2)
    cdx = _sel_right(jnp.broadcast_to(jnp.exp(tot), (SUBLANES, LANES)), ex, parts=2)[0:1, :]
    causal = _iota((cs, cs), 1) <= _iota((cs, cs), 0)

    for g in range(groups):
        xs = xbc_ref[:, g * gp:(g + 1) * gp]
        bm = xbc_ref[:, di + g * ns:di + (g + 1) * ns]
        cm = xbc_ref[:, di + (groups + g) * ns:di + (groups + g + 1) * ns]
        cb = _dot_nt(cm.astype(BF), bm.astype(BF))
        ht = ht_ref[g]
        htb = ht.astype(BF)
        xb = xs.astype(BF)
        ys = []
        for r in range(hpg):
            hh = g * hpg + r
            acol = acum[:, hh:hh + 1]
            seg = jnp.broadcast_to(acol, (cs, cs)) - acum_t[hh:hh + 1, :]
            mp = cb * jnp.exp(jnp.where(causal, seg, NEG_INF)) * dt_t[hh:hh + 1, :]
            csc = cm * jnp.exp(jnp.broadcast_to(acol, (cs, ns)))
            lhs = jnp.concatenate([mp, csc], axis=1).astype(BF)
            rhs = jnp.concatenate([xb[:, r * hd:(r + 1) * hd], htb[:, r * hd:(r + 1) * hd]], axis=0)
            ys.append(_dot(lhs, rhs))
        yg = jnp.concatenate(ys, axis=1) + dsk_ref[:, g * gp:(g + 1) * gp] * xs
        y_ref[:, g * gp:(g + 1) * gp] = _gate_norm(
            yg, z_ref[:, g * gp:(g + 1) * gp], ng_ref[:, g * gp:(g + 1) * gp]).astype(BF)
        wg = (coefx[:, g * gp:(g + 1) * gp] * xs).astype(BF)
        ht_ref[g] = ht * cdx[:, g * gp:(g + 1) * gp] + _dot(bm.T.astype(BF), wg)

    @pl.when(c == pl.num_programs(0) - 1)
    def _():
        hout_ref[...] = ht_ref[...]


def _ssd_seq(xbc, dt, z, alog, dsk, ng, nheads, groups, hd, ns, chunk):
    l, cch = xbc.shape
    di = nheads * hd
    gp = di // groups
    return pl.pallas_call(
        functools.partial(_ssd_seq_body, nheads=nheads, groups=groups, hd=hd, ns=ns),
        grid=(l // chunk,),
        in_specs=[
            pl.BlockSpec((chunk, cch), lambda c: (c, 0)),
            pl.BlockSpec((chunk, LANES), lambda c: (c, 0)),
            pl.BlockSpec((chunk, di), lambda c: (c, 0)),
            pl.BlockSpec((1, LANES), lambda c: (0, 0)),
            pl.BlockSpec((1, di), lambda c: (0, 0)),
            pl.BlockSpec((1, di), lambda c: (0, 0)),
        ],
        out_specs=[pl.BlockSpec((chunk, di), lambda c: (c, 0)),
                   pl.BlockSpec((groups, ns, gp), lambda c: (0, 0, 0))],
        out_shape=[jax.ShapeDtypeStruct((l, di), BF),
                   jax.ShapeDtypeStruct((groups, ns, gp), F32)],
        scratch_shapes=[pltpu.VMEM((groups, ns, gp), F32)],
        compiler_params=_cparams("arbitrary"),
        name="ssd_seq",
    )(xbc, dt, z, alog, dsk, ng)


def _ssd_step_body(xbc_ref, dt_ref, z_ref, alog_ref, alogc_ref, dsk_ref, ng_ref, h0_ref, y_ref, hn_ref,
                   *, nheads, groups, hd, ns):
    t = dt_ref.shape[0]
    hpg = nheads // groups
    gp = hpg * hd
    di = nheads * hd
    gn = groups * ns

    dt = dt_ref[...]
    a_neg = jnp.where(_iota((1, LANES), 1) < nheads, -jnp.exp(alog_ref[...]), 0.0)
    rt = _iota((t, LANES), 0)
    acum = dt * a_neg
    sh = 1
    while sh < t:
        acum = acum + jnp.where(rt >= sh, pltpu.roll(acum, sh, 0), 0.0)
        sh *= 2
    tot = acum[t - 1:t, :]
    coef = jnp.exp(tot - acum) * dt
    eac = jnp.exp(acum)

    xs = xbc_ref[:, 0:di]
    bm = xbc_ref[:, di:di + gn]
    cm = xbc_ref[:, di + gn:di + 2 * gn]

    prods = jnp.concatenate([cm * bm[s:s + 1, :] for s in range(t)], axis=0)
    rsel = ((_iota((gn, LANES), 1) // hpg == _iota((gn, LANES), 0) // ns)
            & (_iota((gn, LANES), 1) < nheads)).astype(BF)
    cbe = _dot(prods.astype(BF), rsel)
    acl = jnp.concatenate([acum] * t, axis=0)
    acs = jnp.concatenate([jnp.broadcast_to(acum[s:s + 1, :], (t, LANES)) for s in range(t)], axis=0)
    dts = jnp.concatenate([jnp.broadcast_to(dt[s:s + 1, :], (t, LANES)) for s in range(t)], axis=0)
    rr = _iota((t * t, LANES), 0)
    ms = cbe * jnp.exp(jnp.where(rr % t >= rr // t, acl - acs, NEG_INF)) * dts
    ex = (_iota((LANES, di), 1) // hd == _iota((LANES, di), 0)).astype(BF)
    big = _sel_right(jnp.concatenate([ms, eac, coef], axis=0), ex, parts=2)
    y = big[0:t, :] * xs[0:1, :]
    for s in range(1, t):
        y = y + big[s * t:(s + 1) * t, :] * xs[s:s + 1, :]
    eacx = big[t * t:t * t + t, :]
    coefx = big[t * t + t:, :]

    dt_t = jnp.concatenate([dt, jnp.zeros((LANES - t, LANES), F32)], axis=0).T
    a_col = -jnp.exp(alogc_ref[...])
    cd_col = jnp.exp(jnp.sum(dt_t * a_col, axis=1, keepdims=True))

    for g in range(groups):
        gs = slice(g * gp, (g + 1) * gp)
        h0 = h0_ref[gs, :]
        yoff = _dot_nt(cm[:, g * ns:(g + 1) * ns].astype(BF), h0.astype(BF))
        yg = y[:, gs] + yoff * eacx[:, gs] + dsk_ref[:, gs] * xs[:, gs]
        y_ref[:, gs] = _gate_norm(yg, z_ref[:, gs], ng_ref[:, gs]).astype(BF)
        wg = (coefx[:, gs] * xs[:, gs]).astype(BF)
        wpad = jnp.concatenate([wg.astype(F32), jnp.zeros((LANES - t, gp), F32)], axis=0)
        wt = jnp.concatenate([wpad[:, k * LANES:(k + 1) * LANES].T for k in range(gp // LANES)], axis=0)
        bpad = jnp.concatenate([bm[:, g * ns:(g + 1) * ns], jnp.zeros((LANES - t, ns), F32)], axis=0)
        upd = _dot(wt.astype(BF), bpad.astype(BF))
        for r in range(hpg):
            hh = g * hpg + r
            rs = slice(r * hd, (r + 1) * hd)
            hn_ref[g * gp + r * hd:g * gp + (r + 1) * hd, :] = (
                h0[rs, :] * jnp.broadcast_to(cd_col[hh:hh + 1, :], (hd, ns)) + upd[rs, :])


def _ssd_step(xbc, dt, z, alog, alogc, dsk, ng, h0, nheads, groups, hd, ns):
    nb = h0.shape[0]
    m, cch = xbc.shape
    t = m // nb
    di = nheads * hd
    return pl.pallas_call(
        functools.partial(_ssd_step_body, nheads=nheads, groups=groups, hd=hd, ns=ns),
        grid=(nb,),
        in_specs=[
            pl.BlockSpec((t, cch), lambda b: (b, 0)),
            pl.BlockSpec((t, LANES), lambda b: (b, 0)),
            pl.BlockSpec((t, di), lambda b: (b, 0)),
            pl.BlockSpec((1, LANES), lambda b: (0, 0)),
            pl.BlockSpec((LANES, 1), lambda b: (0, 0)),
            pl.BlockSpec((1, di), lambda b: (0, 0)),
            pl.BlockSpec((1, di), lambda b: (0, 0)),
            pl.BlockSpec((None, di, ns), lambda b: (b, 0, 0)),
        ],
        out_specs=[pl.BlockSpec((t, di), lambda b: (b, 0)),
                   pl.BlockSpec((None, di, ns), lambda b: (b, 0, 0))],
        out_shape=[jax.ShapeDtypeStruct((m, di), BF),
                   jax.ShapeDtypeStruct((nb, di, ns), F32)],
        compiler_params=_cparams("parallel"),
        name="ssd_step",
    )(xbc, dt, z, alog, alogc, dsk, ng, h0)


def _pad_lanes(a):
    return jnp.pad(a, [(0, 0)] * (a.ndim - 1) + [(0, LANES - a.shape[-1])])


def _trunk(x, nb, page_table, cache_k, cache_v, cache_lf, conf_prev, mconv_prev, ssm_prev, ffn_prev, w):
    _, t, d = x.shape
    m = nb * t
    stepwise = conf_prev is not None
    nh, dh = cache_k.shape[-2:]
    fw = nh * dh
    x = x.reshape(m, d)

    w_in = w["att_w_in"][0]
    cc = w["conf_dw_w"].shape[2]
    cw = w["conf_dw_w"].shape[1]
    w5 = jnp.concatenate([w_in[:, :3 * fw], w_in[:, 3 * fw + nh:]], axis=1)
    wf = _pad_lanes(w_in[:, 3 * fw:3 * fw + nh]).astype(BF)
    bfp = _pad_lanes(w["att_b_f"][0].reshape(1, nh))
    q, k, v, u, lf = _even_in(x, w["ln_mix"][0].reshape(1, d), w5, wf, bfp, fw, cc)
    logf = lf[:, :nh]
    if stepwise:
        npool, page = cache_k.shape[1:3]
        att = _fox_sample(q, k, v, lf,
                          cache_k[0].reshape(npool, page * nh, dh), cache_v[0].reshape(npool, page * nh, dh),
                          cache_lf[0].reshape(npool, page * nh // LANES, LANES), page_table, nh, dh)
        xp = jnp.concatenate([conf_prev[0], u.reshape(nb, t, cc)], axis=1)
        conf_state = xp[:, t:]
        cv = _conv_step(xp, w["conf_dw_w"][0], w["conf_dw_b"][0]).reshape(m, cc)
    else:
        cq, ckt = _fox_cumsum(lf, nh)
        att = _fox_prompt(q, k, v, cq, ckt.reshape(nh, 1, m), nh, dh)
        conf_state = u[m - (cw - 1):].reshape(1, cw - 1, cc)
        cv = _conv_seq(u, w["conf_dw_w"][0], w["conf_dw_b"][0])
    w_out = w["att_w_out"][0]
    x = _out_even(x, att, cv, w["conf_ln_g"][0], w["conf_ln_b"][0], w_out[:fw], w_out[fw:])

    ffn_states = []

    def ffn(x, i, final_g):
        if stepwise:
            two_f = ffn_prev.shape[-1]
            st = jnp.zeros((nb, t, two_f), F32).at[:, t - 2:].set(ffn_prev[i]).reshape(m, two_f)
        else:
            st = None
        x, us = _ffn(x, w["ln_ffn"][i], w["ffn_w_up"], w["ffn_dw_w"][i], w["ffn_dw_b"][i],
                     w["ffn_w_down"], i, st, final_g, t)
        ffn_states.append(us)
        return x

    x = ffn(x, 0, None)

    nheads, hd, ns = w["ssm_state_shape"]
    di = nheads * hd
    cch = w["ssm_conv_w"].shape[2]
    groups = (cch - di) // (2 * ns)
    w_in = w["ssm_w_in"][0]
    wdt = _pad_lanes(w_in[:, di + cch:]).astype(BF)
    dtb = _pad_lanes(w["ssm_dt_bias"][0].reshape(1, nheads))
    sw = w["ssm_conv_w"].shape[1]
    if stepwise:
        st = jnp.zeros((nb, t, cch), F32).at[:, t - (sw - 1):].set(mconv_prev[0]).reshape(m, cch)
    else:
        st = None
    z, xbc, dt, mconv_state = _ssm_in(x, w["ln_mix"][1].reshape(1, d), w_in, wdt, dtb,
                                      w["ssm_conv_w"][0], w["ssm_conv_b"][0], st, di, cch, t)
    alog = _pad_lanes(w["ssm_a_log"][0].reshape(1, nheads))
    dsk = jnp.repeat(w["ssm_d"][0], hd).reshape(1, di)
    ng = w["ssm_norm_g"][0].reshape(1, di)
    if stepwise:
        yn, hn = _ssd_step(xbc, dt, z, alog, alog.reshape(LANES, 1), dsk, ng,
                           ssm_prev[0].reshape(nb, di, ns), nheads, groups, hd, ns)
        ssm_state = hn.reshape(nb, nheads, hd, ns)
    else:
        chunk = _tile(m, 128)
        yn, hout = _ssd_seq(xbc, dt, z, alog, dsk, ng, nheads, groups, hd, ns, chunk)
        hpg = nheads // groups
        ssm_state = jnp.transpose(hout.reshape(groups, ns, hpg, hd), (0, 2, 3, 1)).reshape(1, nheads, hd, ns)
    x = _mm_res(x, yn, w["ssm_w_out"][0])
    x = ffn(x, 1, w["ln_final"])

    return (x.reshape(nb, t, d),
            k.reshape(1, nb, t, nh, dh), v.reshape(1, nb, t, nh, dh), logf.reshape(1, nb, t, nh),
            conf_state[None], mconv_state[None], ssm_state[None], jnp.stack(ffn_states))


def kernel(x_prompt, x_sample, cache_k, cache_v, cache_logf, state_conf_conv, state_ssm_conv, state_ssm,
           state_ffn_conv, page_table,
           ln_mix, ln_ffn, ln_final,
           att_w_in, att_b_f, conf_dw_w, conf_dw_b, conf_ln_g, conf_ln_b, att_w_out,
           ssm_w_in, ssm_conv_w, ssm_conv_b, ssm_dt_bias, ssm_a_log, ssm_d, ssm_norm_g, ssm_w_out,
           ffn_w_up, ffn_dw_w, ffn_dw_b, ffn_w_down):
    assert ln_mix.shape[0] == 2 and x_prompt.shape[0] == 1, "two layers, one prompt sequence"
    w = dict(ln_mix=ln_mix, ln_ffn=ln_ffn, ln_final=ln_final,
             att_w_in=att_w_in, att_b_f=att_b_f, conf_dw_w=conf_dw_w, conf_dw_b=conf_dw_b,
             conf_ln_g=conf_ln_g, conf_ln_b=conf_ln_b, att_w_out=att_w_out,
             ssm_w_in=ssm_w_in, ssm_conv_w=ssm_conv_w, ssm_conv_b=ssm_conv_b, ssm_dt_bias=ssm_dt_bias,
             ssm_a_log=ssm_a_log, ssm_d=ssm_d, ssm_norm_g=ssm_norm_g, ssm_w_out=ssm_w_out,
             ffn_w_up=ffn_w_up, ffn_dw_w=ffn_dw_w, ffn_dw_b=ffn_dw_b, ffn_w_down=ffn_w_down,
             ssm_state_shape=state_ssm.shape[2:])
    nb = x_sample.shape[0]
    outs_p = _trunk(x_prompt, 1, None, cache_k, cache_v, cache_logf, None, None, None, None, w)
    outs_s = _trunk(x_sample, nb, page_table, cache_k, cache_v, cache_logf,
                    state_conf_conv, state_ssm_conv, state_ssm, state_ffn_conv, w)
    return (outs_p[0], outs_s[0]) + tuple(outs_p[1:]) + tuple(outs_s[1:])
```

```python
import functools

import jax
import jax.numpy as jnp
from jax import lax
from jax.experimental import pallas as pl
from jax.experimental.pallas import tpu as pltpu

EPS = 1e-6
BF = jnp.bfloat16
F32 = jnp.float32
LANES = 128
SUBLANES = 8
VMEM_LIMIT = 56 * 1024 * 1024
NEG_INF = float("-inf")
PROJ_ROWS = 1024
FFN_ROWS = 512


def _cparams(*sem):
    return pltpu.CompilerParams(dimension_semantics=sem, vmem_limit_bytes=VMEM_LIMIT)


def _tile(n, pref):
    t = pref
    while t >= LANES and t % LANES == 0:
        if n % t == 0:
            return t
        t //= 2
    return n


def _iota(shape, dim):
    return lax.broadcasted_iota(jnp.int32, shape, dim)


def _split3(x):
    hi = x.astype(BF)
    r1 = x - hi.astype(F32)
    mid = r1.astype(BF)
    lo = (r1 - mid.astype(F32)).astype(BF)
    return hi, mid, lo


def _dot(a, b):
    return jnp.dot(a, b, preferred_element_type=F32)


def _dot_nt(a, b):
    return lax.dot_general(a, b, (((1,), (1,)), ((), ())), preferred_element_type=F32)


def _sel_left(sel, x, parts=3):
    ps = _split3(x)[:parts]
    out = _dot(sel, ps[0])
    for p in ps[1:]:
        out = out + _dot(sel, p)
    return out


def _sel_right(x, sel, parts=3):
    ps = _split3(x)[:parts]
    out = _dot(ps[0], sel)
    for p in ps[1:]:
        out = out + _dot(p, sel)
    return out


def _log_sigmoid(x):
    return jnp.minimum(x, 0.0) - jnp.log(1.0 + jnp.exp(-jnp.abs(x)))


def _softplus(x):
    return jnp.maximum(x, 0.0) + jnp.log(1.0 + jnp.exp(-jnp.abs(x)))


def _silu(x):
    return x * jax.nn.sigmoid(x)


def _rms(x, g):
    return x * lax.rsqrt(jnp.mean(x * x, axis=-1, keepdims=True) + EPS) * g


def _tri(n):
    return (_iota((n, n), 1) <= _iota((n, n), 0)).astype(BF)


def _even_in_body(x_ref, g_ref, w_ref, wf_ref, bf_ref,
                  q_ref, k_ref, v_ref, u_ref, lf_ref, xn_ref, a_ref, *, nf, nc):
    j = pl.program_id(1)

    @pl.when(j == 0)
    def _():
        xn = _rms(x_ref[...], g_ref[...]).astype(BF)
        xn_ref[...] = xn
        lf_ref[...] = _log_sigmoid(_dot(xn, wf_ref[...]) + bf_ref[...])

    z = _dot(xn_ref[...], w_ref[...])

    @pl.when(j < nf)
    def _():
        q_ref[...] = z

    @pl.when((j >= nf) & (j < 2 * nf))
    def _():
        k_ref[...] = z

    @pl.when((j >= 2 * nf) & (j < 3 * nf))
    def _():
        v_ref[...] = z

    @pl.when((j >= 3 * nf) & ((j - 3 * nf) % 2 == 0))
    def _():
        a_ref[...] = z

    @pl.when((j >= 3 * nf) & ((j - 3 * nf) % 2 == 1))
    def _():
        u_ref[...] = a_ref[...] * jax.nn.sigmoid(z)


def _even_in(x, g, w5, wf, bfp, fw, cc):
    m, d = x.shape
    tm = _tile(m, PROJ_ROWS)
    tn = min(_tile(fw, 512), _tile(cc, 512))
    nf, nc = fw // tn, cc // tn
    nj = 3 * nf + 2 * nc

    def w_map(i, j):
        jj = j - 3 * nf
        return (0, jnp.where(j < 3 * nf, j, 3 * nf + (jj % 2) * nc + jj // 2))

    def seg_map(base, n, div=1):
        return lambda i, j: (i, jnp.clip((j - base) // div, 0, n - 1))

    return pl.pallas_call(
        functools.partial(_even_in_body, nf=nf, nc=nc),
        grid=(m // tm, nj),
        in_specs=[
            pl.BlockSpec((tm, d), lambda i, j: (i, 0)),
            pl.BlockSpec((1, d), lambda i, j: (0, 0)),
            pl.BlockSpec((d, tn), w_map),
            pl.BlockSpec((d, LANES), lambda i, j: (0, 0)),
            pl.BlockSpec((1, LANES), lambda i, j: (0, 0)),
        ],
        out_specs=[
            pl.BlockSpec((tm, tn), seg_map(0, nf)),
            pl.BlockSpec((tm, tn), seg_map(nf, nf)),
            pl.BlockSpec((tm, tn), seg_map(2 * nf, nf)),
            pl.BlockSpec((tm, tn), seg_map(3 * nf, nc, 2)),
            pl.BlockSpec((tm, LANES), lambda i, j: (i, 0)),
        ],
        out_shape=[
            jax.ShapeDtypeStruct((m, fw), F32),
            jax.ShapeDtypeStruct((m, fw), F32),
            jax.ShapeDtypeStruct((m, fw), F32),
            jax.ShapeDtypeStruct((m, cc), F32),
            jax.ShapeDtypeStruct((m, LANES), F32),
        ],
        scratch_shapes=[pltpu.VMEM((tm, d), BF), pltpu.VMEM((tm, tn), F32)],
        compiler_params=_cparams("parallel", "arbitrary"),
        name="even_in",
    )(x, g, w5.astype(BF), wf, bfp)


def _fox_cumsum_body(lf_ref, cq_ref, ckt_ref, carry_ref, *, nh):
    @pl.when(pl.program_id(0) == 0)
    def _():
        carry_ref[...] = jnp.zeros_like(carry_ref)

    lf = lf_ref[...]
    tb = lf.shape[0]
    cs = _sel_left(_tri(tb), lf) + carry_ref[...]
    carry_ref[...] = cs[tb - 1:tb, :]
    for h in range(nh):
        cq_ref[h] = jnp.broadcast_to(cs[:, h:h + 1], (tb, LANES))
    ckt_ref[...] = cs.T[:nh, :]


def _fox_cumsum(lf, nh):
    m = lf.shape[0]
    tb = _tile(m, 256)
    return pl.pallas_call(
        functools.partial(_fox_cumsum_body, nh=nh),
        grid=(m // tb,),
        in_specs=[pl.BlockSpec((tb, LANES), lambda i: (i, 0))],
        out_specs=[pl.BlockSpec((nh, tb, LANES), lambda i: (0, i, 0)),
                   pl.BlockSpec((nh, tb), lambda i: (0, i))],
        out_shape=[jax.ShapeDtypeStruct((nh, m, LANES), F32),
                   jax.ShapeDtypeStruct((nh, m), F32)],
        scratch_shapes=[pltpu.VMEM((1, LANES), F32)],
        compiler_params=_cparams("arbitrary"),
        name="fox_cumsum",
    )(lf)


def _fox_prompt_body(q_ref, k_ref, v_ref, cq_ref, ck_ref, o_ref, kb_ref, vb_ref, m_ref, acc_ref, s_ref, *, scale):
    qi = pl.program_id(1)
    t, dh = q_ref.shape
    log2e = 1.4426950408889634

    @pl.when(qi == 0)
    def _():
        kb_ref[...] = k_ref[...].astype(BF)
        vb_ref[:, 0:dh] = v_ref[...].astype(BF)
        vb_ref[:, dh:] = (_iota((k_ref.shape[0], dh), 1) == 0).astype(BF)

    qs = (q_ref[...] * (scale * log2e)).astype(BF)
    m_ref[...] = jnp.full_like(m_ref, NEG_INF)
    acc_ref[...] = jnp.zeros_like(acc_ref)
    reps = t // LANES
    rc = min(t, 128)

    def scores(kj, slot):
        off = pl.multiple_of(kj * t, t)
        s_ref[slot] = _dot_nt(qs, kb_ref[pl.ds(off, t), :]) - ck_ref[kj] * log2e

    def update(kj, slot, masked):
        off = pl.multiple_of(kj * t, t)
        for r in range(t // rc):
            rows = slice(r * rc, (r + 1) * rc)
            s = s_ref[slot, rows, :]
            if masked:
                s = jnp.where(_iota((rc, t), 1) <= r * rc + _iota((rc, t), 0), s, NEG_INF)
            cq = cq_ref[rows, :] * log2e
            m_prev = m_ref[rows, :]
            m_new = jnp.maximum(m_prev, jnp.max(s, axis=1, keepdims=True) + cq)
            shift = m_new - cq
            p = jnp.exp2(s - jnp.concatenate([shift] * reps, axis=1))
            alpha = jnp.exp2(m_prev - m_new)
            acc_ref[rows, :] = (jnp.concatenate([alpha] * (2 * dh // LANES), axis=1) * acc_ref[rows, :]
                                + _dot(p.astype(BF), vb_ref[pl.ds(off, t), :]))
            m_ref[rows, :] = m_new

    scores(0, 0)

    def body(k2, carry):
        update(2 * k2, 0, False)
        scores(2 * k2 + 1, 1)
        update(2 * k2 + 1, 1, False)
        scores(2 * k2 + 2, 0)
        return carry

    lax.fori_loop(0, qi // 2, body, 0)

    @pl.when(qi % 2 == 0)
    def _():
        update(qi, 0, True)

    @pl.when(qi % 2 == 1)
    def _():
        update(qi - 1, 0, False)
        scores(qi, 1)
        update(qi, 1, True)

    o_ref[...] = acc_ref[:, 0:dh] / acc_ref[:, dh:dh + 1]


def _fox_prompt(q, k, v, cq, ck, nh, dh):
    m = q.shape[0]
    t = _tile(m, 1024)
    n = m // t
    return pl.pallas_call(
        functools.partial(_fox_prompt_body, scale=dh ** -0.5),
        grid=(nh, n),
        in_specs=[
            pl.BlockSpec((t, dh), lambda h, i: (i, h)),
            pl.BlockSpec((m, dh), lambda h, i: (0, h)),
            pl.BlockSpec((m, dh), lambda h, i: (0, h)),
            pl.BlockSpec((None, t, LANES), lambda h, i: (h, i, 0)),
            pl.BlockSpec((None, n, 1, t), lambda h, i: (h, 0, 0, 0)),
        ],
        out_specs=pl.BlockSpec((t, dh), lambda h, i: (i, h)),
        out_shape=jax.ShapeDtypeStruct((m, nh * dh), F32),
        scratch_shapes=[pltpu.VMEM((m, dh), BF), pltpu.VMEM((m, 2 * dh), BF),
                        pltpu.VMEM((t, LANES), F32), pltpu.VMEM((t, 2 * dh), F32),
                        pltpu.VMEM((2, t, t), F32)],
        compiler_params=_cparams("parallel", "arbitrary"),
        name="fox_prompt",
    )(q, k, v, cq, ck.reshape(nh, n, 1, t))


def _fox_sample_body(pt_ref, q_ref, kn_ref, vn_ref, lfn_ref, *rest, npg, nh, dh, scale):
    k_refs = rest[:npg]
    v_refs = rest[npg:2 * npg]
    lf_refs = rest[2 * npg:3 * npg]
    o_ref, s_ref, acc_ref = rest[3 * npg:]
    t, fw = q_ref.shape
    page = k_refs[0].shape[0] // nh
    ht = nh * t
    ppr = LANES // nh

    def load_page(ref):
        return jnp.concatenate([ref[pl.ds(h, page, stride=nh), :] for h in range(nh)], axis=1)

    q = q_ref[...]
    qt = jnp.concatenate([q] * nh + [jnp.zeros((LANES - ht, fw), F32)], axis=0)
    qbd = jnp.where(_iota((LANES, fw), 0) // t == _iota((LANES, fw), 1) // dh, qt, 0.0).astype(BF)

    r2 = _iota((LANES, LANES), 0)
    c2 = _iota((LANES, LANES), 1)
    e_past = ((r2 % nh == c2 // t) & (c2 < ht)).astype(BF)
    e_new = ((r2 == c2 // t) & (c2 < ht)).astype(BF)
    tri = _tri(page)
    rp = _iota((page, LANES), 0)
    cp = _iota((page, LANES), 1)

    carry = jnp.zeros((1, LANES), F32)
    for p in range(npg):
        lfp = lf_refs[p][...]
        b = jnp.concatenate([jnp.broadcast_to(lfp[r:r + 1, :], (ppr, LANES))
                             for r in range(page // ppr)], axis=0)
        b = jnp.where(cp // nh == rp % ppr, b, 0.0)
        clocal = _sel_left(tri, _sel_right(b, e_past))
        st = _dot_nt(load_page(k_refs[p]).astype(BF), qbd) * scale
        s_ref[p * page:(p + 1) * page, :] = st - (clocal + carry)
        carry = carry + clocal[page - 1:page, :]

    y = _sel_right(lfn_ref[...], e_new)
    rt = _iota((t, LANES), 0)
    sh = 1
    while sh < t:
        y = y + jnp.where(rt >= sh, pltpu.roll(y, sh, 0), 0.0)
        sh *= 2
    cn = carry + y
    cq = jnp.sum(jnp.where(rt == _iota((t, LANES), 1) % t, cn, 0.0), axis=0, keepdims=True)
    knp = jnp.concatenate([kn_ref[...], jnp.zeros((page - t, fw), F32)], axis=0)
    stn = _dot_nt(knp.astype(BF), qbd) * scale
    cnp = jnp.concatenate([cn, jnp.zeros((page - t, LANES), F32)], axis=0)
    s_ref[npg * page:, :] = jnp.where((rp < t) & (rp <= cp % t), stn - cnp, NEG_INF)

    mx = jnp.max(s_ref[0:page, :], axis=0, keepdims=True)
    for p in range(1, npg + 1):
        mx = jnp.maximum(mx, jnp.max(s_ref[p * page:(p + 1) * page, :], axis=0, keepdims=True))
    shift = cq - (mx + cq)

    acc_ref[...] = jnp.zeros_like(acc_ref)
    lsum = jnp.zeros((LANES, 1), F32)
    for p in range(npg + 1):
        pt = jnp.exp(s_ref[p * page:(p + 1) * page, :] + shift).T
        lsum = lsum + jnp.sum(pt, axis=1, keepdims=True)
        if p < npg:
            vp = load_page(v_refs[p])
        else:
            vp = jnp.concatenate([vn_ref[...], jnp.zeros((page - t, fw), F32)], axis=0)
        acc_ref[...] += _dot(pt[:ht, :].astype(BF), vp.astype(BF))

    for h in range(nh):
        o_ref[:, h * dh:(h + 1) * dh] = (acc_ref[h * t:(h + 1) * t, h * dh:(h + 1) * dh]
                                          / lsum[h * t:(h + 1) * t, :])


def _fox_sample(q, kn, vn, lfn, ck, cv, clf, page_table, nh, dh):
    nb, npg = page_table.shape
    m, fw = q.shape
    t = m // nb
    prows = ck.shape[1]
    page = prows // nh
    rows = page * nh // LANES

    def pg_map(p):
        return lambda b, pt: (pt[b * npg + p], 0, 0)

    seq = pl.BlockSpec((t, fw), lambda b, pt: (b, 0))
    in_specs = [seq, seq, seq, pl.BlockSpec((t, LANES), lambda b, pt: (b, 0))]
    in_specs += [pl.BlockSpec((None, prows, dh), pg_map(p)) for p in range(npg)]
    in_specs += [pl.BlockSpec((None, prows, dh), pg_map(p)) for p in range(npg)]
    in_specs += [pl.BlockSpec((None, rows, LANES), pg_map(p)) for p in range(npg)]
    return pl.pallas_call(
        functools.partial(_fox_sample_body, npg=npg, nh=nh, dh=dh, scale=dh ** -0.5),
        grid_spec=pltpu.PrefetchScalarGridSpec(
            num_scalar_prefetch=1,
            grid=(nb,),
            in_specs=in_specs,
            out_specs=pl.BlockSpec((t, fw), lambda b, pt: (b, 0)),
            scratch_shapes=[pltpu.VMEM(((npg + 1) * page, LANES), F32),
                            pltpu.VMEM((nh * t, fw), F32)],
        ),
        out_shape=jax.ShapeDtypeStruct((m, fw), F32),
        compiler_params=_cparams("parallel"),
        name="fox_sample",
    )(page_table.reshape(-1), q, kn, vn, lfn, *([ck] * npg), *([cv] * npg), *([clf] * npg))


def _conv_seq_body(xm_ref, xh_ref, w_ref, b_ref, o_ref, xf_ref, xs_ref, *, width):
    tl = xm_ref.shape[0]
    hb = xh_ref.shape[0]
    n = hb + tl
    xf_ref[0:hb, :] = jnp.where(pl.program_id(0) == 0, 0.0, xh_ref[...])
    xf_ref[hb:n, :] = xm_ref[...]
    xf_ref[n:, :] = jnp.zeros((SUBLANES, xf_ref.shape[1]), F32)
    for b in range(SUBLANES):
        xs_ref[b] = xf_ref[b:b + n, :]
    base = hb - (width - 1)
    rc = min(tl, 32)
    for c in range(tl // rc):
        acc = b_ref[...]
        for j in range(width):
            a, b = divmod(base + c * rc + j, SUBLANES)
            acc = acc + w_ref[j:j + 1, :] * xs_ref[b, a * SUBLANES:a * SUBLANES + rc, :]
        o_ref[c * rc:(c + 1) * rc, :] = acc


def _conv_seq(x, w, b):
    l, c = x.shape
    width = w.shape[0]
    hb = -(-(width - 1) // SUBLANES) * SUBLANES
    tl = _tile(l, 512)
    tc = _tile(c, 256)
    r = tl // hb
    return pl.pallas_call(
        functools.partial(_conv_seq_body, width=width),
        grid=(l // tl, c // tc),
        in_specs=[
            pl.BlockSpec((tl, tc), lambda i, j: (i, j)),
            pl.BlockSpec((hb, tc), lambda i, j: (jnp.maximum(i * r - 1, 0), j)),
            pl.BlockSpec((width, tc), lambda i, j: (0, j)),
            pl.BlockSpec((1, tc), lambda i, j: (0, j)),
        ],
        out_specs=pl.BlockSpec((tl, tc), lambda i, j: (i, j)),
        out_shape=jax.ShapeDtypeStruct((l, c), F32),
        scratch_shapes=[pltpu.VMEM((hb + tl + SUBLANES, tc), F32),
                        pltpu.VMEM((SUBLANES, hb + tl, tc), F32)],
        compiler_params=_cparams("parallel", "parallel"),
        name="conv_seq",
    )(x, x, w, b.reshape(1, c))


def _conv_step_body(xp_ref, w_ref, b_ref, o_ref, *, width):
    t = o_ref.shape[1]
    acc = b_ref[...] + w_ref[0:1, :] * xp_ref[:, 0:t, :]
    for j in range(1, width):
        acc = acc + w_ref[j:j + 1, :] * xp_ref[:, j:j + t, :]
    o_ref[...] = acc


def _conv_step(xp, w, b):
    nb, rows, c = xp.shape
    width = w.shape[0]
    t = rows - (width - 1)
    bb = 16 if nb % 16 == 0 else nb
    tc = _tile(c, 512)
    return pl.pallas_call(
        functools.partial(_conv_step_body, width=width),
        grid=(nb // bb, c // tc),
        in_specs=[
            pl.BlockSpec((bb, rows, tc), lambda i, j: (i, 0, j)),
            pl.BlockSpec((width, tc), lambda i, j: (0, j)),
            pl.BlockSpec((1, tc), lambda i, j: (0, j)),
        ],
        out_specs=pl.BlockSpec((bb, t, tc), lambda i, j: (i, 0, j)),
        out_shape=jax.ShapeDtypeStruct((nb, t, c), F32),
        compiler_params=_cparams("parallel", "parallel"),
        name="conv_step",
    )(xp, w, b.reshape(1, c))


def _out_even_body(x_ref, att_ref, cv_ref, lg_ref, lb_ref, wa_ref, wc_ref, o_ref, ab_ref, cb_ref):
    @pl.when(pl.program_id(1) == 0)
    def _():
        u = cv_ref[...]
        xc = u - jnp.mean(u, axis=-1, keepdims=True)
        var = jnp.mean(xc * xc, axis=-1, keepdims=True)
        cb_ref[...] = _silu(xc * lax.rsqrt(var + EPS) * lg_ref[...] + lb_ref[...]).astype(BF)
        ab_ref[...] = att_ref[...].astype(BF)

    o_ref[...] = x_ref[...] + _dot(ab_ref[...], wa_ref[...]) + _dot(cb_ref[...], wc_ref[...])


def _out_even(x, att, cv, lg, lb, wa, wc):
    m, d = x.shape
    fw, cc = att.shape[1], cv.shape[1]
    tm = _tile(m, PROJ_ROWS)
    tn = _tile(d, 512)
    return pl.pallas_call(
        _out_even_body,
        grid=(m // tm, d // tn),
        in_specs=[
            pl.BlockSpec((tm, tn), lambda i, j: (i, j)),
            pl.BlockSpec((tm, fw), lambda i, j: (i, 0)),
            pl.BlockSpec((tm, cc), lambda i, j: (i, 0)),
            pl.BlockSpec((1, cc), lambda i, j: (0, 0)),
            pl.BlockSpec((1, cc), lambda i, j: (0, 0)),
            pl.BlockSpec((fw, tn), lambda i, j: (0, j)),
            pl.BlockSpec((cc, tn), lambda i, j: (0, j)),
        ],
        out_specs=pl.BlockSpec((tm, tn), lambda i, j: (i, j)),
        out_shape=jax.ShapeDtypeStruct((m, d), F32),
        scratch_shapes=[pltpu.VMEM((tm, fw), BF), pltpu.VMEM((tm, cc), BF)],
        compiler_params=_cparams("parallel", "arbitrary"),
        name="out_even",
    )(x, att, cv, lg.reshape(1, cc), lb.reshape(1, cc), wa.astype(BF), wc.astype(BF))


def _mm_res_body(x_ref, a_ref, w_ref, o_ref):
    o_ref[...] = x_ref[...] + _dot(a_ref[...], w_ref[...])


def _mm_res(x, a, w):
    m, d = x.shape
    kk = a.shape[1]
    tm = _tile(m, PROJ_ROWS)
    tn = _tile(d, 512)
    return pl.pallas_call(
        _mm_res_body,
        grid=(m // tm, d // tn),
        in_specs=[
            pl.BlockSpec((tm, tn), lambda i, j: (i, j)),
            pl.BlockSpec((tm, kk), lambda i, j: (i, 0)),
            pl.BlockSpec((kk, tn), lambda i, j: (0, j)),
        ],
        out_specs=pl.BlockSpec((tm, tn), lambda i, j: (i, j)),
        out_shape=jax.ShapeDtypeStruct((m, d), F32),
        compiler_params=_cparams("parallel", "arbitrary"),
        name="mm_res",
    )(x, a, w.astype(BF))


def _causal_conv(u, w_ref, b_ref, prev, seq):
    tm = u.shape[0]
    width = w_ref.shape[0]
    acc = b_ref[...] + w_ref[width - 1:width, :] * u
    if seq is None:
        row = _iota(prev.shape, 0)
    else:
        row = _iota(u.shape, 0) % seq
        spread = (_iota((tm, tm // seq), 0) // seq == _iota((tm, tm // seq), 1)).astype(BF)
        prev = [_sel_left(spread, p[...]) for p in prev]
    for s in range(1, width):
        r = pltpu.roll(u, s, 0)
        if seq is None:
            head = jnp.where(row < s, pltpu.roll(prev, s, 0), r[:SUBLANES])
            r = jnp.concatenate([head, r[SUBLANES:]], axis=0)
        else:
            for k in range(s):
                r = jnp.where(row == k, prev[width - 1 - s + k], r)
        acc = acc + w_ref[width - 1 - s:width - s, :] * r
    return acc


def _save_seq_tails(u, us_ref, col, ub_ref, seq, keep):
    tm, c = u.shape
    for k in range(c // LANES):
        ub_ref[...] = u[:, k * LANES:(k + 1) * LANES]
        for r in range(keep):
            us_ref[r, :, col + k * LANES:col + (k + 1) * LANES] = (
                ub_ref[pl.ds(seq - keep + r, tm // seq, stride=seq), :])


def _ffn_body(*refs, stepwise, final, seq, nf):
    x_ref, g_ref, wa_ref, wv_ref, dwa_ref, dwv_ref, ba_ref, bv_ref, wd_ref = refs[:9]
    k = 9
    keep = dwa_ref.shape[0] - 1
    if stepwise:
        sa_refs, sv_refs = refs[k:k + keep], refs[k + keep:k + 2 * keep]
        k += 2 * keep
    if final:
        fg_ref = refs[k]
        k += 1
    o_ref, us_ref, xn_ref = refs[k:k + 3]
    k += 3
    i = pl.program_id(0)
    j = pl.program_id(1)
    tm, tf = x_ref.shape[0], wa_ref.shape[1]
    width = dwa_ref.shape[0]

    @pl.when(j == 0)
    def _():
        xn_ref[...] = _rms(x_ref[...], g_ref[...]).astype(BF)
        o_ref[...] = x_ref[...]

    xn = xn_ref[...]
    if stepwise:
        ua = _dot(xn, wa_ref[...])
        uv = _dot(xn, wv_ref[...])
        _save_seq_tails(ua, us_ref, 0, refs[k], seq, width - 1)
        _save_seq_tails(uv, us_ref, tf, refs[k], seq, width - 1)
        a = _causal_conv(ua, dwa_ref, ba_ref, sa_refs, seq)
        v = _causal_conv(uv, dwv_ref, bv_ref, sv_refs, seq)
        o_ref[...] += _dot((_silu(a) * v).astype(BF), wd_ref[...])
    else:
        ua_ref, uv_ref, h_ref, ca_ref, cv_ref = refs[k:k + 5]

        @pl.when(i == 0)
        def _():
            ca_ref[j] = jnp.zeros((SUBLANES, tf), F32)
            cv_ref[j] = jnp.zeros((SUBLANES, tf), F32)

        ua_ref[0:SUBLANES, :] = ca_ref[j]
        uv_ref[0:SUBLANES, :] = cv_ref[j]
        ua_ref[SUBLANES:, :] = _dot(xn, wa_ref[...])
        uv_ref[SUBLANES:, :] = _dot(xn, wv_ref[...])
        ca_ref[j] = ua_ref[tm:, :]
        cv_ref[j] = uv_ref[tm:, :]
        us_ref[:, 0:tf] = ua_ref[tm:, :]
        us_ref[:, tf:2 * tf] = uv_ref[tm:, :]
        rc = min(tm, 16)

        def conv_rows(u_ref, w_ref, b_ref, r0):
            win = u_ref[r0:r0 + SUBLANES + rc, :]
            acc = b_ref[...] + w_ref[width - 1:width, :] * win[SUBLANES:, :]
            for s in range(1, width):
                acc = acc + w_ref[width - 1 - s:width - s, :] * pltpu.roll(win, s, 0)[SUBLANES:, :]
            return acc

        for c in range(tm // rc):
            a = conv_rows(ua_ref, dwa_ref, ba_ref, c * rc)
            v = conv_rows(uv_ref, dwv_ref, bv_ref, c * rc)
            h_ref[c * rc:(c + 1) * rc, :] = (_silu(a) * v).astype(BF)
        o_ref[...] += _dot(h_ref[...], wd_ref[...])

    if final:
        @pl.when(j == nf - 1)
        def _():
            o_ref[...] = _rms(o_ref[...], fg_ref[...])


def _ffn(x, g, w_up, dw_w, dw_b, w_down, layer, state, final_g, seq):
    m, d = x.shape
    f = w_down.shape[1]
    tm = _tile(m, FFN_ROWS)
    tf = _tile(f, 512)
    nf = f // tf
    stepwise = state is not None
    final = final_g is not None
    row_spec = lambda i, j: (i, 0)
    a_col = lambda i, j: (0, j)
    v_col = lambda i, j: (0, j + nf)
    in_specs = [
        pl.BlockSpec((tm, d), row_spec),
        pl.BlockSpec((1, d), lambda i, j: (0, 0)),
        pl.BlockSpec((None, d, tf), lambda i, j: (layer, 0, j)),
        pl.BlockSpec((None, d, tf), lambda i, j: (layer, 0, j + nf)),
        pl.BlockSpec((3, tf), a_col),
        pl.BlockSpec((3, tf), v_col),
        pl.BlockSpec((1, tf), a_col),
        pl.BlockSpec((1, tf), v_col),
        pl.BlockSpec((None, tf, d), lambda i, j: (layer, j, 0)),
    ]
    w_up = w_up.astype(BF)
    w_down = w_down.astype(BF)
    args = [x, g.reshape(1, d), w_up, w_up, dw_w, dw_w, dw_b.reshape(1, 2 * f), dw_b.reshape(1, 2 * f), w_down]
    scratch = [pltpu.VMEM((tm, d), BF)]
    if stepwise:
        rows = [state[:, r, :] for r in range(state.shape[1])]
        in_specs += [pl.BlockSpec((tm // seq, tf), lambda i, j: (i, j))] * len(rows)
        in_specs += [pl.BlockSpec((tm // seq, tf), lambda i, j: (i, j + nf))] * len(rows)
        args += rows + rows
        us_shape = jax.ShapeDtypeStruct((2, nf, m // seq, 2 * tf), F32)
        us_spec = pl.BlockSpec((2, None, tm // seq, 2 * tf), lambda i, j: (0, j, i, 0))
        scratch += [pltpu.VMEM((tm, LANES), F32)]
    else:
        us_shape = jax.ShapeDtypeStruct((m // tm, nf, SUBLANES, 2 * tf), F32)
        us_spec = pl.BlockSpec((None, None, SUBLANES, 2 * tf), lambda i, j: (i, j, 0, 0))
        scratch += [pltpu.VMEM((SUBLANES + tm, tf), F32), pltpu.VMEM((SUBLANES + tm, tf), F32),
                    pltpu.VMEM((tm, tf), BF),
                    pltpu.VMEM((nf, SUBLANES, tf), F32), pltpu.VMEM((nf, SUBLANES, tf), F32)]
    if final:
        in_specs.append(pl.BlockSpec((1, d), lambda i, j: (0, 0)))
        args.append(final_g.reshape(1, d))
    out, us = pl.pallas_call(
        functools.partial(_ffn_body, stepwise=stepwise, final=final, seq=seq, nf=nf),
        grid=(m // tm, nf),
        in_specs=in_specs,
        out_specs=[pl.BlockSpec((tm, d), row_spec), us_spec],
        out_shape=[jax.ShapeDtypeStruct((m, d), F32), us_shape],
        scratch_shapes=scratch,
        compiler_params=_cparams("arbitrary", "arbitrary"),
        name="conv_ffn",
    )(*args)
    if stepwise:
        halves = [jnp.transpose(h, (2, 0, 1, 3)).reshape(m // seq, 2, f) for h in (us[..., :tf], us[..., tf:])]
    else:
        us = us[m // tm - 1]
        halves = [jnp.swapaxes(h, 0, 1).reshape(SUBLANES, f)[None, SUBLANES - 2:]
                  for h in (us[..., :tf], us[..., tf:])]
    return out, jnp.concatenate(halves, axis=-1)


def _ssm_in_body(*refs, nz, stepwise, seq):
    x_ref, g_ref, w_ref, wdt_ref, dtb_ref, cw_ref, cb_ref = refs[:7]
    k = 7
    keep = cw_ref.shape[0] - 1
    if stepwise:
        st_refs = refs[k:k + keep]
        k += keep
    z_ref, xbc_ref, dt_ref, us_ref, xn_ref, aux_ref = refs[k:k + 6]
    i = pl.program_id(0)
    j = pl.program_id(1)
    tm = x_ref.shape[0]

    @pl.when(j == 0)
    def _():
        xn = _rms(x_ref[...], g_ref[...]).astype(BF)
        xn_ref[...] = xn
        dt_ref[...] = _softplus(_dot(xn, wdt_ref[...]) + dtb_ref[...])

    z = _dot(xn_ref[...], w_ref[...])

    @pl.when(j < nz)
    def _():
        z_ref[...] = z

    @pl.when(j >= nz)
    def _():
        if stepwise:
            prev = st_refs
            _save_seq_tails(z, us_ref, 0, aux_ref, seq, keep)
        else:
            jx = j - nz

            @pl.when(i == 0)
            def _():
                aux_ref[jx] = jnp.zeros(aux_ref.shape[1:], F32)

            prev = aux_ref[jx]
            aux_ref[jx] = z[tm - SUBLANES:, :]
            us_ref[...] = z[tm - SUBLANES:, :]
        xbc_ref[...] = _silu(_causal_conv(z, cw_ref, cb_ref, prev, seq if stepwise else None))


def _ssm_in(x, g, w, wdt, dtb, cw, cb, state, di, cch, seq):
    m, d = x.shape
    tm = _tile(m, PROJ_ROWS)
    tn = min(_tile(di, 512), _tile(cch, 512))
    nz, nx = di // tn, cch // tn
    width = cw.shape[0]
    keep = width - 1
    stepwise = state is not None
    xcol = lambda i, j: (0, jnp.clip(j - nz, 0, nx - 1))
    in_specs = [
        pl.BlockSpec((tm, d), lambda i, j: (i, 0)),
        pl.BlockSpec((1, d), lambda i, j: (0, 0)),
        pl.BlockSpec((d, tn), lambda i, j: (0, j)),
        pl.BlockSpec((d, LANES), lambda i, j: (0, 0)),
        pl.BlockSpec((1, LANES), lambda i, j: (0, 0)),
        pl.BlockSpec((width, tn), xcol),
        pl.BlockSpec((1, tn), xcol),
    ]
    args = [x, g, w.astype(BF), wdt, dtb, cw, cb.reshape(1, cch)]
    if stepwise:
        in_specs += [pl.BlockSpec((tm // seq, tn), lambda i, j: (i, jnp.clip(j - nz, 0, nx - 1)))] * keep
        args += [state[:, r, :] for r in range(keep)]
        us_shape = jax.ShapeDtypeStruct((keep, nx, m // seq, tn), F32)
        us_spec = pl.BlockSpec((keep, None, tm // seq, tn), lambda i, j: (0, jnp.clip(j - nz, 0, nx - 1), i, 0))
        aux = pltpu.VMEM((tm, LANES), F32)
    else:
        us_shape = jax.ShapeDtypeStruct((m // tm, nx, SUBLANES, tn), F32)
        us_spec = pl.BlockSpec((None, None, SUBLANES, tn), lambda i, j: (i, jnp.clip(j - nz, 0, nx - 1), 0, 0))
        aux = pltpu.VMEM((nx, SUBLANES, tn), F32)
    z, xbc, dt, us = pl.pallas_call(
        functools.partial(_ssm_in_body, nz=nz, stepwise=stepwise, seq=seq),
        grid=(m // tm, nz + nx),
        in_specs=in_specs,
        out_specs=[
            pl.BlockSpec((tm, tn), lambda i, j: (i, jnp.clip(j, 0, nz - 1))),
            pl.BlockSpec((tm, tn), lambda i, j: (i, jnp.clip(j - nz, 0, nx - 1))),
            pl.BlockSpec((tm, LANES), lambda i, j: (i, 0)),
            us_spec,
        ],
        out_shape=[
            jax.ShapeDtypeStruct((m, di), F32),
            jax.ShapeDtypeStruct((m, cch), F32),
            jax.ShapeDtypeStruct((m, LANES), F32),
            us_shape,
        ],
        scratch_shapes=[pltpu.VMEM((tm, d), BF), aux],
        compiler_params=_cparams("arbitrary", "arbitrary"),
        name="ssm_in",
    )(*args)
    if stepwise:
        conv_state = jnp.transpose(us, (2, 0, 1, 3)).reshape(m // seq, keep, cch)
    else:
        conv_state = jnp.swapaxes(us[m // tm - 1], 0, 1).reshape(SUBLANES, cch)[None, SUBLANES - keep:]
    return z, xbc, dt, conv_state


def _gate_norm(y, z, ng):
    yz = y * _silu(z)
    return yz * lax.rsqrt(jnp.mean(yz * yz, axis=-1, keepdims=True) + EPS) * ng


def _ssd_seq_body(xbc_ref, dt_ref, z_ref, alog_ref, dsk_ref, ng_ref, y_ref, hout_ref, ht_ref,
                  *, nheads, groups, hd, ns):
    c = pl.program_id(0)
    cs = dt_ref.shape[0]
    hpg = nheads // groups
    gp = hpg * hd
    di = nheads * hd

    @pl.when(c == 0)
    def _():
        ht_ref[...] = jnp.zeros_like(ht_ref)

    dt = dt_ref[...]
    a_neg = jnp.where(_iota((1, LANES), 1) < nheads, -jnp.exp(alog_ref[...]), 0.0)
    acum = _sel_left(_tri(cs), dt * a_neg)
    acum_t = acum.T
    dt_t = dt.T
    tot = acum[cs - 1:cs, :]
    ex = (_iota((LANES, di), 1) // hd == _iota((LANES, di), 0)).astype(BF)
    coefx = _sel_right(jnp.exp(tot - acum) * dt, ex, parts=2)
    cdx = _sel_right(jnp.broadcast_to(jnp.exp(tot), (SUBLANES, LANES)), ex, parts=2)[0:1, :]
    causal = _iota((cs, cs), 1) <= _iota((cs, cs), 0)

    for g in range(groups):
        xs = xbc_ref[:, g * gp:(g + 1) * gp]
        bm = xbc_ref[:, di + g * ns:di + (g + 1) * ns]
        cm = xbc_ref[:, di + (groups + g) * ns:di + (groups + g + 1) * ns]
        cb = _dot_nt(cm.astype(BF), bm.astype(BF))
        ht = ht_ref[g]
        htb = ht.astype(BF)
        xb = xs.astype(BF)
        ys = []
        for r in range(hpg):
            hh = g * hpg + r
            acol = acum[:, hh:hh + 1]
            seg = jnp.broadcast_to(acol, (cs, cs)) - acum_t[hh:hh + 1, :]
            mp = cb * jnp.exp(jnp.where(causal, seg, NEG_INF)) * dt_t[hh:hh + 1, :]
            csc = cm * jnp.exp(jnp.broadcast_to(acol, (cs, ns)))
            lhs = jnp.concatenate([mp, csc], axis=1).astype(BF)
            rhs = jnp.concatenate([xb[:, r * hd:(r + 1) * hd], htb[:, r * hd:(r + 1) * hd]], axis=0)
            ys.append(_dot(lhs, rhs))
        yg = jnp.concatenate(ys, axis=1) + dsk_ref[:, g * gp:(g + 1) * gp] * xs
        y_ref[:, g * gp:(g + 1) * gp] = _gate_norm(
            yg, z_ref[:, g * gp:(g + 1) * gp], ng_ref[:, g * gp:(g + 1) * gp]).astype(BF)
        wg = (coefx[:, g * gp:(g + 1) * gp] * xs).astype(BF)
        ht_ref[g] = ht * cdx[:, g * gp:(g + 1) * gp] + _dot(bm.T.astype(BF), wg)

    @pl.when(c == pl.num_programs(0) - 1)
    def _():
        hout_ref[...] = ht_ref[...]


def _ssd_seq(xbc, dt, z, alog, dsk, ng, nheads, groups, hd, ns, chunk):
    l, cch = xbc.shape
    di = nheads * hd
    gp = di // groups
    return pl.pallas_call(
        functools.partial(_ssd_seq_body, nheads=nheads, groups=groups, hd=hd, ns=ns),
        grid=(l // chunk,),
        in_specs=[
            pl.BlockSpec((chunk, cch), lambda c: (c, 0)),
            pl.BlockSpec((chunk, LANES), lambda c: (c, 0)),
            pl.BlockSpec((chunk, di), lambda c: (c, 0)),
            pl.BlockSpec((1, LANES), lambda c: (0, 0)),
            pl.BlockSpec((1, di), lambda c: (0, 0)),
            pl.BlockSpec((1, di), lambda c: (0, 0)),
        ],
        out_specs=[pl.BlockSpec((chunk, di), lambda c: (c, 0)),
                   pl.BlockSpec((groups, ns, gp), lambda c: (0, 0, 0))],
        out_shape=[jax.ShapeDtypeStruct((l, di), BF),
                   jax.ShapeDtypeStruct((groups, ns, gp), F32)],
        scratch_shapes=[pltpu.VMEM((groups, ns, gp), F32)],
        compiler_params=_cparams("arbitrary"),
        name="ssd_seq",
    )(xbc, dt, z, alog, dsk, ng)


def _ssd_step_body(xbc_ref, dt_ref, z_ref, alog_ref, alogc_ref, dsk_ref, ng_ref, h0_ref, y_ref, hn_ref,
                   *, nheads, groups, hd, ns):
    t = dt_ref.shape[0]
    hpg = nheads // groups
    gp = hpg * hd
    di = nheads * hd
    gn = groups * ns

    dt = dt_ref[...]
    a_neg = jnp.where(_iota((1, LANES), 1) < nheads, -jnp.exp(alog_ref[...]), 0.0)
    rt = _iota((t, LANES), 0)
    acum = dt * a_neg
    sh = 1
    while sh < t:
        acum = acum + jnp.where(rt >= sh, pltpu.roll(acum, sh, 0), 0.0)
        sh *= 2
    tot = acum[t - 1:t, :]
    coef = jnp.exp(tot - acum) * dt
    eac = jnp.exp(acum)

    xs = xbc_ref[:, 0:di]
    bm = xbc_ref[:, di:di + gn]
    cm = xbc_ref[:, di + gn:di + 2 * gn]

    prods = jnp.concatenate([cm * bm[s:s + 1, :] for s in range(t)], axis=0)
    rsel = ((_iota((gn, LANES), 1) // hpg == _iota((gn, LANES), 0) // ns)
            & (_iota((gn, LANES), 1) < nheads)).astype(BF)
    cbe = _dot(prods.astype(BF), rsel)
    acl = jnp.concatenate([acum] * t, axis=0)
    acs = jnp.concatenate([jnp.broadcast_to(acum[s:s + 1, :], (t, LANES)) for s in range(t)], axis=0)
    dts = jnp.concatenate([jnp.broadcast_to(dt[s:s + 1, :], (t, LANES)) for s in range(t)], axis=0)
    rr = _iota((t * t, LANES), 0)
    ms = cbe * jnp.exp(jnp.where(rr % t >= rr // t, acl - acs, NEG_INF)) * dts
    ex = (_iota((LANES, di), 1) // hd == _iota((LANES, di), 0)).astype(BF)
    big = _sel_right(jnp.concatenate([ms, eac, coef], axis=0), ex, parts=2)
    y = big[0:t, :] * xs[0:1, :]
    for s in range(1, t):
        y = y + big[s * t:(s + 1) * t, :] * xs[s:s + 1, :]
    eacx = big[t * t:t * t + t, :]
    coefx = big[t * t + t:, :]

    dt_t = jnp.concatenate([dt, jnp.zeros((LANES - t, LANES), F32)], axis=0).T
    a_col = -jnp.exp(alogc_ref[...])
    cd_col = jnp.exp(jnp.sum(dt_t * a_col, axis=1, keepdims=True))

    for g in range(groups):
        gs = slice(g * gp, (g + 1) * gp)
        h0 = h0_ref[gs, :]
        yoff = _dot_nt(cm[:, g * ns:(g + 1) * ns].astype(BF), h0.astype(BF))
        yg = y[:, gs] + yoff * eacx[:, gs] + dsk_ref[:, gs] * xs[:, gs]
        y_ref[:, gs] = _gate_norm(yg, z_ref[:, gs], ng_ref[:, gs]).astype(BF)
        wg = (coefx[:, gs] * xs[:, gs]).astype(BF)
        wpad = jnp.concatenate([wg.astype(F32), jnp.zeros((LANES - t, gp), F32)], axis=0)
        wt = jnp.concatenate([wpad[:, k * LANES:(k + 1) * LANES].T for k in range(gp // LANES)], axis=0)
        bpad = jnp.concatenate([bm[:, g * ns:(g + 1) * ns], jnp.zeros((LANES - t, ns), F32)], axis=0)
        upd = _dot(wt.astype(BF), bpad.astype(BF))
        for r in range(hpg):
            hh = g * hpg + r
            rs = slice(r * hd, (r + 1) * hd)
            hn_ref[g * gp + r * hd:g * gp + (r + 1) * hd, :] = (
                h0[rs, :] * jnp.broadcast_to(cd_col[hh:hh + 1, :], (hd, ns)) + upd[rs, :])


def _ssd_step(xbc, dt, z, alog, alogc, dsk, ng, h0, nheads, groups, hd, ns):
    nb = h0.shape[0]
    m, cch = xbc.shape
    t = m // nb
    di = nheads * hd
    return pl.pallas_call(
        functools.partial(_ssd_step_body, nheads=nheads, groups=groups, hd=hd, ns=ns),
        grid=(nb,),
        in_specs=[
            pl.BlockSpec((t, cch), lambda b: (b, 0)),
            pl.BlockSpec((t, LANES), lambda b: (b, 0)),
            pl.BlockSpec((t, di), lambda b: (b, 0)),
            pl.BlockSpec((1, LANES), lambda b: (0, 0)),
            pl.BlockSpec((LANES, 1), lambda b: (0, 0)),
            pl.BlockSpec((1, di), lambda b: (0, 0)),
            pl.BlockSpec((1, di), lambda b: (0, 0)),
            pl.BlockSpec((None, di, ns), lambda b: (b, 0, 0)),
        ],
        out_specs=[pl.BlockSpec((t, di), lambda b: (b, 0)),
                   pl.BlockSpec((None, di, ns), lambda b: (b, 0, 0))],
        out_shape=[jax.ShapeDtypeStruct((m, di), BF),
                   jax.ShapeDtypeStruct((nb, di, ns), F32)],
        compiler_params=_cparams("parallel"),
        name="ssd_step",
    )(xbc, dt, z, alog, alogc, dsk, ng, h0)


def _pad_lanes(a):
    return jnp.pad(a, [(0, 0)] * (a.ndim - 1) + [(0, LANES - a.shape[-1])])


def _trunk(x, nb, page_table, cache_k, cache_v, cache_lf, conf_prev, mconv_prev, ssm_prev, ffn_prev, w):
    _, t, d = x.shape
    m = nb * t
    stepwise = conf_prev is not None
    nh, dh = cache_k.shape[-2:]
    fw = nh * dh
    x = x.reshape(m, d)

    w_in = w["att_w_in"][0]
    cc = w["conf_dw_w"].shape[2]
    cw = w["conf_dw_w"].shape[1]
    w5 = jnp.concatenate([w_in[:, :3 * fw], w_in[:, 3 * fw + nh:]], axis=1)
    wf = _pad_lanes(w_in[:, 3 * fw:3 * fw + nh]).astype(BF)
    bfp = _pad_lanes(w["att_b_f"][0].reshape(1, nh))
    q, k, v, u, lf = _even_in(x, w["ln_mix"][0].reshape(1, d), w5, wf, bfp, fw, cc)
    logf = lf[:, :nh]
    if stepwise:
        npool, page = cache_k.shape[1:3]
        att = _fox_sample(q, k, v, lf,
                          cache_k[0].reshape(npool, page * nh, dh), cache_v[0].reshape(npool, page * nh, dh),
                          cache_lf[0].reshape(npool, page * nh // LANES, LANES), page_table, nh, dh)
        xp = jnp.concatenate([conf_prev[0], u.reshape(nb, t, cc)], axis=1)
        conf_state = xp[:, t:]
        cv = _conv_step(xp, w["conf_dw_w"][0], w["conf_dw_b"][0]).reshape(m, cc)
    else:
        cq, ckt = _fox_cumsum(lf, nh)
        att = _fox_prompt(q, k, v, cq, ckt.reshape(nh, 1, m), nh, dh)
        conf_state = u[m - (cw - 1):].reshape(1, cw - 1, cc)
        cv = _conv_seq(u, w["conf_dw_w"][0], w["conf_dw_b"][0])
    w_out = w["att_w_out"][0]
    x = _out_even(x, att, cv, w["conf_ln_g"][0], w["conf_ln_b"][0], w_out[:fw], w_out[fw:])

    ffn_states = []

    def ffn(x, i, final_g):
        if stepwise:
            two_f = ffn_prev.shape[-1]
            st = ffn_prev[i]
        else:
            st = None
        x, us = _ffn(x, w["ln_ffn"][i], w["ffn_w_up"], w["ffn_dw_w"][i], w["ffn_dw_b"][i],
                     w["ffn_w_down"], i, st, final_g, t)
        ffn_states.append(us)
        return x

    x = ffn(x, 0, None)

    nheads, hd, ns = w["ssm_state_shape"]
    di = nheads * hd
    cch = w["ssm_conv_w"].shape[2]
    groups = (cch - di) // (2 * ns)
    w_in = w["ssm_w_in"][0]
    wdt = _pad_lanes(w_in[:, di + cch:]).astype(BF)
    dtb = _pad_lanes(w["ssm_dt_bias"][0].reshape(1, nheads))
    sw = w["ssm_conv_w"].shape[1]
    if stepwise:
        st = mconv_prev[0]
    else:
        st = None
    z, xbc, dt, mconv_state = _ssm_in(x, w["ln_mix"][1].reshape(1, d), w_in, wdt, dtb,
                                      w["ssm_conv_w"][0], w["ssm_conv_b"][0], st, di, cch, t)
    alog = _pad_lanes(w["ssm_a_log"][0].reshape(1, nheads))
    dsk = jnp.repeat(w["ssm_d"][0], hd).reshape(1, di)
    ng = w["ssm_norm_g"][0].reshape(1, di)
    if stepwise:
        yn, hn = _ssd_step(xbc, dt, z, alog, alog.reshape(LANES, 1), dsk, ng,
                           ssm_prev[0].reshape(nb, di, ns), nheads, groups, hd, ns)
        ssm_state = hn.reshape(nb, nheads, hd, ns)
    else:
        chunk = _tile(m, 128)
        yn, hout = _ssd_seq(xbc, dt, z, alog, dsk, ng, nheads, groups, hd, ns, chunk)
        hpg = nheads // groups
        ssm_state = jnp.transpose(hout.reshape(groups, ns, hpg, hd), (0, 2, 3, 1)).reshape(1, nheads, hd, ns)
    x = _mm_res(x, yn, w["ssm_w_out"][0])
    x = ffn(x, 1, w["ln_final"])

    return (x.reshape(nb, t, d),
            k.reshape(1, nb, t, nh, dh), v.reshape(1, nb, t, nh, dh), logf.reshape(1, nb, t, nh),
            conf_state[None], mconv_state[None], ssm_state[None], jnp.stack(ffn_states))


def kernel(x_prompt, x_sample, cache_k, cache_v, cache_logf, state_conf_conv, state_ssm_conv, state_ssm,
           state_ffn_conv, page_table,
           ln_mix, ln_ffn, ln_final,
           att_w_in, att_b_f, conf_dw_w, conf_dw_b, conf_ln_g, conf_ln_b, att_w_out,
           ssm_w_in, ssm_conv_w, ssm_conv_b, ssm_dt_bias, ssm_a_log, ssm_d, ssm_norm_g, ssm_w_out,
           ffn_w_up, ffn_dw_w, ffn_dw_b, ffn_w_down):
    assert ln_mix.shape[0] == 2 and x_prompt.shape[0] == 1, "two layers, one prompt sequence"
    w = dict(ln_mix=ln_mix, ln_ffn=ln_ffn, ln_final=ln_final,
             att_w_in=att_w_in, att_b_f=att_b_f, conf_dw_w=conf_dw_w, conf_dw_b=conf_dw_b,
             conf_ln_g=conf_ln_g, conf_ln_b=conf_ln_b, att_w_out=att_w_out,
             ssm_w_in=ssm_w_in, ssm_conv_w=ssm_conv_w, ssm_conv_b=ssm_conv_b, ssm_dt_bias=ssm_dt_bias,
             ssm_a_log=ssm_a_log, ssm_d=ssm_d, ssm_norm_g=ssm_norm_g, ssm_w_out=ssm_w_out,
             ffn_w_up=ffn_w_up, ffn_dw_w=ffn_dw_w, ffn_dw_b=ffn_dw_b, ffn_w_down=ffn_w_down,
             ssm_state_shape=state_ssm.shape[2:])
    nb = x_sample.shape[0]
    outs_p = _trunk(x_prompt, 1, None, cache_k, cache_v, cache_logf, None, None, None, None, w)
    outs_s = _trunk(x_sample, nb, page_table, cache_k, cache_v, cache_logf,
                    state_conf_conv, state_ssm_conv, state_ssm, state_ffn_conv, w)
    return (outs_p[0], outs_s[0]) + tuple(outs_p[1:]) + tuple(outs_s[1:])
```

```python
import functools

import jax
import jax.numpy as jnp
from jax import lax
from jax.experimental import pallas as pl
from jax.experimental.pallas import tpu as pltpu

EPS = 1e-6
BF = jnp.bfloat16
F32 = jnp.float32
LANES = 128
SUBLANES = 8
VMEM_LIMIT = 56 * 1024 * 1024
NEG_INF = float("-inf")
PROJ_ROWS = 1024
FFN_ROWS = 512
CONV_ROWS = 16


def _cparams(*sem):
    return pltpu.CompilerParams(dimension_semantics=sem, vmem_limit_bytes=VMEM_LIMIT)


def _tile(n, pref):
    t = pref
    while t >= LANES and t % LANES == 0:
        if n % t == 0:
            return t
        t //= 2
    return n


def _iota(shape, dim):
    return lax.broadcasted_iota(jnp.int32, shape, dim)


def _split3(x):
    hi = x.astype(BF)
    r1 = x - hi.astype(F32)
    mid = r1.astype(BF)
    lo = (r1 - mid.astype(F32)).astype(BF)
    return hi, mid, lo


def _dot(a, b):
    return jnp.dot(a, b, preferred_element_type=F32)


def _dot_nt(a, b):
    return lax.dot_general(a, b, (((1,), (1,)), ((), ())), preferred_element_type=F32)


def _sel_left(sel, x, parts=3):
    ps = _split3(x)[:parts]
    out = _dot(sel, ps[0])
    for p in ps[1:]:
        out = out + _dot(sel, p)
    return out


def _sel_right(x, sel, parts=3):
    ps = _split3(x)[:parts]
    out = _dot(ps[0], sel)
    for p in ps[1:]:
        out = out + _dot(p, sel)
    return out


def _log_sigmoid(x):
    return jnp.minimum(x, 0.0) - jnp.log(1.0 + jnp.exp(-jnp.abs(x)))


def _softplus(x):
    return jnp.maximum(x, 0.0) + jnp.log(1.0 + jnp.exp(-jnp.abs(x)))


def _silu(x):
    return x * jax.nn.sigmoid(x)


def _rms(x, g):
    return x * lax.rsqrt(jnp.mean(x * x, axis=-1, keepdims=True) + EPS) * g


def _tri(n):
    return (_iota((n, n), 1) <= _iota((n, n), 0)).astype(BF)


def _even_in_body(x_ref, g_ref, w_ref, wf_ref, bf_ref,
                  q_ref, k_ref, v_ref, u_ref, lf_ref, xn_ref, a_ref, *, nf, nc):
    j = pl.program_id(1)

    @pl.when(j == 0)
    def _():
        xn = _rms(x_ref[...], g_ref[...]).astype(BF)
        xn_ref[...] = xn
        lf_ref[...] = _log_sigmoid(_dot(xn, wf_ref[...]) + bf_ref[...])

    z = _dot(xn_ref[...], w_ref[...])

    @pl.when(j < nf)
    def _():
        q_ref[...] = z

    @pl.when((j >= nf) & (j < 2 * nf))
    def _():
        k_ref[...] = z

    @pl.when((j >= 2 * nf) & (j < 3 * nf))
    def _():
        v_ref[...] = z

    @pl.when((j >= 3 * nf) & ((j - 3 * nf) % 2 == 0))
    def _():
        a_ref[...] = z

    @pl.when((j >= 3 * nf) & ((j - 3 * nf) % 2 == 1))
    def _():
        u_ref[...] = a_ref[...] * jax.nn.sigmoid(z)


def _even_in(x, g, w5, wf, bfp, fw, cc):
    m, d = x.shape
    tm = _tile(m, PROJ_ROWS)
    tn = min(_tile(fw, 512), _tile(cc, 512))
    nf, nc = fw // tn, cc // tn
    nj = 3 * nf + 2 * nc

    def w_map(i, j):
        jj = j - 3 * nf
        return (0, jnp.where(j < 3 * nf, j, 3 * nf + (jj % 2) * nc + jj // 2))

    def seg_map(base, n, div=1):
        return lambda i, j: (i, jnp.clip((j - base) // div, 0, n - 1))

    return pl.pallas_call(
        functools.partial(_even_in_body, nf=nf, nc=nc),
        grid=(m // tm, nj),
        in_specs=[
            pl.BlockSpec((tm, d), lambda i, j: (i, 0)),
            pl.BlockSpec((1, d), lambda i, j: (0, 0)),
            pl.BlockSpec((d, tn), w_map),
            pl.BlockSpec((d, LANES), lambda i, j: (0, 0)),
            pl.BlockSpec((1, LANES), lambda i, j: (0, 0)),
        ],
        out_specs=[
            pl.BlockSpec((tm, tn), seg_map(0, nf)),
            pl.BlockSpec((tm, tn), seg_map(nf, nf)),
            pl.BlockSpec((tm, tn), seg_map(2 * nf, nf)),
            pl.BlockSpec((tm, tn), seg_map(3 * nf, nc, 2)),
            pl.BlockSpec((tm, LANES), lambda i, j: (i, 0)),
        ],
        out_shape=[
            jax.ShapeDtypeStruct((m, fw), F32),
            jax.ShapeDtypeStruct((m, fw), F32),
            jax.ShapeDtypeStruct((m, fw), F32),
            jax.ShapeDtypeStruct((m, cc), F32),
            jax.ShapeDtypeStruct((m, LANES), F32),
        ],
        scratch_shapes=[pltpu.VMEM((tm, d), BF), pltpu.VMEM((tm, tn), F32)],
        compiler_params=_cparams("parallel", "arbitrary"),
        name="even_in",
    )(x, g, w5.astype(BF), wf, bfp)


def _fox_cumsum_body(lf_ref, cq_ref, ckt_ref, carry_ref, *, nh):
    @pl.when(pl.program_id(0) == 0)
    def _():
        carry_ref[...] = jnp.zeros_like(carry_ref)

    lf = lf_ref[...]
    tb = lf.shape[0]
    cs = _sel_left(_tri(tb), lf) + carry_ref[...]
    carry_ref[...] = cs[tb - 1:tb, :]
    for h in range(nh):
        cq_ref[h] = jnp.broadcast_to(cs[:, h:h + 1], (tb, LANES))
    ckt_ref[...] = cs.T[:nh, :]


def _fox_cumsum(lf, nh):
    m = lf.shape[0]
    tb = _tile(m, 1024)
    return pl.pallas_call(
        functools.partial(_fox_cumsum_body, nh=nh),
        grid=(m // tb,),
        in_specs=[pl.BlockSpec((tb, LANES), lambda i: (i, 0))],
        out_specs=[pl.BlockSpec((nh, tb, LANES), lambda i: (0, i, 0)),
                   pl.BlockSpec((nh, tb), lambda i: (0, i))],
        out_shape=[jax.ShapeDtypeStruct((nh, m, LANES), F32),
                   jax.ShapeDtypeStruct((nh, m), F32)],
        scratch_shapes=[pltpu.VMEM((1, LANES), F32)],
        compiler_params=_cparams("arbitrary"),
        name="fox_cumsum",
    )(lf)


def _fox_prompt_body(q_ref, k_ref, v_ref, cq_ref, ck_ref, o_ref, kb_ref, vb_ref, m_ref, acc_ref, s_ref, *, scale):
    qi = pl.program_id(1)
    t, dh = q_ref.shape
    log2e = 1.4426950408889634

    @pl.when(qi == 0)
    def _():
        kb_ref[...] = k_ref[...].astype(BF)
        vb_ref[:, 0:dh] = v_ref[...].astype(BF)
        vb_ref[:, dh:] = (_iota((k_ref.shape[0], dh), 1) == 0).astype(BF)

    qs = (q_ref[...] * (scale * log2e)).astype(BF)
    m_ref[...] = jnp.full_like(m_ref, NEG_INF)
    acc_ref[...] = jnp.zeros_like(acc_ref)
    reps = t // LANES
    rc = min(t, 256)

    def scores(kj, slot):
        off = pl.multiple_of(kj * t, t)
        s_ref[slot] = _dot_nt(qs, kb_ref[pl.ds(off, t), :]) - ck_ref[kj] * log2e

    def update(kj, slot, masked):
        off = pl.multiple_of(kj * t, t)
        for r in range(t // rc):
            rows = slice(r * rc, (r + 1) * rc)
            s = s_ref[slot, rows, :]
            if masked:
                s = jnp.where(_iota((rc, t), 1) <= r * rc + _iota((rc, t), 0), s, NEG_INF)
            cq = cq_ref[rows, :] * log2e
            m_prev = m_ref[rows, :]
            m_new = jnp.maximum(m_prev, jnp.max(s, axis=1, keepdims=True) + cq)
            shift = m_new - cq
            p = jnp.exp2(s - jnp.concatenate([shift] * reps, axis=1))
            alpha = jnp.exp2(m_prev - m_new)
            acc_ref[rows, :] = (jnp.concatenate([alpha] * (2 * dh // LANES), axis=1) * acc_ref[rows, :]
                                + _dot(p.astype(BF), vb_ref[pl.ds(off, t), :]))
            m_ref[rows, :] = m_new

    scores(0, 0)

    def body(k2, carry):
        update(2 * k2, 0, False)
        scores(2 * k2 + 1, 1)
        update(2 * k2 + 1, 1, False)
        scores(2 * k2 + 2, 0)
        return carry

    lax.fori_loop(0, qi // 2, body, 0)

    @pl.when(qi % 2 == 0)
    def _():
        update(qi, 0, True)

    @pl.when(qi % 2 == 1)
    def _():
        update(qi - 1, 0, False)
        scores(qi, 1)
        update(qi, 1, True)

    o_ref[...] = acc_ref[:, 0:dh] / acc_ref[:, dh:dh + 1]


def _fox_prompt(q, k, v, cq, ck, nh, dh):
    m = q.shape[0]
    t = _tile(m, 1024)
    n = m // t
    return pl.pallas_call(
        functools.partial(_fox_prompt_body, scale=dh ** -0.5),
        grid=(nh, n),
        in_specs=[
            pl.BlockSpec((t, dh), lambda h, i: (i, h)),
            pl.BlockSpec((m, dh), lambda h, i: (0, h)),
            pl.BlockSpec((m, dh), lambda h, i: (0, h)),
            pl.BlockSpec((None, t, LANES), lambda h, i: (h, i, 0)),
            pl.BlockSpec((None, n, 1, t), lambda h, i: (h, 0, 0, 0)),
        ],
        out_specs=pl.BlockSpec((t, dh), lambda h, i: (i, h)),
        out_shape=jax.ShapeDtypeStruct((m, nh * dh), F32),
        scratch_shapes=[pltpu.VMEM((m, dh), BF), pltpu.VMEM((m, 2 * dh), BF),
                        pltpu.VMEM((t, LANES), F32), pltpu.VMEM((t, 2 * dh), F32),
                        pltpu.VMEM((2, t, t), F32)],
        compiler_params=_cparams("parallel", "arbitrary"),
        name="fox_prompt",
    )(q, k, v, cq, ck.reshape(nh, n, 1, t))


def _fox_sample_body(pt_ref, q_ref, kn_ref, vn_ref, lfn_ref, *rest, npg, nh, dh, scale):
    k_refs = rest[:npg]
    v_refs = rest[npg:2 * npg]
    lf_refs = rest[2 * npg:3 * npg]
    o_ref, s_ref, acc_ref = rest[3 * npg:]
    t, fw = q_ref.shape
    page = k_refs[0].shape[0] // nh
    ht = nh * t
    ppr = LANES // nh

    def load_page(ref):
        return jnp.concatenate([ref[pl.ds(h, page, stride=nh), :] for h in range(nh)], axis=1)

    q = q_ref[...]
    qt = jnp.concatenate([q] * nh + [jnp.zeros((LANES - ht, fw), F32)], axis=0)
    qbd = jnp.where(_iota((LANES, fw), 0) // t == _iota((LANES, fw), 1) // dh, qt, 0.0).astype(BF)

    r2 = _iota((LANES, LANES), 0)
    c2 = _iota((LANES, LANES), 1)
    e_past = ((r2 % nh == c2 // t) & (c2 < ht)).astype(BF)
    e_new = ((r2 == c2 // t) & (c2 < ht)).astype(BF)
    tri = _tri(page)
    rp = _iota((page, LANES), 0)
    cp = _iota((page, LANES), 1)

    carry = jnp.zeros((1, LANES), F32)
    for p in range(npg):
        lfp = lf_refs[p][...]
        b = jnp.concatenate([jnp.broadcast_to(lfp[r:r + 1, :], (ppr, LANES))
                             for r in range(page // ppr)], axis=0)
        b = jnp.where(cp // nh == rp % ppr, b, 0.0)
        clocal = _sel_left(tri, _sel_right(b, e_past))
        st = _dot_nt(load_page(k_refs[p]).astype(BF), qbd) * scale
        s_ref[p * page:(p + 1) * page, :] = st - (clocal + carry)
        carry = carry + clocal[page - 1:page, :]

    y = _sel_right(lfn_ref[...], e_new)
    rt = _iota((t, LANES), 0)
    sh = 1
    while sh < t:
        y = y + jnp.where(rt >= sh, pltpu.roll(y, sh, 0), 0.0)
        sh *= 2
    cn = carry + y
    cq = jnp.sum(jnp.where(rt == _iota((t, LANES), 1) % t, cn, 0.0), axis=0, keepdims=True)
    knp = jnp.concatenate([kn_ref[...], jnp.zeros((page - t, fw), F32)], axis=0)
    stn = _dot_nt(knp.astype(BF), qbd) * scale
    cnp = jnp.concatenate([cn, jnp.zeros((page - t, LANES), F32)], axis=0)
    s_ref[npg * page:, :] = jnp.where((rp < t) & (rp <= cp % t), stn - cnp, NEG_INF)

    mx = jnp.max(s_ref[0:page, :], axis=0, keepdims=True)
    for p in range(1, npg + 1):
        mx = jnp.maximum(mx, jnp.max(s_ref[p * page:(p + 1) * page, :], axis=0, keepdims=True))
    shift = cq - (mx + cq)

    acc_ref[...] = jnp.zeros_like(acc_ref)
    lsum = jnp.zeros((LANES, 1), F32)
    for p in range(npg + 1):
        pt = jnp.exp(s_ref[p * page:(p + 1) * page, :] + shift).T
        lsum = lsum + jnp.sum(pt, axis=1, keepdims=True)
        if p < npg:
            vp = load_page(v_refs[p])
        else:
            vp = jnp.concatenate([vn_ref[...], jnp.zeros((page - t, fw), F32)], axis=0)
        acc_ref[...] += _dot(pt[:ht, :].astype(BF), vp.astype(BF))

    for h in range(nh):
        o_ref[:, h * dh:(h + 1) * dh] = (acc_ref[h * t:(h + 1) * t, h * dh:(h + 1) * dh]
                                          / lsum[h * t:(h + 1) * t, :])


def _fox_sample(q, kn, vn, lfn, ck, cv, clf, page_table, nh, dh):
    nb, npg = page_table.shape
    m, fw = q.shape
    t = m // nb
    prows = ck.shape[1]
    page = prows // nh
    rows = page * nh // LANES

    def pg_map(p):
        return lambda b, pt: (pt[b * npg + p], 0, 0)

    seq = pl.BlockSpec((t, fw), lambda b, pt: (b, 0))
    in_specs = [seq, seq, seq, pl.BlockSpec((t, LANES), lambda b, pt: (b, 0))]
    in_specs += [pl.BlockSpec((None, prows, dh), pg_map(p)) for p in range(npg)]
    in_specs += [pl.BlockSpec((None, prows, dh), pg_map(p)) for p in range(npg)]
    in_specs += [pl.BlockSpec((None, rows, LANES), pg_map(p)) for p in range(npg)]
    return pl.pallas_call(
        functools.partial(_fox_sample_body, npg=npg, nh=nh, dh=dh, scale=dh ** -0.5),
        grid_spec=pltpu.PrefetchScalarGridSpec(
            num_scalar_prefetch=1,
            grid=(nb,),
            in_specs=in_specs,
            out_specs=pl.BlockSpec((t, fw), lambda b, pt: (b, 0)),
            scratch_shapes=[pltpu.VMEM(((npg + 1) * page, LANES), F32),
                            pltpu.VMEM((nh * t, fw), F32)],
        ),
        out_shape=jax.ShapeDtypeStruct((m, fw), F32),
        compiler_params=_cparams("parallel"),
        name="fox_sample",
    )(page_table.reshape(-1), q, kn, vn, lfn, *([ck] * npg), *([cv] * npg), *([clf] * npg))


def _conv_seq_body(xm_ref, xh_ref, w_ref, b_ref, o_ref, xf_ref, xs_ref, *, width):
    tl = xm_ref.shape[0]
    hb = xh_ref.shape[0]
    n = hb + tl
    xf_ref[0:hb, :] = jnp.where(pl.program_id(0) == 0, 0.0, xh_ref[...])
    xf_ref[hb:n, :] = xm_ref[...]
    xf_ref[n:, :] = jnp.zeros((SUBLANES, xf_ref.shape[1]), F32)
    for b in range(SUBLANES):
        xs_ref[b] = xf_ref[b:b + n, :]
    base = hb - (width - 1)
    rc = min(tl, 32)
    for c in range(tl // rc):
        acc = b_ref[...]
        for j in range(width):
            a, b = divmod(base + c * rc + j, SUBLANES)
            acc = acc + w_ref[j:j + 1, :] * xs_ref[b, a * SUBLANES:a * SUBLANES + rc, :]
        o_ref[c * rc:(c + 1) * rc, :] = acc


def _conv_seq(x, w, b):
    l, c = x.shape
    width = w.shape[0]
    hb = -(-(width - 1) // SUBLANES) * SUBLANES
    tl = _tile(l, 512)
    tc = _tile(c, 256)
    r = tl // hb
    return pl.pallas_call(
        functools.partial(_conv_seq_body, width=width),
        grid=(l // tl, c // tc),
        in_specs=[
            pl.BlockSpec((tl, tc), lambda i, j: (i, j)),
            pl.BlockSpec((hb, tc), lambda i, j: (jnp.maximum(i * r - 1, 0), j)),
            pl.BlockSpec((width, tc), lambda i, j: (0, j)),
            pl.BlockSpec((1, tc), lambda i, j: (0, j)),
        ],
        out_specs=pl.BlockSpec((tl, tc), lambda i, j: (i, j)),
        out_shape=jax.ShapeDtypeStruct((l, c), F32),
        scratch_shapes=[pltpu.VMEM((hb + tl + SUBLANES, tc), F32),
                        pltpu.VMEM((SUBLANES, hb + tl, tc), F32)],
        compiler_params=_cparams("parallel", "parallel"),
        name="conv_seq",
    )(x, x, w, b.reshape(1, c))


def _conv_step_body(xp_ref, w_ref, b_ref, o_ref, *, width):
    t = o_ref.shape[1]
    acc = b_ref[...] + w_ref[0:1, :] * xp_ref[:, 0:t, :]
    for j in range(1, width):
        acc = acc + w_ref[j:j + 1, :] * xp_ref[:, j:j + t, :]
    o_ref[...] = acc


def _conv_step(xp, w, b):
    nb, rows, c = xp.shape
    width = w.shape[0]
    t = rows - (width - 1)
    bb = 16 if nb % 16 == 0 else nb
    tc = _tile(c, 512)
    return pl.pallas_call(
        functools.partial(_conv_step_body, width=width),
        grid=(nb // bb, c // tc),
        in_specs=[
            pl.BlockSpec((bb, rows, tc), lambda i, j: (i, 0, j)),
            pl.BlockSpec((width, tc), lambda i, j: (0, j)),
            pl.BlockSpec((1, tc), lambda i, j: (0, j)),
        ],
        out_specs=pl.BlockSpec((bb, t, tc), lambda i, j: (i, 0, j)),
        out_shape=jax.ShapeDtypeStruct((nb, t, c), F32),
        compiler_params=_cparams("parallel", "parallel"),
        name="conv_step",
    )(xp, w, b.reshape(1, c))


def _out_even_body(x_ref, att_ref, cv_ref, lg_ref, lb_ref, wa_ref, wc_ref, o_ref, ab_ref, cb_ref):
    @pl.when(pl.program_id(1) == 0)
    def _():
        u = cv_ref[...]
        xc = u - jnp.mean(u, axis=-1, keepdims=True)
        var = jnp.mean(xc * xc, axis=-1, keepdims=True)
        cb_ref[...] = _silu(xc * lax.rsqrt(var + EPS) * lg_ref[...] + lb_ref[...]).astype(BF)
        ab_ref[...] = att_ref[...].astype(BF)

    o_ref[...] = x_ref[...] + _dot(ab_ref[...], wa_ref[...]) + _dot(cb_ref[...], wc_ref[...])


def _out_even(x, att, cv, lg, lb, wa, wc):
    m, d = x.shape
    fw, cc = att.shape[1], cv.shape[1]
    tm = _tile(m, PROJ_ROWS)
    tn = _tile(d, 512)
    return pl.pallas_call(
        _out_even_body,
        grid=(m // tm, d // tn),
        in_specs=[
            pl.BlockSpec((tm, tn), lambda i, j: (i, j)),
            pl.BlockSpec((tm, fw), lambda i, j: (i, 0)),
            pl.BlockSpec((tm, cc), lambda i, j: (i, 0)),
            pl.BlockSpec((1, cc), lambda i, j: (0, 0)),
            pl.BlockSpec((1, cc), lambda i, j: (0, 0)),
            pl.BlockSpec((fw, tn), lambda i, j: (0, j)),
            pl.BlockSpec((cc, tn), lambda i, j: (0, j)),
        ],
        out_specs=pl.BlockSpec((tm, tn), lambda i, j: (i, j)),
        out_shape=jax.ShapeDtypeStruct((m, d), F32),
        scratch_shapes=[pltpu.VMEM((tm, fw), BF), pltpu.VMEM((tm, cc), BF)],
        compiler_params=_cparams("parallel", "arbitrary"),
        name="out_even",
    )(x, att, cv, lg.reshape(1, cc), lb.reshape(1, cc), wa.astype(BF), wc.astype(BF))


def _mm_res_body(x_ref, a_ref, w_ref, o_ref):
    o_ref[...] = x_ref[...] + _dot(a_ref[...], w_ref[...])


def _mm_res(x, a, w):
    m, d = x.shape
    kk = a.shape[1]
    tm = _tile(m, PROJ_ROWS)
    tn = _tile(d, 512)
    return pl.pallas_call(
        _mm_res_body,
        grid=(m // tm, d // tn),
        in_specs=[
            pl.BlockSpec((tm, tn), lambda i, j: (i, j)),
            pl.BlockSpec((tm, kk), lambda i, j: (i, 0)),
            pl.BlockSpec((kk, tn), lambda i, j: (0, j)),
        ],
        out_specs=pl.BlockSpec((tm, tn), lambda i, j: (i, j)),
        out_shape=jax.ShapeDtypeStruct((m, d), F32),
        compiler_params=_cparams("parallel", "arbitrary"),
        name="mm_res",
    )(x, a, w.astype(BF))


def _causal_conv(u, w_ref, b_ref, prev, seq):
    tm = u.shape[0]
    width = w_ref.shape[0]
    acc = b_ref[...] + w_ref[width - 1:width, :] * u
    if seq is None:
        row = _iota(prev.shape, 0)
    else:
        row = _iota(u.shape, 0) % seq
        spread = (_iota((tm, tm // seq), 0) // seq == _iota((tm, tm // seq), 1)).astype(BF)
        prev = [_sel_left(spread, p[...]) for p in prev]
    for s in range(1, width):
        r = pltpu.roll(u, s, 0)
        if seq is None:
            head = jnp.where(row < s, pltpu.roll(prev, s, 0), r[:SUBLANES])
            r = jnp.concatenate([head, r[SUBLANES:]], axis=0)
        else:
            for k in range(s):
                r = jnp.where(row == k, prev[width - 1 - s + k], r)
        acc = acc + w_ref[width - 1 - s:width - s, :] * r
    return acc


def _conv_rows(u_ref, w_ref, b_ref, r0, rc):
    width = w_ref.shape[0]
    win = u_ref[r0:r0 + SUBLANES + rc, :]
    acc = b_ref[...] + w_ref[width - 1:width, :] * win[SUBLANES:, :]
    for s in range(1, width):
        acc = acc + w_ref[width - 1 - s:width - s, :] * pltpu.roll(win, s, 0)[SUBLANES:, :]
    return acc


def _save_seq_tails(u, us_ref, col, ub_ref, seq, keep):
    tm, c = u.shape
    for k in range(c // LANES):
        ub_ref[...] = u[:, k * LANES:(k + 1) * LANES]
        for r in range(keep):
            us_ref[r, :, col + k * LANES:col + (k + 1) * LANES] = (
                ub_ref[pl.ds(seq - keep + r, tm // seq, stride=seq), :])


def _ffn_body(*refs, stepwise, final, seq, nf):
    x_ref, g_ref, wa_ref, wv_ref, dwa_ref, dwv_ref, ba_ref, bv_ref, wd_ref = refs[:9]
    k = 9
    keep = dwa_ref.shape[0] - 1
    if stepwise:
        sa_refs, sv_refs = refs[k:k + keep], refs[k + keep:k + 2 * keep]
        k += 2 * keep
    if final:
        fg_ref = refs[k]
        k += 1
    o_ref, us_ref, xn_ref = refs[k:k + 3]
    k += 3
    i = pl.program_id(0)
    j = pl.program_id(1)
    tm, tf = x_ref.shape[0], wa_ref.shape[1]
    width = dwa_ref.shape[0]

    @pl.when(j == 0)
    def _():
        xn_ref[...] = _rms(x_ref[...], g_ref[...]).astype(BF)
        o_ref[...] = x_ref[...]

    xn = xn_ref[...]
    if stepwise:
        ua = _dot(xn, wa_ref[...])
        uv = _dot(xn, wv_ref[...])
        _save_seq_tails(ua, us_ref, 0, refs[k], seq, width - 1)
        _save_seq_tails(uv, us_ref, tf, refs[k], seq, width - 1)
        a = _causal_conv(ua, dwa_ref, ba_ref, sa_refs, seq)
        v = _causal_conv(uv, dwv_ref, bv_ref, sv_refs, seq)
        o_ref[...] += _dot((_silu(a) * v).astype(BF), wd_ref[...])
    else:
        ua_ref, uv_ref, h_ref, ca_ref, cv_ref = refs[k:k + 5]

        @pl.when(i == 0)
        def _():
            ca_ref[j] = jnp.zeros((SUBLANES, tf), F32)
            cv_ref[j] = jnp.zeros((SUBLANES, tf), F32)

        ua_ref[0:SUBLANES, :] = ca_ref[j]
        uv_ref[0:SUBLANES, :] = cv_ref[j]
        ua_ref[SUBLANES:, :] = _dot(xn, wa_ref[...])
        uv_ref[SUBLANES:, :] = _dot(xn, wv_ref[...])
        ca_ref[j] = ua_ref[tm:, :]
        cv_ref[j] = uv_ref[tm:, :]
        us_ref[:, 0:tf] = ua_ref[tm:, :]
        us_ref[:, tf:2 * tf] = uv_ref[tm:, :]
        rc = min(tm, CONV_ROWS)
        for c in range(tm // rc):
            a = _conv_rows(ua_ref, dwa_ref, ba_ref, c * rc, rc)
            v = _conv_rows(uv_ref, dwv_ref, bv_ref, c * rc, rc)
            h_ref[c * rc:(c + 1) * rc, :] = (_silu(a) * v).astype(BF)
        o_ref[...] += _dot(h_ref[...], wd_ref[...])

    if final:
        @pl.when(j == nf - 1)
        def _():
            o_ref[...] = _rms(o_ref[...], fg_ref[...])


def _ffn(x, g, w_up, dw_w, dw_b, w_down, layer, state, final_g, seq):
    m, d = x.shape
    f = w_down.shape[1]
    tm = _tile(m, FFN_ROWS)
    tf = _tile(f, 512)
    nf = f // tf
    stepwise = state is not None
    final = final_g is not None
    row_spec = lambda i, j: (i, 0)
    a_col = lambda i, j: (0, j)
    v_col = lambda i, j: (0, j + nf)
    in_specs = [
        pl.BlockSpec((tm, d), row_spec),
        pl.BlockSpec((1, d), lambda i, j: (0, 0)),
        pl.BlockSpec((None, d, tf), lambda i, j: (layer, 0, j)),
        pl.BlockSpec((None, d, tf), lambda i, j: (layer, 0, j + nf)),
        pl.BlockSpec((3, tf), a_col),
        pl.BlockSpec((3, tf), v_col),
        pl.BlockSpec((1, tf), a_col),
        pl.BlockSpec((1, tf), v_col),
        pl.BlockSpec((None, tf, d), lambda i, j: (layer, j, 0)),
    ]
    w_up = w_up.astype(BF)
    w_down = w_down.astype(BF)
    args = [x, g.reshape(1, d), w_up, w_up, dw_w, dw_w, dw_b.reshape(1, 2 * f), dw_b.reshape(1, 2 * f), w_down]
    scratch = [pltpu.VMEM((tm, d), BF)]
    if stepwise:
        rows = [state[:, r, :] for r in range(state.shape[1])]
        in_specs += [pl.BlockSpec((tm // seq, tf), lambda i, j: (i, j))] * len(rows)
        in_specs += [pl.BlockSpec((tm // seq, tf), lambda i, j: (i, j + nf))] * len(rows)
        args += rows + rows
        us_shape = jax.ShapeDtypeStruct((2, nf, m // seq, 2 * tf), F32)
        us_spec = pl.BlockSpec((2, None, tm // seq, 2 * tf), lambda i, j: (0, j, i, 0))
        scratch += [pltpu.VMEM((tm, LANES), F32)]
    else:
        us_shape = jax.ShapeDtypeStruct((m // tm, nf, SUBLANES, 2 * tf), F32)
        us_spec = pl.BlockSpec((None, None, SUBLANES, 2 * tf), lambda i, j: (i, j, 0, 0))
        scratch += [pltpu.VMEM((SUBLANES + tm, tf), F32), pltpu.VMEM((SUBLANES + tm, tf), F32),
                    pltpu.VMEM((tm, tf), BF),
                    pltpu.VMEM((nf, SUBLANES, tf), F32), pltpu.VMEM((nf, SUBLANES, tf), F32)]
    if final:
        in_specs.append(pl.BlockSpec((1, d), lambda i, j: (0, 0)))
        args.append(final_g.reshape(1, d))
    out, us = pl.pallas_call(
        functools.partial(_ffn_body, stepwise=stepwise, final=final, seq=seq, nf=nf),
        grid=(m // tm, nf),
        in_specs=in_specs,
        out_specs=[pl.BlockSpec((tm, d), row_spec), us_spec],
        out_shape=[jax.ShapeDtypeStruct((m, d), F32), us_shape],
        scratch_shapes=scratch,
        compiler_params=_cparams("arbitrary", "arbitrary"),
        name="conv_ffn",
    )(*args)
    if stepwise:
        halves = [jnp.transpose(h, (2, 0, 1, 3)).reshape(m // seq, 2, f) for h in (us[..., :tf], us[..., tf:])]
    else:
        us = us[m // tm - 1]
        halves = [jnp.swapaxes(h, 0, 1).reshape(SUBLANES, f)[None, SUBLANES - 2:]
                  for h in (us[..., :tf], us[..., tf:])]
    return out, jnp.concatenate(halves, axis=-1)


def _ssm_in_body(*refs, nz, stepwise, seq):
    x_ref, g_ref, w_ref, wdt_ref, dtb_ref, cw_ref, cb_ref = refs[:7]
    k = 7
    keep = cw_ref.shape[0] - 1
    if stepwise:
        st_refs = refs[k:k + keep]
        k += keep
    z_ref, xbc_ref, dt_ref, us_ref, xn_ref, aux_ref = refs[k:k + 6]
    i = pl.program_id(0)
    j = pl.program_id(1)
    tm = x_ref.shape[0]

    @pl.when(j == 0)
    def _():
        xn = _rms(x_ref[...], g_ref[...]).astype(BF)
        xn_ref[...] = xn
        dt_ref[...] = _softplus(_dot(xn, wdt_ref[...]) + dtb_ref[...])

    z = _dot(xn_ref[...], w_ref[...])

    @pl.when(j < nz)
    def _():
        z_ref[...] = z

    @pl.when(j >= nz)
    def _():
        if stepwise:
            _save_seq_tails(z, us_ref, 0, aux_ref, seq, keep)
            xbc_ref[...] = _silu(_causal_conv(z, cw_ref, cb_ref, st_refs, seq))
        else:
            jx = j - nz

            @pl.when(i == 0)
            def _():
                aux_ref[jx] = jnp.zeros(aux_ref.shape[1:], F32)

            prev = aux_ref[jx]
            aux_ref[jx] = z[tm - SUBLANES:, :]
            us_ref[...] = z[tm - SUBLANES:, :]
            xbc_ref[...] = _silu(_causal_conv(z, cw_ref, cb_ref, prev, None))


def _ssm_in(x, g, w, wdt, dtb, cw, cb, state, di, cch, seq):
    m, d = x.shape
    tm = _tile(m, PROJ_ROWS)
    tn = min(_tile(di, 512), _tile(cch, 512))
    nz, nx = di // tn, cch // tn
    width = cw.shape[0]
    keep = width - 1
    stepwise = state is not None
    xcol = lambda i, j: (0, jnp.clip(j - nz, 0, nx - 1))
    in_specs = [
        pl.BlockSpec((tm, d), lambda i, j: (i, 0)),
        pl.BlockSpec((1, d), lambda i, j: (0, 0)),
        pl.BlockSpec((d, tn), lambda i, j: (0, j)),
        pl.BlockSpec((d, LANES), lambda i, j: (0, 0)),
        pl.BlockSpec((1, LANES), lambda i, j: (0, 0)),
        pl.BlockSpec((width, tn), xcol),
        pl.BlockSpec((1, tn), xcol),
    ]
    args = [x, g, w.astype(BF), wdt, dtb, cw, cb.reshape(1, cch)]
    if stepwise:
        in_specs += [pl.BlockSpec((tm // seq, tn), lambda i, j: (i, jnp.clip(j - nz, 0, nx - 1)))] * keep
        args += [state[:, r, :] for r in range(keep)]
        us_shape = jax.ShapeDtypeStruct((keep, nx, m // seq, tn), F32)
        us_spec = pl.BlockSpec((keep, None, tm // seq, tn), lambda i, j: (0, jnp.clip(j - nz, 0, nx - 1), i, 0))
        aux = pltpu.VMEM((tm, LANES), F32)
    else:
        us_shape = jax.ShapeDtypeStruct((m // tm, nx, SUBLANES, tn), F32)
        us_spec = pl.BlockSpec((None, None, SUBLANES, tn), lambda i, j: (i, jnp.clip(j - nz, 0, nx - 1), 0, 0))
        aux = pltpu.VMEM((nx, SUBLANES, tn), F32)
    scratch = [pltpu.VMEM((tm, d), BF), aux]
    z, xbc, dt, us = pl.pallas_call(
        functools.partial(_ssm_in_body, nz=nz, stepwise=stepwise, seq=seq),
        grid=(m // tm, nz + nx),
        in_specs=in_specs,
        out_specs=[
            pl.BlockSpec((tm, tn), lambda i, j: (i, jnp.clip(j, 0, nz - 1))),
            pl.BlockSpec((tm, tn), lambda i, j: (i, jnp.clip(j - nz, 0, nx - 1))),
            pl.BlockSpec((tm, LANES), lambda i, j: (i, 0)),
            us_spec,
        ],
        out_shape=[
            jax.ShapeDtypeStruct((m, di), F32),
            jax.ShapeDtypeStruct((m, cch), F32),
            jax.ShapeDtypeStruct((m, LANES), F32),
            us_shape,
        ],
        scratch_shapes=scratch,
        compiler_params=_cparams("arbitrary", "arbitrary"),
        name="ssm_in",
    )(*args)
    if stepwise:
        conv_state = jnp.transpose(us, (2, 0, 1, 3)).reshape(m // seq, keep, cch)
    else:
        conv_state = jnp.swapaxes(us[m // tm - 1], 0, 1).reshape(SUBLANES, cch)[None, SUBLANES - keep:]
    return z, xbc, dt, conv_state


def _gate_norm(y, z, ng):
    yz = y * _silu(z)
    return yz * lax.rsqrt(jnp.mean(yz * yz, axis=-1, keepdims=True) + EPS) * ng


def _ssd_seq_body(xbc_ref, dt_ref, z_ref, alog_ref, dsk_ref, ng_ref, y_ref, hout_ref, ht_ref,
                  *, nheads, groups, hd, ns):
    c = pl.program_id(0)
    cs = dt_ref.shape[0]
    hpg = nheads // groups
    gp = hpg * hd
    di = nheads * hd

    @pl.when(c == 0)
    def _():
        ht_ref[...] = jnp.zeros_like(ht_ref)

    dt = dt_ref[...]
    a_neg = jnp.where(_iota((1, LANES), 1) < nheads, -jnp.exp(alog_ref[...]), 0.0)
    acum = _sel_left(_tri(cs), dt * a_neg)
    acum_t = acum.T
    dt_t = dt.T
    tot = acum[cs - 1:cs, :]
    ex = (_iota((LANES, di), 1) // hd == _iota((LANES, di), 0)).astype(BF)
    coefx = _sel_right(jnp.exp(tot - acum) * dt, ex, parts=2)
    cdx = _sel_right(jnp.broadcast_to(jnp.exp(tot), (SUBLANES, LANES)), ex, parts=2)[0:1, :]
    causal = _iota((cs, cs), 1) <= _iota((cs, cs), 0)

    for g in range(groups):
        xs = xbc_ref[:, g * gp:(g + 1) * gp]
        bm = xbc_ref[:, di + g * ns:di + (g + 1) * ns]
        cm = xbc_ref[:, di + (groups + g) * ns:di + (groups + g + 1) * ns]
        cb = _dot_nt(cm.astype(BF), bm.astype(BF))
        ht = ht_ref[g]
        htb = ht.astype(BF)
        xb = xs.astype(BF)
        ys = []
        for r in range(hpg):
            hh = g * hpg + r
            acol = acum[:, hh:hh + 1]
            seg = jnp.broadcast_to(acol, (cs, cs)) - acum_t[hh:hh + 1, :]
            mp = cb * jnp.exp(jnp.where(causal, seg, NEG_INF)) * dt_t[hh:hh + 1, :]
            csc = cm * jnp.exp(jnp.broadcast_to(acol, (cs, ns)))
            lhs = jnp.concatenate([mp, csc], axis=1).astype(BF)
            rhs = jnp.concatenate([xb[:, r * hd:(r + 1) * hd], htb[:, r * hd:(r + 1) * hd]], axis=0)
            ys.append(_dot(lhs, rhs))
        yg = jnp.concatenate(ys, axis=1) + dsk_ref[:, g * gp:(g + 1) * gp] * xs
        y_ref[:, g * gp:(g + 1) * gp] = _gate_norm(
            yg, z_ref[:, g * gp:(g + 1) * gp], ng_ref[:, g * gp:(g + 1) * gp]).astype(BF)
        wg = (coefx[:, g * gp:(g + 1) * gp] * xs).astype(BF)
        ht_ref[g] = ht * cdx[:, g * gp:(g + 1) * gp] + _dot(bm.T.astype(BF), wg)

    @pl.when(c == pl.num_programs(0) - 1)
    def _():
        hout_ref[...] = ht_ref[...]


def _ssd_seq(xbc, dt, z, alog, dsk, ng, nheads, groups, hd, ns, chunk):
    l, cch = xbc.shape
    di = nheads * hd
    gp = di // groups
    return pl.pallas_call(
        functools.partial(_ssd_seq_body, nheads=nheads, groups=groups, hd=hd, ns=ns),
        grid=(l // chunk,),
        in_specs=[
            pl.BlockSpec((chunk, cch), lambda c: (c, 0)),
            pl.BlockSpec((chunk, LANES), lambda c: (c, 0)),
            pl.BlockSpec((chunk, di), lambda c: (c, 0)),
            pl.BlockSpec((1, LANES), lambda c: (0, 0)),
            pl.BlockSpec((1, di), lambda c: (0, 0)),
            pl.BlockSpec((1, di), lambda c: (0, 0)),
        ],
        out_specs=[pl.BlockSpec((chunk, di), lambda c: (c, 0)),
                   pl.BlockSpec((groups, ns, gp), lambda c: (0, 0, 0))],
        out_shape=[jax.ShapeDtypeStruct((l, di), BF),
                   jax.ShapeDtypeStruct((groups, ns, gp), F32)],
        scratch_shapes=[pltpu.VMEM((groups, ns, gp), F32)],
        compiler_params=_cparams("arbitrary"),
        name="ssd_seq",
    )(xbc, dt, z, alog, dsk, ng)


def _ssd_step_body(xbc_ref, dt_ref, z_ref, alog_ref, alogc_ref, dsk_ref, ng_ref, h0_ref, y_ref, hn_ref,
                   *, nheads, groups, hd, ns):
    t = dt_ref.shape[0]
    hpg = nheads // groups
    gp = hpg * hd
    di = nheads * hd
    gn = groups * ns

    dt = dt_ref[...]
    a_neg = jnp.where(_iota((1, LANES), 1) < nheads, -jnp.exp(alog_ref[...]), 0.0)
    rt = _iota((t, LANES), 0)
    acum = dt * a_neg
    sh = 1
    while sh < t:
        acum = acum + jnp.where(rt >= sh, pltpu.roll(acum, sh, 0), 0.0)
        sh *= 2
    tot = acum[t - 1:t, :]
    coef = jnp.exp(tot - acum) * dt
    eac = jnp.exp(acum)

    xs = xbc_ref[:, 0:di]
    bm = xbc_ref[:, di:di + gn]
    cm = xbc_ref[:, di + gn:di + 2 * gn]

    prods = jnp.concatenate([cm * bm[s:s + 1, :] for s in range(t)], axis=0)
    rsel = ((_iota((gn, LANES), 1) // hpg == _iota((gn, LANES), 0) // ns)
            & (_iota((gn, LANES), 1) < nheads)).astype(BF)
    cbe = _dot(prods.astype(BF), rsel)
    acl = jnp.concatenate([acum] * t, axis=0)
    acs = jnp.concatenate([jnp.broadcast_to(acum[s:s + 1, :], (t, LANES)) for s in range(t)], axis=0)
    dts = jnp.concatenate([jnp.broadcast_to(dt[s:s + 1, :], (t, LANES)) for s in range(t)], axis=0)
    rr = _iota((t * t, LANES), 0)
    ms = cbe * jnp.exp(jnp.where(rr % t >= rr // t, acl - acs, NEG_INF)) * dts
    ex = (_iota((LANES, di), 1) // hd == _iota((LANES, di), 0)).astype(BF)
    big = _sel_right(jnp.concatenate([ms, eac, coef], axis=0), ex, parts=2)
    y = big[0:t, :] * xs[0:1, :]
    for s in range(1, t):
        y = y + big[s * t:(s + 1) * t, :] * xs[s:s + 1, :]
    eacx = big[t * t:t * t + t, :]
    coefx = big[t * t + t:, :]

    dt_t = jnp.concatenate([dt, jnp.zeros((LANES - t, LANES), F32)], axis=0).T
    a_col = -jnp.exp(alogc_ref[...])
    cd_col = jnp.exp(jnp.sum(dt_t * a_col, axis=1, keepdims=True))

    for g in range(groups):
        gs = slice(g * gp, (g + 1) * gp)
        h0 = h0_ref[gs, :]
        yoff = _dot_nt(cm[:, g * ns:(g + 1) * ns].astype(BF), h0.astype(BF))
        yg = y[:, gs] + yoff * eacx[:, gs] + dsk_ref[:, gs] * xs[:, gs]
        y_ref[:, gs] = _gate_norm(yg, z_ref[:, gs], ng_ref[:, gs]).astype(BF)
        wg = (coefx[:, gs] * xs[:, gs]).astype(BF)
        wpad = jnp.concatenate([wg.astype(F32), jnp.zeros((LANES - t, gp), F32)], axis=0)
        wt = jnp.concatenate([wpad[:, k * LANES:(k + 1) * LANES].T for k in range(gp // LANES)], axis=0)
        bpad = jnp.concatenate([bm[:, g * ns:(g + 1) * ns], jnp.zeros((LANES - t, ns), F32)], axis=0)
        upd = _dot(wt.astype(BF), bpad.astype(BF))
        for r in range(hpg):
            hh = g * hpg + r
            rs = slice(r * hd, (r + 1) * hd)
            hn_ref[g * gp + r * hd:g * gp + (r + 1) * hd, :] = (
                h0[rs, :] * jnp.broadcast_to(cd_col[hh:hh + 1, :], (hd, ns)) + upd[rs, :])


def _ssd_step(xbc, dt, z, alog, alogc, dsk, ng, h0, nheads, groups, hd, ns):
    nb = h0.shape[0]
    m, cch = xbc.shape
    t = m // nb
    di = nheads * hd
    return pl.pallas_call(
        functools.partial(_ssd_step_body, nheads=nheads, groups=groups, hd=hd, ns=ns),
        grid=(nb,),
        in_specs=[
            pl.BlockSpec((t, cch), lambda b: (b, 0)),
            pl.BlockSpec((t, LANES), lambda b: (b, 0)),
            pl.BlockSpec((t, di), lambda b: (b, 0)),
            pl.BlockSpec((1, LANES), lambda b: (0, 0)),
            pl.BlockSpec((LANES, 1), lambda b: (0, 0)),
            pl.BlockSpec((1, di), lambda b: (0, 0)),
            pl.BlockSpec((1, di), lambda b: (0, 0)),
            pl.BlockSpec((None, di, ns), lambda b: (b, 0, 0)),
        ],
        out_specs=[pl.BlockSpec((t, di), lambda b: (b, 0)),
                   pl.BlockSpec((None, di, ns), lambda b: (b, 0, 0))],
        out_shape=[jax.ShapeDtypeStruct((m, di), BF),
                   jax.ShapeDtypeStruct((nb, di, ns), F32)],
        compiler_params=_cparams("parallel"),
        name="ssd_step",
    )(xbc, dt, z, alog, alogc, dsk, ng, h0)


def _pad_lanes(a):
    return jnp.pad(a, [(0, 0)] * (a.ndim - 1) + [(0, LANES - a.shape[-1])])


def _trunk(x, nb, page_table, cache_k, cache_v, cache_lf, conf_prev, mconv_prev, ssm_prev, ffn_prev, w):
    _, t, d = x.shape
    m = nb * t
    stepwise = conf_prev is not None
    nh, dh = cache_k.shape[-2:]
    fw = nh * dh
    x = x.reshape(m, d)

    w_in = w["att_w_in"][0]
    cc = w["conf_dw_w"].shape[2]
    cw = w["conf_dw_w"].shape[1]
    w5 = jnp.concatenate([w_in[:, :3 * fw], w_in[:, 3 * fw + nh:]], axis=1)
    wf = _pad_lanes(w_in[:, 3 * fw:3 * fw + nh]).astype(BF)
    bfp = _pad_lanes(w["att_b_f"][0].reshape(1, nh))
    q, k, v, u, lf = _even_in(x, w["ln_mix"][0].reshape(1, d), w5, wf, bfp, fw, cc)
    logf = lf[:, :nh]
    if stepwise:
        npool, page = cache_k.shape[1:3]
        att = _fox_sample(q, k, v, lf,
                          cache_k[0].reshape(npool, page * nh, dh), cache_v[0].reshape(npool, page * nh, dh),
                          cache_lf[0].reshape(npool, page * nh // LANES, LANES), page_table, nh, dh)
        xp = jnp.concatenate([conf_prev[0], u.reshape(nb, t, cc)], axis=1)
        conf_state = xp[:, t:]
        cv = _conv_step(xp, w["conf_dw_w"][0], w["conf_dw_b"][0]).reshape(m, cc)
    else:
        cq, ckt = _fox_cumsum(lf, nh)
        att = _fox_prompt(q, k, v, cq, ckt.reshape(nh, 1, m), nh, dh)
        conf_state = u[m - (cw - 1):].reshape(1, cw - 1, cc)
        cv = _conv_seq(u, w["conf_dw_w"][0], w["conf_dw_b"][0])
    w_out = w["att_w_out"][0]
    x = _out_even(x, att, cv, w["conf_ln_g"][0], w["conf_ln_b"][0], w_out[:fw], w_out[fw:])

    ffn_states = []

    def ffn(x, i, final_g):
        if stepwise:
            two_f = ffn_prev.shape[-1]
            st = ffn_prev[i]
        else:
            st = None
        x, us = _ffn(x, w["ln_ffn"][i], w["ffn_w_up"], w["ffn_dw_w"][i], w["ffn_dw_b"][i],
                     w["ffn_w_down"], i, st, final_g, t)
        ffn_states.append(us)
        return x

    x = ffn(x, 0, None)

    nheads, hd, ns = w["ssm_state_shape"]
    di = nheads * hd
    cch = w["ssm_conv_w"].shape[2]
    groups = (cch - di) // (2 * ns)
    w_in = w["ssm_w_in"][0]
    wdt = _pad_lanes(w_in[:, di + cch:]).astype(BF)
    dtb = _pad_lanes(w["ssm_dt_bias"][0].reshape(1, nheads))
    sw = w["ssm_conv_w"].shape[1]
    if stepwise:
        st = mconv_prev[0]
    else:
        st = None
    z, xbc, dt, mconv_state = _ssm_in(x, w["ln_mix"][1].reshape(1, d), w_in, wdt, dtb,
                                      w["ssm_conv_w"][0], w["ssm_conv_b"][0], st, di, cch, t)
    alog = _pad_lanes(w["ssm_a_log"][0].reshape(1, nheads))
    dsk = jnp.repeat(w["ssm_d"][0], hd).reshape(1, di)
    ng = w["ssm_norm_g"][0].reshape(1, di)
    if stepwise:
        yn, hn = _ssd_step(xbc, dt, z, alog, alog.reshape(LANES, 1), dsk, ng,
                           ssm_prev[0].reshape(nb, di, ns), nheads, groups, hd, ns)
        ssm_state = hn.reshape(nb, nheads, hd, ns)
    else:
        chunk = _tile(m, 128)
        yn, hout = _ssd_seq(xbc, dt, z, alog, dsk, ng, nheads, groups, hd, ns, chunk)
        hpg = nheads // groups
        ssm_state = jnp.transpose(hout.reshape(groups, ns, hpg, hd), (0, 2, 3, 1)).reshape(1, nheads, hd, ns)
    x = _mm_res(x, yn, w["ssm_w_out"][0])
    x = ffn(x, 1, w["ln_final"])

    return (x.reshape(nb, t, d),
            k.reshape(1, nb, t, nh, dh), v.reshape(1, nb, t, nh, dh), logf.reshape(1, nb, t, nh),
            conf_state[None], mconv_state[None], ssm_state[None], jnp.stack(ffn_states))


def kernel(x_prompt, x_sample, cache_k, cache_v, cache_logf, state_conf_conv, state_ssm_conv, state_ssm,
           state_ffn_conv, page_table,
           ln_mix, ln_ffn, ln_final,
           att_w_in, att_b_f, conf_dw_w, conf_dw_b, conf_ln_g, conf_ln_b, att_w_out,
           ssm_w_in, ssm_conv_w, ssm_conv_b, ssm_dt_bias, ssm_a_log, ssm_d, ssm_norm_g, ssm_w_out,
           ffn_w_up, ffn_dw_w, ffn_dw_b, ffn_w_down):
    assert ln_mix.shape[0] == 2 and x_prompt.shape[0] == 1, "two layers, one prompt sequence"
    w = dict(ln_mix=ln_mix, ln_ffn=ln_ffn, ln_final=ln_final,
             att_w_in=att_w_in, att_b_f=att_b_f, conf_dw_w=conf_dw_w, conf_dw_b=conf_dw_b,
             conf_ln_g=conf_ln_g, conf_ln_b=conf_ln_b, att_w_out=att_w_out,
             ssm_w_in=ssm_w_in, ssm_conv_w=ssm_conv_w, ssm_conv_b=ssm_conv_b, ssm_dt_bias=ssm_dt_bias,
             ssm_a_log=ssm_a_log, ssm_d=ssm_d, ssm_norm_g=ssm_norm_g, ssm_w_out=ssm_w_out,
             ffn_w_up=ffn_w_up, ffn_dw_w=ffn_dw_w, ffn_dw_b=ffn_dw_b, ffn_w_down=ffn_w_down,
             ssm_state_shape=state_ssm.shape[2:])
    nb = x_sample.shape[0]
    outs_p = _trunk(x_prompt, 1, None, cache_k, cache_v, cache_logf, None, None, None, None, w)
    outs_s = _trunk(x_sample, nb, page_table, cache_k, cache_v, cache_logf,
                    state_conf_conv, state_ssm_conv, state_ssm, state_ffn_conv, w)
    return (outs_p[0], outs_s[0]) + tuple(outs_p[1:]) + tuple(outs_s[1:])
```
